```python
import math
import numpy as np
import jax
import jax.numpy as jnp
from jax import lax

D_MODEL = 1024
BATCH = 16
SEQ = 2048
DEPTH = 2
DEC_BATCH = 32
DEC_SEQ = 4
PAST_LEN = 16384
PAGE_SIZE = 128

MIX_WIDTH = D_MODEL
GROUP_WIDTH = MIX_WIDTH // 4
CONV_CH = GROUP_WIDTH
CONV_WIDTH = 31
POOL_CH = GROUP_WIDTH
POOL_WINDOWS = (2, 4, 8, 16)
POOL_GROUP = POOL_CH // len(POOL_WINDOWS)
POOL_BUF = max(POOL_WINDOWS) - 1
ATT_HEADS = 4
ATT_KV_HEADS = 2
ATT_GROUP = ATT_HEADS // ATT_KV_HEADS
HEAD_DIM = GROUP_WIDTH // ATT_HEADS
ATT_SCALE = HEAD_DIM ** -0.5
CMP_BLOCK = 32
CMP_STRIDE = 16
SEL_BLOCK = 64
SEL_TOPN = 16
WINDOW = 512
SEL_QBLOCK = 16
WIN_QBLOCK = 128
N_BUCKETS = 32
MAX_DISTANCE = 128
GDN_HEADS = 4
GDN_DK = GROUP_WIDTH // GDN_HEADS
GDN_DV = GROUP_WIDTH // GDN_HEADS
GDN_QKV = GDN_HEADS * (2 * GDN_DK + GDN_DV)
GDN_CONV = 4
GDN_CHUNK = 64
D_FF = 4 * D_MODEL
DN_ALPHA = (2 * DEPTH) ** 0.25
DN_BETA = (8 * DEPTH) ** -0.25
LN_EPS = 1e-5
NEG = -1e30
FORCE = 1e4
SPLITS = (2 * CONV_CH, POOL_CH, ATT_HEADS * HEAD_DIM, 6 * ATT_KV_HEADS * HEAD_DIM, 3 * ATT_HEADS,
          GDN_QKV, GDN_HEADS * GDN_DV, GDN_HEADS, GDN_HEADS)
IN_WIDTH = sum(SPLITS)
F32 = jnp.float32

kernel_name = 'hymba_conv_pool_nsa_gdn_decoder_step'


def split_cols(h, sizes):
    out, start = [], 0
    for s in sizes:
        out.append(h[..., start:start + s])
        start += s
    return out


def layer_norm(x, g, b):
    xf = x.astype(F32)
    mu = xf.mean(-1, keepdims=True)
    var = jnp.square(xf - mu).mean(-1, keepdims=True)
    return ((xf - mu) * lax.rsqrt(var + LN_EPS) * g + b).astype(x.dtype)


def rms_norm(x, g):
    xf = x.astype(F32)
    return xf * lax.rsqrt(jnp.square(xf).mean(-1, keepdims=True) + LN_EPS) * g


def l2_normalize(x):
    return x * lax.rsqrt(jnp.sum(jnp.square(x), -1, keepdims=True) + 1e-6)


def masked_softmax(s, mask):
    s = jnp.where(mask, s.astype(F32), NEG)
    e = jnp.where(mask, jnp.exp(s - s.max(-1, keepdims=True)), 0.0)
    return e / jnp.maximum(e.sum(-1, keepdims=True), 1e-30)


def causal_dwconv(u, w):
    return lax.conv_general_dilated(u, w[:, None, :].astype(u.dtype), (1,), 'VALID',
                                    dimension_numbers=('NWC', 'WIO', 'NWC'),
                                    feature_group_count=u.shape[-1])


def t5_bucket(d):
    d = jnp.maximum(d, 0)
    exact = N_BUCKETS // 2
    logd = jnp.log(jnp.maximum(d, 1).astype(F32) / exact) / math.log(MAX_DISTANCE / exact)
    large = jnp.minimum(exact + (logd * (N_BUCKETS - exact)).astype(jnp.int32), N_BUCKETS - 1)
    return jnp.where(d < exact, d, large)


def head_bias(rel_bias, d):
    b = rel_bias[t5_bucket(d)].astype(F32)
    return jnp.moveaxis(b, -1, 0).reshape((ATT_KV_HEADS, ATT_GROUP) + d.shape)


def conformer_conv(h, buf, dw, dw_b, ln_g, ln_b, pw):
    u = h[..., :CONV_CH] * jax.nn.sigmoid(h[..., CONV_CH:])
    full = jnp.concatenate([buf.astype(u.dtype), u], axis=1)
    y = causal_dwconv(full, dw) + dw_b
    y = jax.nn.silu(layer_norm(y, ln_g, ln_b))
    return y @ pw, full[:, -(CONV_WIDTH - 1):]


def pool_mixer(h, buf, offset, pool_w, pool_scale):
    B, T, C = h.shape
    u = jnp.concatenate([buf.astype(h.dtype), h], axis=1)
    uf = u.astype(F32)
    cs = jnp.concatenate([jnp.zeros((B, 1, C), F32), jnp.cumsum(uf, axis=1)], axis=1)
    hi = cs[:, POOL_BUF + 1:POOL_BUF + 1 + T]
    pos = offset + jnp.arange(T)
    parts = []
    for gi, w in enumerate(POOL_WINDOWS):
        sl = slice(gi * POOL_GROUP, (gi + 1) * POOL_GROUP)
        lo = cs[:, POOL_BUF + 1 - w:POOL_BUF + 1 - w + T, sl]
        cnt = jnp.minimum(pos + 1, w).astype(F32)[None, :, None]
        parts.append((hi[..., sl] - lo) / cnt - uf[:, POOL_BUF:, sl])
    d = jnp.stack(parts, axis=2)
    y = jnp.einsum('btgc,gcd->btgd', d, pool_w.astype(F32)).reshape(B, T, C) * pool_scale
    return y, u[:, -POOL_BUF:]


def compress_blocks(kseq, pe, w1, w2):
    B, L, KH, HD = kseq.shape
    nc = -(-L // CMP_STRIDE)
    kp = jnp.pad(kseq, ((0, 0), (0, nc * CMP_STRIDE - L), (0, 0), (0, 0)))
    ch = kp.reshape(B, nc, CMP_STRIDE, KH, HD).transpose(0, 1, 3, 2, 4).reshape(B, nc, KH, CMP_STRIDE * HD)
    half = CMP_STRIDE * HD
    ha = ch @ w1[:half]
    hb = ch @ w1[half:]
    h = ha[:, :-1] + hb[:, 1:] + pe.reshape(-1) @ w1
    return jax.nn.silu(h) @ w2


def nsa_selected(qg, k_sel, v_sel, top_idx, top_ok, qpos, rel_bias):
    B, T, KH, G, HD = qg.shape
    L = k_sel.shape[1]
    n_sel = -(-L // SEL_BLOCK)
    padw = ((0, 0), (0, n_sel * SEL_BLOCK - L), (0, 0), (0, 0))
    kt = jnp.pad(k_sel, padw).transpose(0, 2, 1, 3)
    vt = jnp.pad(v_sel, padw).transpose(0, 2, 1, 3)
    n = top_idx.shape[-1]
    qb = math.gcd(T, SEL_QBLOCK)
    nb = T // qb
    bi = jnp.arange(B)[:, None, None, None]
    hi = jnp.arange(KH)[None, :, None, None]
    bias_tab = rel_bias.reshape(N_BUCKETS, KH, G)

    def block(args):
        q_b, idx, ok, pb = args
        tok = (idx[..., None] * SEL_BLOCK + jnp.arange(SEL_BLOCK)).reshape(B, KH, qb, n * SEL_BLOCK)
        okt = jnp.repeat(ok, SEL_BLOCK, axis=-1)
        kg = kt[bi, hi, tok]
        vg = vt[bi, hi, tok]
        d = pb[None, None, :, None] - tok
        mask = okt & (d >= 0)
        bias = jnp.moveaxis(bias_tab[t5_bucket(d), hi].astype(F32), -1, 2)
        s = jnp.einsum('bqkgd,bkqmd->bkgqm', q_b, kg).astype(F32) * ATT_SCALE + bias
        p = masked_softmax(s, mask[:, :, None])
        return jnp.einsum('bkgqm,bkqmd->bqkgd', p, vg.astype(F32))

    xs = (qg.reshape(B, nb, qb, KH, G, HD).swapaxes(0, 1),
          jnp.moveaxis(top_idx.reshape(B, KH, nb, qb, n), 2, 0),
          jnp.moveaxis(top_ok.reshape(B, KH, nb, qb, n), 2, 0),
          qpos.reshape(nb, qb))
    o = lax.map(block, xs)
    return o.swapaxes(0, 1).reshape(B, T, KH, G, HD)


def nsa_window(qg, kw_all, key_pos0, qpos, rel_bias):
    B, T, KH, G, HD = qg.shape
    Lk = kw_all.shape[1]
    Pw = Lk - T
    qb = math.gcd(T, WIN_QBLOCK)
    nb = T // qb
    band = WINDOW - 1 + qb
    kw_pad = jnp.pad(kw_all, ((0, 0), (WINDOW - 1, 0), (0, 0), (0, 0), (0, 0)))
    kpos = key_pos0 - (WINDOW - 1) + jnp.arange(Lk + WINDOW - 1)
    real = jnp.arange(Lk + WINDOW - 1) >= WINDOW - 1

    def block(args):
        q_b, start, pb = args
        kv = lax.dynamic_slice_in_dim(kw_pad, start, band, axis=1)
        kp = lax.dynamic_slice_in_dim(kpos, start, band)
        rl = lax.dynamic_slice_in_dim(real, start, band)
        d = pb[:, None] - kp[None, :]
        mask = rl[None, :] & (d >= 0) & (d < WINDOW)
        s = jnp.einsum('bqkgd,bmkd->bkgqm', q_b, kv[:, :, 0]).astype(F32) * ATT_SCALE + head_bias(rel_bias, d)
        p = masked_softmax(s, mask)
        return jnp.einsum('bkgqm,bmkd->bqkgd', p, kv[:, :, 1].astype(F32))

    xs = (qg.reshape(B, nb, qb, KH, G, HD).swapaxes(0, 1), Pw + jnp.arange(nb) * qb, qpos.reshape(nb, qb))
    o = lax.map(block, xs)
    return o.swapaxes(0, 1).reshape(B, T, KH, G, HD)


def nsa_mixer(q, kv_new, gate_logits, past, win_buf, cmp_pe, cmp_w1, cmp_w2, rel_bias):
    B, T = q.shape[:2]
    offset = past[0].shape[1]
    L = offset + T
    qpos = offset + jnp.arange(T)
    qg = q.reshape(B, T, ATT_KV_HEADS, ATT_GROUP, HEAD_DIM)
    k_cmp, v_cmp, k_sel, v_sel = [jnp.concatenate([past[c].astype(kv_new.dtype), kv_new[:, :, c]], axis=1)
                                  for c in range(4)]
    kc = compress_blocks(k_cmp, cmp_pe[0], cmp_w1[0], cmp_w2[0])
    vc = compress_blocks(v_cmp, cmp_pe[1], cmp_w1[1], cmp_w2[1])
    cmp_start = jnp.arange(kc.shape[1]) * CMP_STRIDE
    d_cmp = qpos[:, None] - (cmp_start + CMP_BLOCK - 1)[None, :]
    s = jnp.einsum('btkgd,bnkd->bkgtn', qg, kc).astype(F32) * ATT_SCALE + head_bias(rel_bias, d_cmp)
    p_cmp = masked_softmax(s, d_cmp >= 0)
    o_cmp = jnp.einsum('bkgtn,bnkd->btkgd', p_cmp, vc.astype(F32))
    n_sel = -(-L // SEL_BLOCK)
    sel_start = jnp.arange(n_sel) * SEL_BLOCK
    cover = ((cmp_start[:, None] < sel_start[None, :] + SEL_BLOCK)
             & (cmp_start[:, None] + CMP_BLOCK > sel_start[None, :])).astype(F32)
    imp = jnp.einsum('bkgtn,nj->bktj', p_cmp, cover)
    cur = (qpos // SEL_BLOCK)[:, None]
    j = jnp.arange(n_sel)[None, :]
    forced = (j == 0) | (j == cur) | (j == cur - 1)
    avail = sel_start[None, :] <= qpos[:, None]
    score = jnp.where(avail, imp + jnp.where(forced, FORCE, 0.0), NEG)
    top_val, top_idx = lax.top_k(score, min(SEL_TOPN, n_sel))
    top_ok = top_val > 0.5 * NEG
    o_sel = nsa_selected(qg, k_sel, v_sel, top_idx, top_ok, qpos, rel_bias)
    kw_all = jnp.concatenate([win_buf.astype(kv_new.dtype), kv_new[:, :, 4:]], axis=1)
    o_win = nsa_window(qg, kw_all, offset - win_buf.shape[1], qpos, rel_bias)
    gates = jax.nn.sigmoid(gate_logits.astype(F32)).reshape(B, T, 3, ATT_KV_HEADS, ATT_GROUP, 1)
    o = gates[:, :, 0] * o_cmp + gates[:, :, 1] * o_sel + gates[:, :, 2] * o_win
    return o.reshape(B, T, ATT_HEADS * HEAD_DIM), kw_all[:, -min(WINDOW, kw_all.shape[1]):]


def chunk_gated_delta(q, k, v, g, beta, S0):
    B, T, H, dk = q.shape
    dv = v.shape[-1]
    C = min(GDN_CHUNK, T)
    n = -(-T // C)
    pad = n * C - T

    def prep(a):
        a = jnp.pad(a, ((0, 0), (0, pad)) + ((0, 0),) * (a.ndim - 2))
        return jnp.moveaxis(a.reshape((B, n, C) + a.shape[2:]), (1, 3), (0, 2))

    qc, kc, vc, gc, bc = prep(q), prep(k), prep(v), prep(g), prep(beta)
    gcum = jnp.cumsum(gc, axis=-1)
    incl = jnp.tril(jnp.ones((C, C), bool))
    strict = jnp.tril(jnp.ones((C, C), bool), -1)
    decay = jnp.exp(jnp.where(incl, gcum[..., :, None] - gcum[..., None, :], NEG))
    kb = kc * bc[..., None]
    lower = jnp.where(strict, jnp.einsum('...id,...jd->...ij', kb, kc) * decay, 0.0)
    A = lower + jnp.eye(C, dtype=F32)
    u_val = lax.linalg.triangular_solve(A, vc * bc[..., None], left_side=True, lower=True, unit_diagonal=True)
    w_dec = lax.linalg.triangular_solve(A, kb * jnp.exp(gcum)[..., None], left_side=True, lower=True,
                                        unit_diagonal=True)
    qk = jnp.einsum('...id,...jd->...ij', qc, kc) * decay

    def step(S, xs):
        qi, ki, ui, wi, qki, gi = xs
        vnew = ui - wi @ S
        o = (qi * jnp.exp(gi)[..., None]) @ S + qki @ vnew
        g_last = gi[..., -1:]
        S = S * jnp.exp(g_last)[..., None] + jnp.einsum('bhcd,bhce->bhde', ki * jnp.exp(g_last - gi)[..., None], vnew)
        return S, o

    S, o = lax.scan(step, S0, (qc, kc, u_val, w_dec, qk, gcum))
    o = jnp.moveaxis(o, (0, 2), (1, 3)).reshape(B, n * C, H, dv)[:, :T]
    return o, S


def gdn_mixer(qkv, z, a, b, conv_buf, S0, conv_w, a_log, dt_bias, norm_g):
    B, T = qkv.shape[:2]
    u = jnp.concatenate([conv_buf.astype(qkv.dtype), qkv], axis=1)
    c = jax.nn.silu(causal_dwconv(u, conv_w).astype(F32))
    q, k, v = split_cols(c, (GDN_HEADS * GDN_DK, GDN_HEADS * GDN_DK, GDN_HEADS * GDN_DV))
    q = l2_normalize(q.reshape(B, T, GDN_HEADS, GDN_DK)) * GDN_DK ** -0.5
    k = l2_normalize(k.reshape(B, T, GDN_HEADS, GDN_DK))
    v = v.reshape(B, T, GDN_HEADS, GDN_DV)
    g = -jnp.exp(a_log.astype(F32)) * jax.nn.softplus(a.astype(F32) + dt_bias)
    beta = jax.nn.sigmoid(b.astype(F32))
    o, S = chunk_gated_delta(q, k, v, g, beta, S0.astype(F32))
    o = rms_norm(o, norm_g) * jax.nn.silu(z.astype(F32)).reshape(B, T, GDN_HEADS, GDN_DV)
    return o.reshape(B, T, GDN_HEADS * GDN_DV), u[:, -(GDN_CONV - 1):], S


def decoder_layer(x, past, win_buf, conv_buf, pool_buf, gdn_buf, gdn_S,
                  w_in, conv_dw, conv_dw_b, conv_ln_g, conv_ln_b, conv_pw, pool_w, pool_scale,
                  cmp_pe, cmp_w1, cmp_w2, gdn_conv_w, gdn_a_log, gdn_dt_bias, gdn_norm_g,
                  w_out, ln1_g, ln1_b, w_up, w_down, ln2_g, ln2_b, rel_bias):
    B, T, _ = x.shape
    offset = past[0].shape[1]
    h = x @ w_in
    h_conv, h_pool, a_q, a_kv, a_gate, d_qkv, d_z, d_a, d_b = split_cols(h, SPLITS)
    y_conv, conv_new = conformer_conv(h_conv, conv_buf, conv_dw, conv_dw_b, conv_ln_g, conv_ln_b, conv_pw)
    y_pool, pool_new = pool_mixer(h_pool, pool_buf, offset, pool_w, pool_scale)
    kv_new = a_kv.reshape(B, T, 6, ATT_KV_HEADS, HEAD_DIM)
    y_att, win_new = nsa_mixer(a_q, kv_new, a_gate, past, win_buf, cmp_pe, cmp_w1, cmp_w2, rel_bias)
    y_gdn, gdn_buf_new, S_new = gdn_mixer(d_qkv, d_z, d_a, d_b, gdn_buf, gdn_S, gdn_conv_w, gdn_a_log,
                                          gdn_dt_bias, gdn_norm_g)
    mix = jnp.concatenate([y_conv.astype(F32), y_pool, y_att, y_gdn], axis=-1).astype(x.dtype) @ w_out
    x1 = layer_norm(DN_ALPHA * x + mix, ln1_g, ln1_b)
    f = jnp.square(jax.nn.relu(x1 @ w_up)) @ w_down
    x2 = layer_norm(DN_ALPHA * x1 + f, ln2_g, ln2_b)
    return x2, kv_new[:, :, :4], win_new, conv_new, pool_new, gdn_buf_new, S_new


def setup_inputs(seed: int = 0) -> dict:
    key = jax.random.key(seed)
    ks = list(jax.random.split(key, 40))

    def nrm(i, shape, scale):
        return jax.random.normal(ks[i], shape, F32) * scale

    n_pages = PAST_LEN // PAGE_SIZE
    n_pool = (DEC_BATCH * n_pages * 5) // 4
    wbuf = min(WINDOW, PAST_LEN)
    page_table = jax.random.permutation(ks[8], n_pool)[:DEC_BATCH * n_pages].astype(jnp.int32).reshape(DEC_BATCH, n_pages)
    dt = jnp.exp(jax.random.uniform(ks[21], (DEPTH, GDN_HEADS), F32, math.log(1e-3), math.log(1e-1)))
    return {
        'x_prompt': nrm(0, (BATCH, SEQ, D_MODEL), 1.0),
        'x_sample': nrm(1, (DEC_BATCH, DEC_SEQ, D_MODEL), 1.0),
        'cache_nsa_kv': nrm(2, (DEPTH, n_pool, PAGE_SIZE, 4, ATT_KV_HEADS, HEAD_DIM), 1.0),
        'cache_win_kv': nrm(3, (DEPTH, DEC_BATCH, wbuf, 2, ATT_KV_HEADS, HEAD_DIM), 1.0),
        'state_conv': nrm(4, (DEPTH, DEC_BATCH, CONV_WIDTH - 1, CONV_CH), 0.5),
        'state_pool': nrm(5, (DEPTH, DEC_BATCH, POOL_BUF, POOL_CH), 1.0),
        'state_gdn_conv': nrm(6, (DEPTH, DEC_BATCH, GDN_CONV - 1, GDN_QKV), 1.0),
        'state_gdn': nrm(7, (DEPTH, DEC_BATCH, GDN_HEADS, GDN_DK, GDN_DV), 0.1),
        'page_table': page_table,
        'w_in': nrm(9, (DEPTH, D_MODEL, IN_WIDTH), D_MODEL ** -0.5),
        'conv_dw': nrm(10, (DEPTH, CONV_WIDTH, CONV_CH), CONV_WIDTH ** -0.5),
        'conv_dw_b': nrm(11, (DEPTH, CONV_CH), 0.02),
        'conv_ln_g': 1.0 + nrm(12, (DEPTH, CONV_CH), 0.05),
        'conv_ln_b': nrm(13, (DEPTH, CONV_CH), 0.02),
        'conv_pw': nrm(14, (DEPTH, CONV_CH, CONV_CH), CONV_CH ** -0.5),
        'pool_w': nrm(15, (DEPTH, len(POOL_WINDOWS), POOL_GROUP, POOL_GROUP), POOL_GROUP ** -0.5),
        'pool_scale': 1.0 + nrm(16, (DEPTH, POOL_CH), 0.1),
        'cmp_pe': nrm(17, (DEPTH, 2, CMP_BLOCK, HEAD_DIM), 0.1),
        'cmp_w1': nrm(18, (DEPTH, 2, CMP_BLOCK * HEAD_DIM, HEAD_DIM), (CMP_BLOCK * HEAD_DIM) ** -0.5),
        'cmp_w2': nrm(19, (DEPTH, 2, HEAD_DIM, HEAD_DIM), HEAD_DIM ** -0.5),
        'gdn_conv_w': nrm(20, (DEPTH, GDN_CONV, GDN_QKV), GDN_CONV ** -0.5),
        'gdn_a_log': jnp.log(jax.random.uniform(ks[22], (DEPTH, GDN_HEADS), F32, 1.0, 16.0)),
        'gdn_dt_bias': dt + jnp.log(-jnp.expm1(-dt)),
        'gdn_norm_g': 1.0 + nrm(23, (DEPTH, GDN_DV), 0.05),
        'w_out': nrm(24, (DEPTH, MIX_WIDTH, D_MODEL), MIX_WIDTH ** -0.5 * DN_BETA),
        'ln1_g': 1.0 + nrm(25, (DEPTH, D_MODEL), 0.05),
        'ln1_b': nrm(26, (DEPTH, D_MODEL), 0.02),
        'w_up': nrm(27, (DEPTH, D_MODEL, D_FF), D_MODEL ** -0.5),
        'w_down': nrm(28, (DEPTH, D_FF, D_MODEL), D_FF ** -0.5 * DN_BETA),
        'ln2_g': 1.0 + nrm(29, (DEPTH, D_MODEL), 0.05),
        'ln2_b': nrm(30, (DEPTH, D_MODEL), 0.02),
        'rel_bias': nrm(31, (N_BUCKETS, ATT_HEADS), 0.5),
    }


def reference(x_prompt, x_sample, cache_nsa_kv, cache_win_kv, state_conv, state_pool, state_gdn_conv, state_gdn,
              page_table, w_in, conv_dw, conv_dw_b, conv_ln_g, conv_ln_b, conv_pw, pool_w, pool_scale,
              cmp_pe, cmp_w1, cmp_w2, gdn_conv_w, gdn_a_log, gdn_dt_bias, gdn_norm_g,
              w_out, ln1_g, ln1_b, w_up, w_down, ln2_g, ln2_b, rel_bias):
    per_layer = (w_in, conv_dw, conv_dw_b, conv_ln_g, conv_ln_b, conv_pw, pool_w, pool_scale,
                 cmp_pe, cmp_w1, cmp_w2, gdn_conv_w, gdn_a_log, gdn_dt_bias, gdn_norm_g,
                 w_out, ln1_g, ln1_b, w_up, w_down, ln2_g, ln2_b)
    bp, dtp = x_prompt.shape[0], x_prompt.dtype
    db = x_sample.shape[0]
    n_pages = page_table.shape[1]
    page = cache_nsa_kv.shape[2]
    yp, ys = x_prompt, x_sample
    new_p = [[] for _ in range(6)]
    new_s = [[] for _ in range(6)]
    for l in range(DEPTH):
        lw = [w[l] for w in per_layer] + [rel_bias]
        past_p = tuple(jnp.zeros((bp, 0, ATT_KV_HEADS, HEAD_DIM), dtp) for _ in range(4))
        yp, *st_p = decoder_layer(
            yp, past_p, jnp.zeros((bp, 0, 2, ATT_KV_HEADS, HEAD_DIM), dtp),
            jnp.zeros((bp, CONV_WIDTH - 1, CONV_CH), dtp), jnp.zeros((bp, POOL_BUF, POOL_CH), dtp),
            jnp.zeros((bp, GDN_CONV - 1, GDN_QKV), dtp), jnp.zeros((bp, GDN_HEADS, GDN_DK, GDN_DV), F32), *lw)
        past_s = tuple(jnp.take(cache_nsa_kv[l, :, :, c], page_table, axis=0).reshape(
            db, n_pages * page, ATT_KV_HEADS, HEAD_DIM) for c in range(4))
        ys, *st_s = decoder_layer(ys, past_s, cache_win_kv[l], state_conv[l], state_pool[l],
                                  state_gdn_conv[l], state_gdn[l], *lw)
        for lst, arr in zip(new_p, st_p):
            lst.append(arr)
        for lst, arr in zip(new_s, st_s):
            lst.append(arr)
    p_nsa = jnp.stack(new_p[0]).reshape(DEPTH, -1, page, 4, ATT_KV_HEADS, HEAD_DIM)
    p_win, p_conv, p_pool, p_gconv, p_gdn = [jnp.stack(a) for a in new_p[1:]]
    s_nsa, s_win, s_conv, s_pool, s_gconv, s_gdn = [jnp.stack(a) for a in new_s]
    return (yp, ys, p_nsa, p_win, p_conv, p_pool, p_gconv, p_gdn, s_nsa, s_win, s_conv, s_pool, s_gconv, s_gdn)
```

```python
import functools
import math

import jax
import jax.numpy as jnp
from jax import lax
from jax.experimental import pallas as pl
from jax.experimental.pallas import tpu as pltpu

F32 = jnp.float32
BF16 = jnp.bfloat16
I32 = jnp.int32

D_MODEL = 1024
GROUP_WIDTH = 256
CONV_CH = 256
CONV_WIDTH = 31
POOL_CH = 256
POOL_WINDOWS = (2, 4, 8, 16)
POOL_BUF = 15
ATT_HEADS = 4
ATT_KV_HEADS = 2
HEAD_DIM = 64
ATT_SCALE = HEAD_DIM ** -0.5
CMP_STRIDE = 16
CMP_BLOCK = 32
SEL_BLOCK = 64
SEL_TOPN = 16
WINDOW = 512
N_BUCKETS = 32
GDN_HEADS = 4
GDN_DK = 64
GDN_DV = 64
GDN_QKV = 768
GDN_CONV = 4
GDN_CHUNK = 64
D_FF = 4096
DEPTH = 2
DN_ALPHA = (2 * DEPTH) ** 0.25
LN_EPS = 1e-5
NEG = -1e30
FORCE = 1e4

LANES = 128
VMEM_LIMIT_BYTES = 56 * 1024 * 1024

C_KV, C_GQKV, C_GLU, C_POOL, C_Q, C_Z, C_SM = 0, 768, 1536, 2048, 2304, 2560, 2816
IN_PAD = 2944
SM_GATE, SM_A, SM_B = 0, 12, 16


def _cparams(*sem):
    return pltpu.CompilerParams(dimension_semantics=sem, vmem_limit_bytes=VMEM_LIMIT_BYTES)


def _const_spec(shape):
    nd = len(shape)
    return pl.BlockSpec(shape, lambda *_: (0,) * nd, pipeline_mode=pl.Buffered(1))


def _smem_spec():
    return pl.BlockSpec(memory_space=pltpu.SMEM)


def _sigmoid(x):
    return jax.nn.sigmoid(x)


def _silu(x):
    return x * jax.nn.sigmoid(x)


def _layer_norm(y, g, b):
    mu = jnp.mean(y, axis=-1, keepdims=True)
    yc = y - mu
    var = jnp.mean(yc * yc, axis=-1, keepdims=True)
    return yc * lax.rsqrt(var + LN_EPS) * g + b


def _dot(a, b):
    return jnp.dot(a, b, preferred_element_type=F32)


def _dot_nt(a, b):
    return lax.dot_general(a, b, (((1,), (1,)), ((), ())), preferred_element_type=F32)


def _dot_tn(a, b):
    return lax.dot_general(a, b, (((0,), (0,)), ((), ())), preferred_element_type=F32)


def _split3(x):
    x1 = x.astype(BF16)
    r = x - x1.astype(F32)
    x2 = r.astype(BF16)
    x3 = (r - x2.astype(F32)).astype(BF16)
    return x1, x2, x3


def _dot_hl(a, b):
    ah = a.astype(BF16)
    al = (a - ah.astype(F32)).astype(BF16)
    bh = b.astype(BF16)
    bl = (b - bh.astype(F32)).astype(BF16)
    lhs = jnp.concatenate([ah, al, ah], axis=1)
    rhs = jnp.concatenate([bh, bh, bl], axis=0)
    return _dot(lhs, rhs)


def _t5_bucket(d):
    d = jnp.maximum(d, 0)
    logd = jnp.log(jnp.maximum(d, 1).astype(F32) / 16.0) / math.log(8.0)
    large = jnp.minimum(16 + (logd * 16.0).astype(I32), N_BUCKETS - 1)
    return jnp.where(d < 16, d, large)


def _bias_lookup(bk, value_of_bucket):
    out = jnp.zeros(bk.shape, F32)
    for k in range(N_BUCKETS):
        out = jnp.where(bk == k, value_of_bucket(k), out)
    return out


def _masked_softmax_parts(parts):
    ss = [jnp.where(ok, s, NEG) for s, ok in parts]
    mx = ss[0].max(-1, keepdims=True)
    for s in ss[1:]:
        mx = jnp.maximum(mx, s.max(-1, keepdims=True))
    es = [jnp.where(ok, jnp.exp(s - mx), 0.0) for s, (_, ok) in zip(ss, parts)]
    tot = es[0].sum(-1, keepdims=True)
    for e in es[1:]:
        tot = tot + e.sum(-1, keepdims=True)
    inv = 1.0 / jnp.maximum(tot, 1e-30)
    return [e * inv for e in es]


def _proj_in_body(x_ref, w_ref, o_ref):
    xb = x_ref[...].astype(BF16)
    for a in range(0, IN_PAD, 512):
        b = min(a + 512, IN_PAD)
        o_ref[:, a:b] = _dot(xb, w_ref[:, a:b])


def _proj_in(x, w, tm):
    n = x.shape[0]
    return pl.pallas_call(
        _proj_in_body,
        grid=(n // tm,),
        in_specs=[pl.BlockSpec((tm, D_MODEL), lambda i: (i, 0)), _const_spec((D_MODEL, IN_PAD))],
        out_specs=pl.BlockSpec((tm, IN_PAD), lambda i: (i, 0)),
        out_shape=jax.ShapeDtypeStruct((n, IN_PAD), F32),
        compiler_params=_cparams("arbitrary"),
        name="proj_in",
    )(x, w)


def _proj_out_body(x_ref, m0_ref, m1_ref, m2_ref, m3_ref, w_ref, g_ref, b_ref, o_ref):
    acc = _dot(m0_ref[...], w_ref[0:256, :])
    acc += _dot(m1_ref[...], w_ref[256:512, :])
    acc += _dot(m2_ref[...], w_ref[512:768, :])
    acc += _dot(m3_ref[...], w_ref[768:1024, :])
    y = DN_ALPHA * x_ref[...] + acc
    o_ref[...] = _layer_norm(y, g_ref[...], b_ref[...])


def _proj_out(x, mixes, w, g, b, tm):
    n = x.shape[0]
    row = lambda i: (i, 0)
    return pl.pallas_call(
        _proj_out_body,
        grid=(n // tm,),
        in_specs=[pl.BlockSpec((tm, D_MODEL), row)] + [pl.BlockSpec((tm, GROUP_WIDTH), row)] * 4
        + [_const_spec((D_MODEL, D_MODEL)), _const_spec((1, D_MODEL)), _const_spec((1, D_MODEL))],
        out_specs=pl.BlockSpec((tm, D_MODEL), row),
        out_shape=jax.ShapeDtypeStruct((n, D_MODEL), F32),
        compiler_params=_cparams("arbitrary"),
        name="proj_out_ln",
    )(x, *mixes, w, g, b)


FF_CHUNK = 1024


def _ffn_body(x_ref, wu_ref, wd_ref, g_ref, b_ref, o_ref):
    x = x_ref[...]
    xb = x.astype(BF16)
    acc = jnp.zeros(x.shape, F32)
    for c in range(0, D_FF, FF_CHUNK):
        h = _dot(xb, wu_ref[:, c:c + FF_CHUNK])
        a = jnp.square(jnp.maximum(h, 0.0)).astype(BF16)
        acc += _dot(a, wd_ref[c:c + FF_CHUNK, :])
    o_ref[...] = _layer_norm(DN_ALPHA * x + acc, g_ref[...], b_ref[...])


def _ffn(x, wu, wd, g, b, tm):
    n = x.shape[0]
    row = lambda i: (i, 0)
    return pl.pallas_call(
        _ffn_body,
        grid=(n // tm,),
        in_specs=[pl.BlockSpec((tm, D_MODEL), row), _const_spec((D_MODEL, D_FF)), _const_spec((D_FF, D_MODEL)),
                  _const_spec((1, D_MODEL)), _const_spec((1, D_MODEL))],
        out_specs=pl.BlockSpec((tm, D_MODEL), row),
        out_shape=jax.ShapeDtypeStruct((n, D_MODEL), F32),
        compiler_params=_cparams("arbitrary"),
        name="ffn_ln",
    )(x, wu, wd, g, b)


CONV_PAD = 32


def _conv_body(h_ref, buf_ref, dw_ref, dwb_ref, g_ref, b_ref, pw_ref, y_ref, new_ref, full_ref, *, T, t_valid):
    hh = h_ref[0]
    full_ref[0:8, :] = jnp.zeros((8, CONV_CH), F32)
    full_ref[2:CONV_PAD, :] = buf_ref[0]
    full_ref[CONV_PAD:CONV_PAD + T, :] = hh[:, :CONV_CH] * _sigmoid(hh[:, CONV_CH:])
    new_ref[0] = full_ref[t_valid + 2:t_valid + CONV_PAD, :]
    rc = min(T, 128)

    def chunk(c, carry):
        base = pl.multiple_of(c * rc, rc)
        win = full_ref[pl.ds(base, rc + CONV_PAD), :]
        acc = jnp.zeros((rc, CONV_CH), F32) + dwb_ref[...]
        for k in range(CONV_WIDTH):
            acc = acc + dw_ref[k:k + 1, :] * win[2 + k:2 + k + rc, :]
        y = _silu(_layer_norm(acc, g_ref[...], b_ref[...]))
        y_ref[0, pl.ds(base, rc), :] = _dot(y.astype(BF16), pw_ref[...]).astype(BF16)
        return carry

    lax.fori_loop(0, T // rc, chunk, 0)


def _conv_mixer(h3, buf, dw, dwb, g, b, pw, t_valid):
    B, T, _ = h3.shape
    return pl.pallas_call(
        functools.partial(_conv_body, T=T, t_valid=t_valid),
        grid=(B,),
        in_specs=[pl.BlockSpec((1, T, 2 * CONV_CH), lambda i: (i, 0, C_GLU // (2 * CONV_CH))),
                  pl.BlockSpec((1, CONV_WIDTH - 1, CONV_CH), lambda i: (i, 0, 0)),
                  _const_spec((CONV_WIDTH, CONV_CH)), _const_spec((1, CONV_CH)), _const_spec((1, CONV_CH)),
                  _const_spec((1, CONV_CH)), _const_spec((CONV_CH, CONV_CH))],
        out_specs=[pl.BlockSpec((1, T, CONV_CH), lambda i: (i, 0, 0)),
                   pl.BlockSpec((1, CONV_WIDTH - 1, CONV_CH), lambda i: (i, 0, 0))],
        out_shape=[jax.ShapeDtypeStruct((B, T, CONV_CH), BF16),
                   jax.ShapeDtypeStruct((B, CONV_WIDTH - 1, CONV_CH), F32)],
        scratch_shapes=[pltpu.VMEM((T + CONV_PAD, CONV_CH), F32)],
        compiler_params=_cparams("arbitrary"),
        name="conv_mixer",
    )(h3, buf, dw, dwb, g, b, pw)


POOL_PAD = 16


def _pool_body(h_ref, buf_ref, w_ref, sc_ref, y_ref, new_ref, full_ref, *, T, t_valid, offset):
    full_ref[0:8, :] = jnp.zeros((8, POOL_CH), F32)
    full_ref[1:POOL_PAD, :] = buf_ref[0]
    full_ref[POOL_PAD:POOL_PAD + T, :] = h_ref[0]
    new_ref[0] = full_ref[t_valid + 1:t_valid + POOL_PAD, :]
    rc = min(T, 128)
    lane = lax.broadcasted_iota(I32, (1, POOL_CH), 1)
    group = lane // (POOL_CH // len(POOL_WINDOWS))
    wl = jnp.where(group == 0, 2, jnp.where(group == 1, 4, jnp.where(group == 2, 8, 16)))

    def chunk(c, carry):
        base = pl.multiple_of(c * rc, rc)
        win = full_ref[pl.ds(base, rc + POOL_PAD), :]
        x0 = win[POOL_PAD:POOL_PAD + rc, :]
        sums = {}
        acc = x0
        for i in range(1, 16):
            acc = acc + win[POOL_PAD - i:POOL_PAD - i + rc, :]
            if i + 1 in POOL_WINDOWS:
                sums[i + 1] = acc
        sel = jnp.where(group == 0, sums[2], jnp.where(group == 1, sums[4], jnp.where(group == 2, sums[8], sums[16])))
        pos = offset + base + lax.broadcasted_iota(I32, (rc, 1), 0)
        cnt = jnp.minimum(pos + 1, wl).astype(F32)
        d = sel / cnt - x0
        y_ref[0, pl.ds(base, rc), :] = (_dot(d.astype(BF16), w_ref[...]) * sc_ref[...]).astype(BF16)
        return carry

    lax.fori_loop(0, T // rc, chunk, 0)


def _pool_mixer(h3, buf, wblk, scale, t_valid, offset):
    B, T, _ = h3.shape
    return pl.pallas_call(
        functools.partial(_pool_body, T=T, t_valid=t_valid, offset=offset),
        grid=(B,),
        in_specs=[pl.BlockSpec((1, T, POOL_CH), lambda i: (i, 0, C_POOL // POOL_CH)),
                  pl.BlockSpec((1, POOL_BUF, POOL_CH), lambda i: (i, 0, 0)),
                  _const_spec((POOL_CH, POOL_CH)), _const_spec((1, POOL_CH))],
        out_specs=[pl.BlockSpec((1, T, POOL_CH), lambda i: (i, 0, 0)),
                   pl.BlockSpec((1, POOL_BUF, POOL_CH), lambda i: (i, 0, 0))],
        out_shape=[jax.ShapeDtypeStruct((B, T, POOL_CH), BF16),
                   jax.ShapeDtypeStruct((B, POOL_BUF, POOL_CH), F32)],
        scratch_shapes=[pltpu.VMEM((T + POOL_PAD, POOL_CH), F32)],
        compiler_params=_cparams("arbitrary"),
        name="pool_mixer",
    )(h3, buf, wblk, scale)


GDN_PAD = 8
CK = GDN_CHUNK


def _gdn_body(qkv_ref, z_ref, sm_ref, buf_ref, s0_ref, cw_ref, alog_ref, dtb_ref, ng_ref,
              y_ref, newbuf_ref, sout_ref, full_ref, c_ref, g_ref, bt_ref, s_ref, *, T, Tp, t_valid):
    full_ref[0:8, :] = jnp.zeros((8, GDN_QKV), F32)
    full_ref[5:GDN_PAD, :] = buf_ref[0]
    full_ref[GDN_PAD:GDN_PAD + T, :] = qkv_ref[0]
    newbuf_ref[0] = full_ref[t_valid + 5:t_valid + GDN_PAD, :]
    if Tp > t_valid:
        c_ref[...] = jnp.zeros((Tp, GDN_QKV), F32)
        g_ref[...] = jnp.zeros((Tp, LANES), F32)
        bt_ref[...] = jnp.zeros((Tp, LANES), F32)

    rc = min(t_valid, 128)

    def conv_chunk(c, carry):
        base = pl.multiple_of(c * rc, rc)
        win = full_ref[pl.ds(base, rc + GDN_PAD), :] if rc % 8 == 0 else full_ref[0:rc + GDN_PAD, :]
        acc = jnp.zeros((rc, GDN_QKV), F32)
        for k in range(GDN_CONV):
            acc = acc + cw_ref[k:k + 1, :] * win[5 + k:5 + k + rc, :]
        sm = sm_ref[0, pl.ds(base, rc), :] if rc % 8 == 0 else sm_ref[0, 0:rc, :]
        x = sm + dtb_ref[...]
        softplus = jnp.maximum(x, 0.0) + jnp.log1p(jnp.exp(-jnp.abs(x)))
        gv = -jnp.exp(alog_ref[...]) * softplus
        bv = _sigmoid(sm)
        if rc % 8 == 0:
            c_ref[pl.ds(base, rc), :] = _silu(acc)
            g_ref[pl.ds(base, rc), :] = gv
            bt_ref[pl.ds(base, rc), :] = bv
        else:
            c_ref[0:rc, :] = _silu(acc)
            g_ref[0:rc, :] = gv
            bt_ref[0:rc, :] = bv
        return carry

    lax.fori_loop(0, t_valid // rc, conv_chunk, 0)

    s_ref[...] = s0_ref[0]
    ri = lax.broadcasted_iota(I32, (CK, CK), 0)
    ci = lax.broadcasted_iota(I32, (CK, CK), 1)
    tri = jnp.where(ri >= ci, 1.0, 0.0).astype(BF16)
    eye = jnp.where(ri == ci, 1.0, 0.0)
    er = lax.broadcasted_iota(I32, (8, LANES), 0)
    ec = lax.broadcasted_iota(I32, (8, LANES), 1)
    esel = jnp.where(ec == er + SM_A, 1.0, 0.0).astype(BF16)
    rows_out = min(CK, T)

    def chunk(c, carry):
        r0 = pl.multiple_of(c * CK, CK)
        gs = _split3(g_ref[pl.ds(r0, CK), :])
        gcum = _dot(tri, gs[0]) + _dot(tri, gs[1]) + _dot(tri, gs[2])
        gc3 = _split3(gcum)
        gcum_t = _dot_nt(esel, gc3[0]) + _dot_nt(esel, gc3[1]) + _dot_nt(esel, gc3[2])
        beta = bt_ref[pl.ds(r0, CK), :]
        cc = c_ref[pl.ds(r0, CK), :]
        zz = z_ref[0, pl.ds(r0, rows_out), :] if T >= CK else z_ref[0]
        for h in range(GDN_HEADS):
            qh = cc[:, 64 * h:64 * h + 64]
            kh = cc[:, 256 + 64 * h:256 + 64 * h + 64]
            vh = cc[:, 512 + 64 * h:512 + 64 * h + 64]
            qn = qh * lax.rsqrt(jnp.sum(qh * qh, -1, keepdims=True) + 1e-6) * (GDN_DK ** -0.5)
            kn = kh * lax.rsqrt(jnp.sum(kh * kh, -1, keepdims=True) + 1e-6)
            bcol = beta[:, SM_B + h:SM_B + h + 1]
            gcol = gcum[:, SM_A + h:SM_A + h + 1]
            grow = gcum_t[h:h + 1, :]
            kb = kn * bcol
            decay = jnp.exp(jnp.where(ri >= ci, gcol - grow, NEG))
            lower = jnp.where(ri > ci, _dot_nt(kb.astype(BF16), kn.astype(BF16)) * decay, 0.0)
            pw = -lower
            tinv = eye + pw
            for _ in range(5):
                pw = _dot_hl(pw, pw)
                tinv = tinv + _dot_hl(tinv, pw)
            rhs = jnp.concatenate([vh * bcol, kb * jnp.exp(gcol)], axis=1)
            sol = _dot_hl(tinv, rhs)
            u_val, w_dec = sol[:, :GDN_DV], sol[:, GDN_DV:]
            s_h = s_ref[h]
            s_b = s_h.astype(BF16)
            vnew = u_val - _dot(w_dec.astype(BF16), s_b)
            qk = _dot_nt(qn.astype(BF16), kn.astype(BF16)) * decay
            o = _dot((qn * jnp.exp(gcol)).astype(BF16), s_b) + _dot(qk.astype(BF16), vnew.astype(BF16))
            glast = gcum[CK - 1:CK, SM_A + h:SM_A + h + 1]
            kd = kn * jnp.exp(glast - gcol)
            s_ref[h] = s_h * jnp.exp(glast) + _dot_tn(kd.astype(BF16), vnew.astype(BF16))
            on = o * lax.rsqrt(jnp.mean(o * o, -1, keepdims=True) + LN_EPS) * ng_ref[...]
            zh = zz[:, 64 * h:64 * h + 64]
            y = on[0:rows_out] * _silu(zh)
            if T >= CK:
                y_ref[0, pl.ds(r0, CK), 64 * h:64 * h + 64] = y.astype(BF16)
            else:
                y_ref[0, :, 64 * h:64 * h + 64] = y.astype(BF16)
        return carry

    lax.fori_loop(0, Tp // CK, chunk, 0)
    sout_ref[0] = s_ref[...]


def _gdn_mixer(h3, buf, s0, cw, alog_l, dtb_l, ng, t_valid):
    B, T, _ = h3.shape
    Tp = -(-T // CK) * CK
    return pl.pallas_call(
        functools.partial(_gdn_body, T=T, Tp=Tp, t_valid=t_valid),
        grid=(B,),
        in_specs=[pl.BlockSpec((1, T, GDN_QKV), lambda i: (i, 0, C_GQKV // GDN_QKV)),
                  pl.BlockSpec((1, T, 256), lambda i: (i, 0, C_Z // 256)),
                  pl.BlockSpec((1, T, LANES), lambda i: (i, 0, C_SM // LANES)),
                  pl.BlockSpec((1, GDN_CONV - 1, GDN_QKV), lambda i: (i, 0, 0)),
                  pl.BlockSpec((1, GDN_HEADS, GDN_DK, GDN_DV), lambda i: (i, 0, 0, 0)),
                  _const_spec((GDN_CONV, GDN_QKV)), _const_spec((1, LANES)), _const_spec((1, LANES)),
                  _const_spec((1, GDN_DV))],
        out_specs=[pl.BlockSpec((1, T, 256), lambda i: (i, 0, 0)),
                   pl.BlockSpec((1, GDN_CONV - 1, GDN_QKV), lambda i: (i, 0, 0)),
                   pl.BlockSpec((1, GDN_HEADS, GDN_DK, GDN_DV), lambda i: (i, 0, 0, 0))],
        out_shape=[jax.ShapeDtypeStruct((B, T, 256), BF16),
                   jax.ShapeDtypeStruct((B, GDN_CONV - 1, GDN_QKV), F32),
                   jax.ShapeDtypeStruct((B, GDN_HEADS, GDN_DK, GDN_DV), F32)],
        scratch_shapes=[pltpu.VMEM((T + GDN_PAD, GDN_QKV), F32), pltpu.VMEM((Tp, GDN_QKV), F32),
                        pltpu.VMEM((Tp, LANES), F32), pltpu.VMEM((Tp, LANES), F32),
                        pltpu.VMEM((GDN_HEADS, GDN_DK, GDN_DV), F32)],
        compiler_params=_cparams("arbitrary"),
        name="gdn_mixer",
    )(h3, h3, h3, buf, s0, cw, alog_l, dtb_l, ng)


def _peterm_body(pe_ref, w1_ref, o_ref):
    pe = jnp.broadcast_to(pe_ref[0], (8, CMP_BLOCK * HEAD_DIM)).astype(BF16)
    o_ref[0] = _dot(pe, w1_ref[0].astype(BF16))


def _peterm(pe_flat, w1):
    n = pe_flat.shape[0]
    return pl.pallas_call(
        _peterm_body,
        grid=(n,),
        in_specs=[pl.BlockSpec((1, 1, CMP_BLOCK * HEAD_DIM), lambda i: (i, 0, 0)),
                  pl.BlockSpec((1, CMP_BLOCK * HEAD_DIM, HEAD_DIM), lambda i: (i, 0, 0))],
        out_specs=pl.BlockSpec((1, 8, HEAD_DIM), lambda i: (i, 0, 0)),
        out_shape=jax.ShapeDtypeStruct((n, 8, HEAD_DIM), F32),
        compiler_params=_cparams("arbitrary"),
        name="cmp_pe_term",
    )(pe_flat, w1)


QT = 128
WBAND = WINDOW + QT


def _tabw_body(rb_ref, o_ref):
    i = lax.broadcasted_iota(I32, (QT, WBAND), 0)
    j = lax.broadcasted_iota(I32, (QT, WBAND), 1)
    bk = _t5_bucket(WINDOW + i - j)
    for h in range(ATT_HEADS):
        o_ref[h] = _bias_lookup(bk, lambda k: rb_ref[k, h])


def _tabc_body(rb_ref, o_ref):
    p0 = pl.program_id(0) * QT
    ns = o_ref.shape[-1]
    t = p0 + lax.broadcasted_iota(I32, (QT, ns), 0)
    n = lax.broadcasted_iota(I32, (QT, ns), 1)
    bk = _t5_bucket(t - (n * CMP_STRIDE + CMP_BLOCK - 1))
    for h in range(ATT_HEADS):
        o_ref[h] = _bias_lookup(bk, lambda k: rb_ref[k, h])


def _bias_tables(rel_bias, T):
    ns = T // CMP_STRIDE
    tabw = pl.pallas_call(
        _tabw_body, in_specs=[_smem_spec()],
        out_shape=jax.ShapeDtypeStruct((ATT_HEADS, QT, WBAND), F32), name="bias_window_table")(rel_bias)
    tabc = pl.pallas_call(
        _tabc_body, grid=(T // QT,), in_specs=[_smem_spec()],
        out_specs=pl.BlockSpec((ATT_HEADS, QT, ns), lambda i: (0, i, 0)),
        out_shape=jax.ShapeDtypeStruct((ATT_HEADS, T, ns), F32),
        compiler_params=_cparams("arbitrary"), name="bias_cmp_table")(rel_bias)
    return tabw, tabc


def _compress_pre(load_rows, wcat_ref, c):
    acc = None
    for r in range(CMP_STRIDE):
        part = _dot(load_rows(r).astype(BF16), wcat_ref[c, r])
        acc = part if acc is None else acc + part
    return acc


def _compress_finish(hcat, pt, w2):
    pre = hcat[:, :LANES] + jnp.roll(hcat[:, LANES:], -1, axis=0) + pt
    return _dot(_silu(pre).astype(BF16), w2)


def _topn_select(score, n_cols):
    j = lax.broadcasted_iota(I32, score.shape, 1)
    rank = jnp.zeros(score.shape, F32)
    for jp in range(n_cols):
        col = score[:, jp:jp + 1]
        ahead = (col > score) | ((col == score) & (jp < j))
        rank = rank + jnp.where(ahead, 1.0, 0.0)
    return jnp.where((rank < SEL_TOPN) & (score > 0.5 * NEG), 1.0, 0.0)


def _nsa_prompt_body(rb_ref, q_ref, sm_ref, kv_ref, wcat_ref, w2_ref, pt_ref, tabw_ref, tabc_ref, cover_ref,
                     expand_ref, y_ref, kvp_ref, kc_ref, vc_ref, cmp_ref, *, T):
    ns = T // CMP_STRIDE
    n_sel = T // SEL_BLOCK
    qt = pl.program_id(1)
    p0 = pl.multiple_of(qt * QT, QT)

    @pl.when(qt == 0)
    def _():
        kvp_ref[0:WINDOW, :] = jnp.zeros((WINDOW, 512), BF16)
        kvp_ref[WINDOW:WINDOW + T, :] = kv_ref[0, :, 256:768].astype(BF16)
        for c, dst in ((0, kc_ref), (1, vc_ref)):
            cmp_ref[c] = kv_ref[0, :, LANES * c:LANES * (c + 1)]
            hcat = _compress_pre(lambda r: cmp_ref[c, pl.ds(r, ns, stride=CMP_STRIDE), :], wcat_ref, c)
            dst[...] = _compress_finish(hcat, pt_ref[c], w2_ref[c]).astype(BF16)

    gates = _sigmoid(sm_ref[0])
    t = p0 + lax.broadcasted_iota(I32, (QT, 1), 0)
    n_i = lax.broadcasted_iota(I32, (1, ns), 1)
    ok_cmp = (t - (n_i * CMP_STRIDE + CMP_BLOCK - 1) >= 0) & (n_i < ns - 1)
    j_i = lax.broadcasted_iota(I32, (1, n_sel), 1)
    cur = t // SEL_BLOCK
    forced = (j_i == 0) | (j_i == cur) | (j_i == cur - 1)
    avail = j_i * SEL_BLOCK <= t
    m_far = lax.broadcasted_iota(I32, (1, T), 1)
    m_near = p0 - QT + lax.broadcasted_iota(I32, (1, 2 * QT), 1)
    near_blk = (p0 - QT + lax.broadcasted_iota(I32, (n_sel, 2 * QT), 1)) // SEL_BLOCK
    e_near = jnp.where(near_blk == lax.broadcasted_iota(I32, (n_sel, 2 * QT), 0), 1.0, 0.0).astype(BF16)
    m_win = p0 - WINDOW + lax.broadcasted_iota(I32, (1, WBAND), 1)
    d_win = t - m_win
    ok_win = (m_win >= 0) & (d_win >= 0) & (d_win < WINDOW)

    for kh in range(ATT_KV_HEADS):
        qs, ocs = [], []
        psum = None
        for g in range(2):
            h = 2 * kh + g
            qh = q_ref[0, :, 64 * h:64 * h + 64].astype(BF16)
            qs.append(qh)
            s = _dot_nt(qh, kc_ref[:, 64 * kh:64 * kh + 64]) * ATT_SCALE + tabc_ref[h]
            (p,) = _masked_softmax_parts([(s, ok_cmp)])
            psum = p if psum is None else psum + p
            ocs.append(_dot(p.astype(BF16), vc_ref[:, 64 * kh:64 * kh + 64]))
        p3 = _split3(psum)
        imp = _dot(p3[0], cover_ref[...]) + _dot(p3[1], cover_ref[...]) + _dot(p3[2], cover_ref[...])
        score = jnp.where(avail, imp + jnp.where(forced, FORCE, 0.0), NEG)
        sel = _topn_select(score, n_sel).astype(BF16)
        ok_far = (_dot(sel, expand_ref[...]) > 0.5) & (m_far < p0 - QT)
        ok_near = (_dot(sel, e_near) > 0.5) & (m_near >= 0) & (m_near <= t)
        for g in range(2):
            h = 2 * kh + g
            qh = qs[g]
            s_far = _dot_nt(qh, kvp_ref[WINDOW:WINDOW + T, 64 * kh:64 * kh + 64]) * ATT_SCALE + rb_ref[N_BUCKETS - 1, h]
            k_near = kvp_ref[pl.ds(WINDOW + p0 - QT, 2 * QT), 64 * kh:64 * kh + 64]
            s_near = _dot_nt(qh, k_near) * ATT_SCALE + tabw_ref[h, :, WINDOW - QT:WINDOW + QT]
            p_far, p_near = _masked_softmax_parts([(s_far, ok_far), (s_near, ok_near)])
            v_near = kvp_ref[pl.ds(WINDOW + p0 - QT, 2 * QT), 128 + 64 * kh:128 + 64 * kh + 64]
            o_sel = (_dot(p_far.astype(BF16), kvp_ref[WINDOW:WINDOW + T, 128 + 64 * kh:128 + 64 * kh + 64])
                     + _dot(p_near.astype(BF16), v_near))
            k_w = kvp_ref[pl.ds(p0, WBAND), 256 + 64 * kh:256 + 64 * kh + 64]
            v_w = kvp_ref[pl.ds(p0, WBAND), 384 + 64 * kh:384 + 64 * kh + 64]
            s_w = _dot_nt(qh, k_w) * ATT_SCALE + tabw_ref[h]
            (p_w,) = _masked_softmax_parts([(s_w, ok_win)])
            o_win = _dot(p_w.astype(BF16), v_w)
            out = (gates[:, h:h + 1] * ocs[g] + gates[:, 4 + h:5 + h] * o_sel + gates[:, 8 + h:9 + h] * o_win)
            y_ref[0, :, 64 * h:64 * h + 64] = out.astype(BF16)


def _nsa_prompt(h3, rel_bias, wcat, w2bd, pt, tabw, tabc, cover, expand):
    B, T, _ = h3.shape
    ns = T // CMP_STRIDE
    n_sel = T // SEL_BLOCK
    return pl.pallas_call(
        functools.partial(_nsa_prompt_body, T=T),
        grid=(B, T // QT),
        in_specs=[_smem_spec(),
                  pl.BlockSpec((1, QT, 256), lambda b, i: (b, i, C_Q // 256)),
                  pl.BlockSpec((1, QT, LANES), lambda b, i: (b, i, C_SM // LANES)),
                  pl.BlockSpec((1, T, 768), lambda b, i: (b, 0, C_KV // 768)),
                  _const_spec((2, CMP_STRIDE, LANES, 2 * LANES)), _const_spec((2, LANES, LANES)),
                  _const_spec((2, 1, LANES)), _const_spec((ATT_HEADS, QT, WBAND)),
                  pl.BlockSpec((ATT_HEADS, QT, ns), lambda b, i: (0, i, 0)),
                  _const_spec((ns, n_sel)), _const_spec((n_sel, T))],
        out_specs=pl.BlockSpec((1, QT, 256), lambda b, i: (b, i, 0)),
        out_shape=jax.ShapeDtypeStruct((B, T, 256), BF16),
        scratch_shapes=[pltpu.VMEM((WINDOW + T, 512), BF16), pltpu.VMEM((ns, LANES), BF16),
                        pltpu.VMEM((ns, LANES), BF16), pltpu.VMEM((2, T, LANES), F32)],
        compiler_params=_cparams("arbitrary", "arbitrary"),
        name="nsa_prompt",
    )(rel_bias, h3, h3, h3, wcat, w2bd, pt, tabw, tabc, cover, expand)


def _nsa_s1_body(pt_ref, cache_ref, wcat_ref, o_ref, buf_ref, row_ref, sem, *, layer, CH):
    s = pl.program_id(0)
    nsteps = pl.num_programs(0)
    slot = s % 2

    def page_copy(step, p, sl):
        phys = pt_ref[step * CH + p]
        return pltpu.make_async_copy(cache_ref.at[layer, phys, pl.ds(0, 2)], buf_ref.at[sl, p], sem.at[sl])

    def issue(step, sl):
        def one(p, carry):
            page_copy(step, p, sl).start()
            return carry
        lax.fori_loop(0, CH, one, 0)

    @pl.when(s == 0)
    def _():
        issue(0, 0)

    @pl.when(s + 1 < nsteps)
    def _():
        issue(s + 1, 1 - slot)

    def wait_one(p, carry):
        page_copy(s, p, slot).wait()
        return carry
    lax.fori_loop(0, CH, wait_one, 0)

    def xpose(p, carry):
        base = pl.multiple_of(p * LANES, LANES)
        for c in range(2):
            for kh in range(2):
                row_ref[c, pl.ds(base, LANES), 64 * kh:64 * kh + 64] = buf_ref[slot, p, c, kh].T
        return carry
    lax.fori_loop(0, CH, xpose, 0)

    nsub = CH * LANES // CMP_STRIDE
    for c in range(2):
        hcat = _compress_pre(lambda r: row_ref[c, pl.ds(r, nsub, stride=CMP_STRIDE), :], wcat_ref, c)
        o_ref[0, :, 2 * LANES * c:2 * LANES * (c + 1)] = hcat


def _nsa_s1(page_flat, cache_t, wcat, layer, B, NP):
    CH = min(64, NP)
    nsub = CH * LANES // CMP_STRIDE
    per_b = NP // CH
    grid_spec = pltpu.PrefetchScalarGridSpec(
        num_scalar_prefetch=1,
        grid=(B * per_b,),
        in_specs=[pl.BlockSpec(memory_space=pl.ANY),
                  pl.BlockSpec((2, CMP_STRIDE, LANES, 2 * LANES), lambda s, pt: (0, 0, 0, 0),
                               pipeline_mode=pl.Buffered(1))],
        out_specs=pl.BlockSpec((1, nsub, 4 * LANES), lambda s, pt: (s // per_b, s % per_b, 0)),
        scratch_shapes=[pltpu.VMEM((2, CH, 2, 2, HEAD_DIM, LANES), F32), pltpu.VMEM((2, CH * LANES, LANES), F32),
                        pltpu.SemaphoreType.DMA((2,))],
    )
    return pl.pallas_call(
        functools.partial(_nsa_s1_body, layer=layer, CH=CH),
        grid_spec=grid_spec,
        out_shape=jax.ShapeDtypeStruct((B, NP * LANES // CMP_STRIDE, 4 * LANES), F32),
        compiler_params=_cparams("arbitrary"),
        name="nsa_sample_compress",
    )(page_flat, cache_t, wcat)


def _row_bias(bk, rb_ref, kh, g_of_row):
    return _bias_lookup(bk, lambda k: jnp.where(g_of_row == 1, rb_ref[k, 2 * kh + 1], rb_ref[k, 2 * kh]))


def _nsa_s2_body(rb_ref, hc_ref, q_ref, w2_ref, pt_ref, cover_ref, pair_ref, ocmp_ref, u_ref, *, P, t_valid, NSELP):
    nsub = hc_ref.shape[1]
    kc = _compress_finish(hc_ref[0, :, 0:2 * LANES], pt_ref[0], w2_ref[0]).astype(BF16)
    vc = _compress_finish(hc_ref[0, :, 2 * LANES:4 * LANES], pt_ref[1], w2_ref[1]).astype(BF16)
    n_sel = P // SEL_BLOCK + 1
    row = lax.broadcasted_iota(I32, (16, 1), 0)
    t16 = row % 8
    g16 = row // 8
    n_i = lax.broadcasted_iota(I32, (1, nsub), 1)
    d_cmp = (P + t16) - (n_i * CMP_STRIDE + CMP_BLOCK - 1)
    ok_cmp = d_cmp >= 0
    bk_cmp = _t5_bucket(d_cmp)
    t8 = lax.broadcasted_iota(I32, (8, 1), 0)
    j_i = lax.broadcasted_iota(I32, (1, NSELP), 1)
    qpos = P + t8
    cur = qpos // SEL_BLOCK
    forced = (j_i == 0) | (j_i == cur) | (j_i == cur - 1)
    avail = (j_i * SEL_BLOCK <= qpos) & (j_i < n_sel)
    wts = jnp.where(t8 < t_valid, jnp.left_shift(1, t8 + 4 * (j_i % 2)), 0).astype(F32)
    u_rows = []
    for kh in range(ATT_KV_HEADS):
        q16 = jnp.concatenate([q_ref[0, :, 128 * kh:128 * kh + 64], q_ref[0, :, 128 * kh + 64:128 * kh + 128]],
                              axis=0).astype(BF16)
        s = _dot_nt(q16, kc[:, 64 * kh:64 * kh + 64]) * ATT_SCALE + _row_bias(bk_cmp, rb_ref, kh, g16)
        (p,) = _masked_softmax_parts([(s, ok_cmp)])
        o = _dot(p.astype(BF16), vc[:, 64 * kh:64 * kh + 64])
        ocmp_ref[0, :, 128 * kh:128 * kh + 64] = o[0:8]
        ocmp_ref[0, :, 128 * kh + 64:128 * kh + 128] = o[8:16]
        p3 = _split3(p[0:8] + p[8:16])
        imp = _dot(p3[0], cover_ref[...]) + _dot(p3[1], cover_ref[...]) + _dot(p3[2], cover_ref[...])
        score = jnp.where(avail, imp + jnp.where(forced, FORCE, 0.0), NEG)
        sel = _topn_select(score, n_sel)
        colsum = jnp.sum(sel * wts, axis=0, keepdims=True)
        u_rows.append(_dot(jnp.broadcast_to(colsum, (8, NSELP)).astype(BF16), pair_ref[...])[0:1])
    u_ref[0] = jnp.concatenate(u_rows + [jnp.zeros((6, 2 * LANES), F32)], axis=0).astype(I32)


def _nsa_s2(hs3, hc, rel_bias, w2bd, pt, cover, pair, P, t_valid):
    B, T, _ = hs3.shape
    nsub = hc.shape[1]
    nselp = cover.shape[1]
    return pl.pallas_call(
        functools.partial(_nsa_s2_body, P=P, t_valid=t_valid, NSELP=nselp),
        grid=(B,),
        in_specs=[_smem_spec(),
                  pl.BlockSpec((1, nsub, 4 * LANES), lambda b: (b, 0, 0)),
                  pl.BlockSpec((1, T, 256), lambda b: (b, 0, C_Q // 256)),
                  _const_spec((2, LANES, LANES)), _const_spec((2, 1, LANES)),
                  _const_spec((nsub, nselp)), _const_spec((nselp, 2 * LANES))],
        out_specs=[pl.BlockSpec((1, 8, 256), lambda b: (b, 0, 0)), pl.BlockSpec((1, 8, 2 * LANES), lambda b: (b, 0, 0))],
        out_shape=[jax.ShapeDtypeStruct((B, 8, 256), F32), jax.ShapeDtypeStruct((B, 8, 2 * LANES), I32)],
        compiler_params=_cparams("arbitrary"),
        name="nsa_sample_cmp_select",
    )(rel_bias, hc, hs3, w2bd, pt, cover, pair)


MAX_SEL_PAGES = 64


def _nsa_s3_body(pt_ref, u_ref, rb_ref, q_ref, sm_ref, kvn_ref, ocmp_ref, cache_ref, win_ref, y_ref,
                 kbuf_ref, vbuf_ref, slot_u, slot_p, sem, *, layer, P, NP, t_valid):
    b = pl.program_id(0)
    wb = win_ref.shape[-1]
    row = lax.broadcasted_iota(I32, (16, 1), 0)
    t16 = row % 8
    g16 = row // 8
    valid_row = t16 < t_valid
    gates = _sigmoid(sm_ref[0])

    def kv_copies(kh, phys, i):
        return (pltpu.make_async_copy(cache_ref.at[layer, phys, 2, kh], kbuf_ref.at[kh, i], sem.at[0]),
                pltpu.make_async_copy(cache_ref.at[layer, phys, 3, kh], vbuf_ref.at[kh, i], sem.at[0]))

    counts = []
    for kh in range(ATT_KV_HEADS):
        def scan(p, cnt, kh=kh):
            u = u_ref[(b * 2 + kh) * 2 * LANES + p]

            @pl.when(u != 0)
            def _():
                ck, cv = kv_copies(kh, pt_ref[b * NP + p], cnt)
                ck.start()
                cv.start()
                slot_u[kh * MAX_SEL_PAGES + cnt] = u
                slot_p[kh * MAX_SEL_PAGES + cnt] = p
            return cnt + jnp.where(u != 0, 1, 0)
        counts.append(lax.fori_loop(0, NP, scan, 0))

    for kh in range(ATT_KV_HEADS):
        def wait_pair(i, carry, kh=kh):
            ck, cv = kv_copies(kh, 0, i)
            ck.wait()
            cv.wait()
            return carry
        lax.fori_loop(0, counts[kh], wait_pair, 0)

    lane = lax.broadcasted_iota(I32, (1, LANES), 1)
    shift = t16 + 4 * (lane // SEL_BLOCK)
    for kh in range(ATT_KV_HEADS):
        q16f = jnp.concatenate([q_ref[0, :, 128 * kh:128 * kh + 64], q_ref[0, :, 128 * kh + 64:128 * kh + 128]], axis=0)
        q16 = q16f.astype(BF16)

        def page(i, carry, kh=kh, q16=q16):
            m_run, l_run, acc = carry
            u = slot_u[kh * MAX_SEL_PAGES + i]
            pg = slot_p[kh * MAX_SEL_PAGES + i]
            s = _dot(q16, kbuf_ref[kh, i].astype(BF16)) * ATT_SCALE
            d = (P + t16) - (pg * LANES + lane)
            s = s + _row_bias(_t5_bucket(d), rb_ref, kh, g16)
            ok = (jnp.bitwise_and(jnp.right_shift(u, shift), 1) == 1) & (d >= 0) & valid_row
            s = jnp.where(ok, s, NEG)
            m_new = jnp.maximum(m_run, s.max(-1, keepdims=True))
            e = jnp.where(ok, jnp.exp(s - m_new), 0.0)
            alpha = jnp.exp(m_run - m_new)
            l_new = alpha * l_run + e.sum(-1, keepdims=True)
            acc_new = alpha * acc + _dot_nt(e.astype(BF16), vbuf_ref[kh, i].astype(BF16))
            return m_new, l_new, acc_new

        init = (jnp.full((16, 1), NEG, F32), jnp.zeros((16, 1), F32), jnp.zeros((16, HEAD_DIM), F32))
        m_run, l_run, acc = lax.fori_loop(0, counts[kh], page, init)
        u_new = u_ref[(b * 2 + kh) * 2 * LANES + NP]
        sel_new = jnp.bitwise_and(jnp.right_shift(u_new, t16), 1) == 1
        cols = []
        for c in range(t_valid):
            kn = kvn_ref[0, c:c + 1, 256 + 64 * kh:256 + 64 * kh + 64]
            sc = jnp.sum(q16f * kn, axis=-1, keepdims=True) * ATT_SCALE + _row_bias(_t5_bucket(t16 - c), rb_ref, kh, g16)
            ok = sel_new & (t16 >= c) & valid_row
            cols.append((jnp.where(ok, sc, NEG), ok))
        m_fin = m_run
        for sc, _ in cols:
            m_fin = jnp.maximum(m_fin, sc)
        alpha = jnp.exp(m_run - m_fin)
        l_fin = alpha * l_run
        acc = alpha * acc
        for c, (sc, ok) in enumerate(cols):
            e = jnp.where(ok, jnp.exp(sc - m_fin), 0.0)
            l_fin = l_fin + e
            acc = acc + e * kvn_ref[0, c:c + 1, 384 + 64 * kh:384 + 64 * kh + 64]
        o_sel = acc / jnp.maximum(l_fin, 1e-30)

        col_w = lax.broadcasted_iota(I32, (1, wb), 1)
        d_w = wb + t16 - col_w
        ok_w = (d_w < WINDOW) & valid_row
        s_w = _dot(q16, win_ref[0, 0, kh].astype(BF16)) * ATT_SCALE + _row_bias(_t5_bucket(d_w), rb_ref, kh, g16)
        s_w = jnp.where(ok_w, s_w, NEG)
        mw = s_w.max(-1, keepdims=True)
        cols = []
        for c in range(t_valid):
            kn = kvn_ref[0, c:c + 1, 512 + 64 * kh:512 + 64 * kh + 64]
            sc = jnp.sum(q16f * kn, axis=-1, keepdims=True) * ATT_SCALE + _row_bias(_t5_bucket(t16 - c), rb_ref, kh, g16)
            ok = (t16 >= c) & valid_row
            sc = jnp.where(ok, sc, NEG)
            cols.append((sc, ok))
            mw = jnp.maximum(mw, sc)
        e_w = jnp.where(ok_w, jnp.exp(s_w - mw), 0.0)
        l_w = e_w.sum(-1, keepdims=True)
        acc_w = _dot_nt(e_w.astype(BF16), win_ref[0, 1, kh].astype(BF16))
        for c, (sc, ok) in enumerate(cols):
            e = jnp.where(ok, jnp.exp(sc - mw), 0.0)
            l_w = l_w + e
            acc_w = acc_w + e * kvn_ref[0, c:c + 1, 640 + 64 * kh:640 + 64 * kh + 64]
        o_win = acc_w / jnp.maximum(l_w, 1e-30)

        for g in range(2):
            h = 2 * kh + g
            out = (gates[:, h:h + 1] * ocmp_ref[0, :, 64 * h:64 * h + 64]
                   + gates[:, 4 + h:5 + h] * o_sel[8 * g:8 * g + 8]
                   + gates[:, 8 + h:9 + h] * o_win[8 * g:8 * g + 8])
            y_ref[0, :, 64 * h:64 * h + 64] = out.astype(BF16)


def _nsa_s3(page_flat, u_flat, rel_bias, hs3, ocmp, cache_t, win_t, layer, P, NP, t_valid):
    B, T, _ = hs3.shape
    wb = win_t.shape[-1]
    grid_spec = pltpu.PrefetchScalarGridSpec(
        num_scalar_prefetch=2,
        grid=(B,),
        in_specs=[_smem_spec(),
                  pl.BlockSpec((1, T, 256), lambda b, *_: (b, 0, C_Q // 256)),
                  pl.BlockSpec((1, T, LANES), lambda b, *_: (b, 0, C_SM // LANES)),
                  pl.BlockSpec((1, T, 768), lambda b, *_: (b, 0, C_KV // 768)),
                  pl.BlockSpec((1, 8, 256), lambda b, *_: (b, 0, 0)),
                  pl.BlockSpec(memory_space=pl.ANY),
                  pl.BlockSpec((1, 2, 2, HEAD_DIM, wb), lambda b, *_: (b, 0, 0, 0, 0))],
        out_specs=pl.BlockSpec((1, T, 256), lambda b, *_: (b, 0, 0)),
        scratch_shapes=[pltpu.VMEM((2, MAX_SEL_PAGES, HEAD_DIM, LANES), F32),
                        pltpu.VMEM((2, MAX_SEL_PAGES, HEAD_DIM, LANES), F32),
                        pltpu.SMEM((2 * MAX_SEL_PAGES,), I32), pltpu.SMEM((2 * MAX_SEL_PAGES,), I32),
                        pltpu.SemaphoreType.DMA((1,))],
    )
    return pl.pallas_call(
        functools.partial(_nsa_s3_body, layer=layer, P=P, NP=NP, t_valid=t_valid),
        grid_spec=grid_spec,
        out_shape=jax.ShapeDtypeStruct((B, T, 256), BF16),
        compiler_params=_cparams("arbitrary"),
        name="nsa_sample_select_window",
    )(page_flat, u_flat, rel_bias, hs3, hs3, hs3, ocmp, cache_t, win_t)


def _blockdiag2(w):
    z = jnp.zeros_like(w)
    return jnp.concatenate([jnp.concatenate([w, z], axis=-1), jnp.concatenate([z, w], axis=-1)], axis=-2)


def _prep_layer(w_in, pool_w, cmp_w1, cmp_w2, alog, dtb):
    glu, pool, q, kv, gate, gqkv, z, a, b = jnp.split(w_in, [512, 768, 1024, 1792, 1804, 2572, 2828, 2832], axis=1)
    pad = jnp.zeros((D_MODEL, IN_PAD - C_SM - 20), F32)
    w_in_p = jnp.concatenate([kv, gqkv, glu, pool, q, z, gate, a, b, pad], axis=1).astype(BF16)
    wblk = jnp.zeros((POOL_CH, POOL_CH), F32)
    for gi in range(4):
        wblk = wblk.at[64 * gi:64 * gi + 64, 64 * gi:64 * gi + 64].set(pool_w[gi])
    w1 = cmp_w1.reshape(2, CMP_BLOCK, HEAD_DIM, HEAD_DIM)
    wcat = jnp.concatenate([_blockdiag2(w1[:, :CMP_STRIDE]), _blockdiag2(w1[:, CMP_STRIDE:])], axis=-1).astype(BF16)
    w2bd = _blockdiag2(cmp_w2).astype(BF16)
    lane_pad = lambda v: jnp.zeros((1, LANES), F32).at[0, SM_A:SM_A + GDN_HEADS].set(v)
    return w_in_p, wblk.astype(BF16), wcat, w2bd, lane_pad(alog), lane_pad(dtb)


def _cover_matrix(nsub, n_sel, cols):
    n = jnp.arange(nsub)[:, None] * CMP_STRIDE
    j = jnp.arange(cols)[None, :] * SEL_BLOCK
    cov = (n < j + SEL_BLOCK) & (n + CMP_BLOCK > j) & (jnp.arange(cols)[None, :] < n_sel) & (jnp.arange(nsub)[:, None] < nsub - 1)
    return cov.astype(BF16)


def _mixers_to_x(x2d, mixes, lw, tm):
    x1 = _proj_out(x2d, mixes, lw["w_out"], lw["ln1_g"], lw["ln1_b"], tm)
    return _ffn(x1, lw["w_up"], lw["w_down"], lw["ln2_g"], lw["ln2_b"], tm)


def kernel(x_prompt, x_sample, cache_nsa_kv, cache_win_kv, state_conv, state_pool, state_gdn_conv, state_gdn,
           page_table, w_in, conv_dw, conv_dw_b, conv_ln_g, conv_ln_b, conv_pw, pool_w, pool_scale,
           cmp_pe, cmp_w1, cmp_w2, gdn_conv_w, gdn_a_log, gdn_dt_bias, gdn_norm_g,
           w_out, ln1_g, ln1_b, w_up, w_down, ln2_g, ln2_b, rel_bias):
    depth = w_in.shape[0]
    BP, T, _ = x_prompt.shape
    BS, TS, _ = x_sample.shape
    NP = page_table.shape[1]
    page = cache_nsa_kv.shape[2]
    P = NP * page
    TSP = 8
    assert page == LANES and TS <= TSP and TS < CMP_STRIDE and T % QT == 0 and P % SEL_BLOCK == 0

    cache_t = jnp.transpose(cache_nsa_kv, (0, 1, 3, 4, 5, 2))
    win_t = jnp.transpose(cache_win_kv, (0, 1, 3, 4, 5, 2))
    page_flat = page_table.reshape(-1)

    tabw, tabc = _bias_tables(rel_bias, T)
    pterm = _peterm(cmp_pe.reshape(depth * 2, 1, CMP_BLOCK * HEAD_DIM), cmp_w1.reshape(depth * 2, CMP_BLOCK * HEAD_DIM, HEAD_DIM))
    pterm = pterm[:, 0:1, :].reshape(depth, 2, 1, HEAD_DIM)
    pterm = jnp.concatenate([pterm, pterm], axis=-1)

    ns_p, nsel_p = T // CMP_STRIDE, T // SEL_BLOCK
    cover_p = _cover_matrix(ns_p, nsel_p, nsel_p)
    expand_p = (jnp.arange(nsel_p)[:, None] == (jnp.arange(T)[None, :] // SEL_BLOCK)).astype(BF16)
    ns_s = P // CMP_STRIDE
    nsel_s = P // SEL_BLOCK + 1
    nselp_s = -(-nsel_s // LANES) * LANES
    cover_s = _cover_matrix(ns_s + 1, nsel_s, nselp_s)[:ns_s]
    jj = jnp.arange(nselp_s)[:, None]
    pp = jnp.arange(2 * LANES)[None, :]
    pair = (((jj // 2 == pp) & (jj < 2 * NP)) | ((jj == 2 * NP) & (pp == NP))).astype(BF16)

    yp = x_prompt.reshape(BP * T, D_MODEL)
    ys = jnp.pad(x_sample, ((0, 0), (0, TSP - TS), (0, 0))).reshape(BS * TSP, D_MODEL)
    zeros = lambda *s: jnp.zeros(s, F32)
    outs_p = [[] for _ in range(6)]
    outs_s = [[] for _ in range(6)]
    for l in range(depth):
        w_in_p, wblk, wcat, w2bd, alog_l, dtb_l = _prep_layer(w_in[l], pool_w[l], cmp_w1[l], cmp_w2[l],
                                                               gdn_a_log[l], gdn_dt_bias[l])
        lw = dict(w_out=w_out[l].astype(BF16), ln1_g=ln1_g[l][None], ln1_b=ln1_b[l][None],
                  w_up=w_up[l].astype(BF16), w_down=w_down[l].astype(BF16), ln2_g=ln2_g[l][None], ln2_b=ln2_b[l][None])
        conv_w = (conv_dw[l], conv_dw_b[l][None], conv_ln_g[l][None], conv_ln_b[l][None], conv_pw[l].astype(BF16))
        gdn_w = (gdn_conv_w[l], alog_l, dtb_l, gdn_norm_g[l][None])

        hp = _proj_in(yp, w_in_p, 512).reshape(BP, T, IN_PAD)
        m_conv, conv_new = _conv_mixer(hp, zeros(BP, CONV_WIDTH - 1, CONV_CH), *conv_w, t_valid=T)
        m_pool, pool_new = _pool_mixer(hp, zeros(BP, POOL_BUF, POOL_CH), wblk, pool_scale[l][None], t_valid=T, offset=0)
        m_att = _nsa_prompt(hp, rel_bias, wcat, w2bd, pterm[l], tabw, tabc, cover_p, expand_p)
        m_gdn, gbuf_new, s_new = _gdn_mixer(hp, zeros(BP, GDN_CONV - 1, GDN_QKV), zeros(BP, GDN_HEADS, GDN_DK, GDN_DV),
                                            *gdn_w, t_valid=T)
        mixes = [m.reshape(BP * T, GROUP_WIDTH) for m in (m_conv, m_pool, m_att, m_gdn)]
        yp = _mixers_to_x(yp, mixes, lw, 512)
        outs_p[0].append(hp[:, :, C_KV:C_KV + 512].reshape(BP, T, 4, ATT_KV_HEADS, HEAD_DIM))
        wrows = min(WINDOW, T)
        outs_p[1].append(hp[:, T - wrows:, C_KV + 512:C_KV + 768].reshape(BP, wrows, 2, ATT_KV_HEADS, HEAD_DIM))
        for lst, arr in zip(outs_p[2:], (conv_new, pool_new, gbuf_new, s_new)):
            lst.append(arr)

        hs = _proj_in(ys, w_in_p, BS * TSP).reshape(BS, TSP, IN_PAD)
        m_conv, conv_new = _conv_mixer(hs, state_conv[l], *conv_w, t_valid=TS)
        m_pool, pool_new = _pool_mixer(hs, state_pool[l], wblk, pool_scale[l][None], t_valid=TS, offset=P)
        hc = _nsa_s1(page_flat, cache_t, wcat, l, BS, NP)
        ocmp, u = _nsa_s2(hs, hc, rel_bias, w2bd, pterm[l], cover_s, pair, P, TS)
        m_att = _nsa_s3(page_flat, u[:, :2, :].reshape(-1), rel_bias, hs, ocmp, cache_t, win_t[l], l, P, NP, TS)
        m_gdn, gbuf_new, s_new = _gdn_mixer(hs, state_gdn_conv[l], state_gdn[l], *gdn_w, t_valid=TS)
        mixes = [m.reshape(BS * TSP, GROUP_WIDTH) for m in (m_conv, m_pool, m_att, m_gdn)]
        ys = _mixers_to_x(ys, mixes, lw, BS * TSP)
        kv_new = hs[:, :TS, C_KV:C_KV + 768].reshape(BS, TS, 6, ATT_KV_HEADS, HEAD_DIM)
        outs_s[0].append(kv_new[:, :, :4])
        kw_all = jnp.concatenate([cache_win_kv[l], kv_new[:, :, 4:]], axis=1)
        outs_s[1].append(kw_all[:, -min(WINDOW, kw_all.shape[1]):])
        for lst, arr in zip(outs_s[2:], (conv_new, pool_new, gbuf_new, s_new)):
            lst.append(arr)

    p_nsa = jnp.stack(outs_p[0]).reshape(depth, -1, page, 4, ATT_KV_HEADS, HEAD_DIM)
    p_rest = [jnp.stack(a) for a in outs_p[1:]]
    s_all = [jnp.stack(a) for a in outs_s]
    y_s = ys.reshape(BS, TSP, D_MODEL)[:, :TS]
    return (yp.reshape(BP, T, D_MODEL), y_s, p_nsa, *p_rest, *s_all)
```

```python
import functools
import math

import jax
import jax.numpy as jnp
from jax import lax
from jax.experimental import pallas as pl
from jax.experimental.pallas import tpu as pltpu

F32 = jnp.float32
BF16 = jnp.bfloat16
I32 = jnp.int32

D_MODEL = 1024
GROUP_WIDTH = 256
CONV_CH = 256
CONV_WIDTH = 31
POOL_CH = 256
POOL_WINDOWS = (2, 4, 8, 16)
POOL_BUF = 15
ATT_HEADS = 4
ATT_KV_HEADS = 2
HEAD_DIM = 64
ATT_SCALE = HEAD_DIM ** -0.5
CMP_STRIDE = 16
CMP_BLOCK = 32
SEL_BLOCK = 64
SEL_TOPN = 16
WINDOW = 512
N_BUCKETS = 32
GDN_HEADS = 4
GDN_DK = 64
GDN_DV = 64
GDN_QKV = 768
GDN_CONV = 4
GDN_CHUNK = 64
D_FF = 4096
DEPTH = 2
DN_ALPHA = (2 * DEPTH) ** 0.25
LN_EPS = 1e-5
NEG = -1e30
FORCE = 1e4

LANES = 128
VMEM_LIMIT_BYTES = 56 * 1024 * 1024

C_KV, C_GQKV, C_GLU, C_POOL, C_Q, C_Z, C_SM = 0, 768, 1536, 2048, 2304, 2560, 2816
IN_PAD = 2944
SM_GATE, SM_A, SM_B = 0, 12, 16


def _cparams(*sem):
    return pltpu.CompilerParams(dimension_semantics=sem, vmem_limit_bytes=VMEM_LIMIT_BYTES)


def _const_spec(shape):
    nd = len(shape)
    return pl.BlockSpec(shape, lambda *_: (0,) * nd, pipeline_mode=pl.Buffered(1))


def _smem_spec():
    return pl.BlockSpec(memory_space=pltpu.SMEM)


def _sigmoid(x):
    return jax.nn.sigmoid(x)


def _silu(x):
    return x * jax.nn.sigmoid(x)


def _layer_norm(y, g, b):
    mu = jnp.mean(y, axis=-1, keepdims=True)
    yc = y - mu
    var = jnp.mean(yc * yc, axis=-1, keepdims=True)
    return yc * lax.rsqrt(var + LN_EPS) * g + b


def _dot(a, b):
    return jnp.dot(a, b, preferred_element_type=F32)


def _dot_nt(a, b):
    return lax.dot_general(a, b, (((1,), (1,)), ((), ())), preferred_element_type=F32)


def _dot_tn(a, b):
    return lax.dot_general(a, b, (((0,), (0,)), ((), ())), preferred_element_type=F32)


def _split3(x):
    x1 = x.astype(BF16)
    r = x - x1.astype(F32)
    x2 = r.astype(BF16)
    x3 = (r - x2.astype(F32)).astype(BF16)
    return x1, x2, x3


def _dot_hl(a, b):
    ah = a.astype(BF16)
    al = (a - ah.astype(F32)).astype(BF16)
    bh = b.astype(BF16)
    bl = (b - bh.astype(F32)).astype(BF16)
    lhs = jnp.concatenate([ah, al, ah], axis=1)
    rhs = jnp.concatenate([bh, bh, bl], axis=0)
    return _dot(lhs, rhs)


def _t5_bucket(d):
    d = jnp.maximum(d, 0)
    logd = jnp.log(jnp.maximum(d, 1).astype(F32) / 16.0) / math.log(8.0)
    large = jnp.minimum(16 + (logd * 16.0).astype(I32), N_BUCKETS - 1)
    return jnp.where(d < 16, d, large)


def _bias_lookup(bk, value_of_bucket):
    out = jnp.zeros(bk.shape, F32)
    for k in range(N_BUCKETS):
        out = jnp.where(bk == k, value_of_bucket(k), out)
    return out


def _masked_softmax_parts(parts):
    ss = [jnp.where(ok, s, NEG) for s, ok in parts]
    mx = ss[0].max(-1, keepdims=True)
    for s in ss[1:]:
        mx = jnp.maximum(mx, s.max(-1, keepdims=True))
    es = [jnp.where(ok, jnp.exp(s - mx), 0.0) for s, (_, ok) in zip(ss, parts)]
    tot = es[0].sum(-1, keepdims=True)
    for e in es[1:]:
        tot = tot + e.sum(-1, keepdims=True)
    inv = 1.0 / jnp.maximum(tot, 1e-30)
    return [e * inv for e in es]


def _proj_in_body(x_ref, w_ref, o_ref):
    xb = x_ref[...].astype(BF16)
    for a in range(0, IN_PAD, 512):
        b = min(a + 512, IN_PAD)
        o_ref[:, a:b] = _dot(xb, w_ref[:, a:b])


def _proj_in(x, w, tm):
    n = x.shape[0]
    return pl.pallas_call(
        _proj_in_body,
        grid=(n // tm,),
        in_specs=[pl.BlockSpec((tm, D_MODEL), lambda i: (i, 0)), _const_spec((D_MODEL, IN_PAD))],
        out_specs=pl.BlockSpec((tm, IN_PAD), lambda i: (i, 0)),
        out_shape=jax.ShapeDtypeStruct((n, IN_PAD), F32),
        compiler_params=_cparams("arbitrary"),
        name="proj_in",
    )(x, w)


def _proj_out_body(x_ref, m0_ref, m1_ref, m2_ref, m3_ref, w_ref, g_ref, b_ref, o_ref):
    acc = _dot(m0_ref[...], w_ref[0:256, :])
    acc += _dot(m1_ref[...], w_ref[256:512, :])
    acc += _dot(m2_ref[...], w_ref[512:768, :])
    acc += _dot(m3_ref[...], w_ref[768:1024, :])
    y = DN_ALPHA * x_ref[...] + acc
    o_ref[...] = _layer_norm(y, g_ref[...], b_ref[...])


def _proj_out(x, mixes, w, g, b, tm):
    n = x.shape[0]
    row = lambda i: (i, 0)
    return pl.pallas_call(
        _proj_out_body,
        grid=(n // tm,),
        in_specs=[pl.BlockSpec((tm, D_MODEL), row)] + [pl.BlockSpec((tm, GROUP_WIDTH), row)] * 4
        + [_const_spec((D_MODEL, D_MODEL)), _const_spec((1, D_MODEL)), _const_spec((1, D_MODEL))],
        out_specs=pl.BlockSpec((tm, D_MODEL), row),
        out_shape=jax.ShapeDtypeStruct((n, D_MODEL), F32),
        compiler_params=_cparams("arbitrary"),
        name="proj_out_ln",
    )(x, *mixes, w, g, b)


FF_CHUNK = 1024


def _ffn_body(x_ref, wu_ref, wd_ref, g_ref, b_ref, o_ref):
    x = x_ref[...]
    xb = x.astype(BF16)
    acc = jnp.zeros(x.shape, F32)
    for c in range(0, D_FF, FF_CHUNK):
        h = _dot(xb, wu_ref[:, c:c + FF_CHUNK])
        a = jnp.square(jnp.maximum(h, 0.0)).astype(BF16)
        acc += _dot(a, wd_ref[c:c + FF_CHUNK, :])
    o_ref[...] = _layer_norm(DN_ALPHA * x + acc, g_ref[...], b_ref[...])


def _ffn(x, wu, wd, g, b, tm):
    n = x.shape[0]
    row = lambda i: (i, 0)
    return pl.pallas_call(
        _ffn_body,
        grid=(n // tm,),
        in_specs=[pl.BlockSpec((tm, D_MODEL), row), _const_spec((D_MODEL, D_FF)), _const_spec((D_FF, D_MODEL)),
                  _const_spec((1, D_MODEL)), _const_spec((1, D_MODEL))],
        out_specs=pl.BlockSpec((tm, D_MODEL), row),
        out_shape=jax.ShapeDtypeStruct((n, D_MODEL), F32),
        compiler_params=_cparams("arbitrary"),
        name="ffn_ln",
    )(x, wu, wd, g, b)


CONV_PAD = 32


def _conv_body(h_ref, buf_ref, dw_ref, dwb_ref, g_ref, b_ref, pw_ref, y_ref, new_ref, full_ref, *, T, t_valid):
    hh = h_ref[0]
    full_ref[0:8, :] = jnp.zeros((8, CONV_CH), F32)
    full_ref[2:CONV_PAD, :] = buf_ref[0]
    full_ref[CONV_PAD:CONV_PAD + T, :] = hh[:, :CONV_CH] * _sigmoid(hh[:, CONV_CH:])
    new_ref[0] = full_ref[t_valid + 2:t_valid + CONV_PAD, :]
    rc = min(T, 128)

    def chunk(c, carry):
        base = pl.multiple_of(c * rc, rc)
        win = full_ref[pl.ds(base, rc + CONV_PAD), :]
        acc = jnp.zeros((rc, CONV_CH), F32) + dwb_ref[...]
        for k in range(CONV_WIDTH):
            acc = acc + dw_ref[k:k + 1, :] * win[2 + k:2 + k + rc, :]
        y = _silu(_layer_norm(acc, g_ref[...], b_ref[...]))
        y_ref[0, pl.ds(base, rc), :] = _dot(y.astype(BF16), pw_ref[...]).astype(BF16)
        return carry

    lax.fori_loop(0, T // rc, chunk, 0)


def _conv_mixer(h3, buf, dw, dwb, g, b, pw, t_valid):
    B, T, _ = h3.shape
    return pl.pallas_call(
        functools.partial(_conv_body, T=T, t_valid=t_valid),
        grid=(B,),
        in_specs=[pl.BlockSpec((1, T, 2 * CONV_CH), lambda i: (i, 0, C_GLU // (2 * CONV_CH))),
                  pl.BlockSpec((1, CONV_WIDTH - 1, CONV_CH), lambda i: (i, 0, 0)),
                  _const_spec((CONV_WIDTH, CONV_CH)), _const_spec((1, CONV_CH)), _const_spec((1, CONV_CH)),
                  _const_spec((1, CONV_CH)), _const_spec((CONV_CH, CONV_CH))],
        out_specs=[pl.BlockSpec((1, T, CONV_CH), lambda i: (i, 0, 0)),
                   pl.BlockSpec((1, CONV_WIDTH - 1, CONV_CH), lambda i: (i, 0, 0))],
        out_shape=[jax.ShapeDtypeStruct((B, T, CONV_CH), BF16),
                   jax.ShapeDtypeStruct((B, CONV_WIDTH - 1, CONV_CH), F32)],
        scratch_shapes=[pltpu.VMEM((T + CONV_PAD, CONV_CH), F32)],
        compiler_params=_cparams("arbitrary"),
        name="conv_mixer",
    )(h3, buf, dw, dwb, g, b, pw)


POOL_PAD = 16


def _pool_body(h_ref, buf_ref, w_ref, sc_ref, y_ref, new_ref, full_ref, *, T, t_valid, offset):
    full_ref[0:8, :] = jnp.zeros((8, POOL_CH), F32)
    full_ref[1:POOL_PAD, :] = buf_ref[0]
    full_ref[POOL_PAD:POOL_PAD + T, :] = h_ref[0]
    new_ref[0] = full_ref[t_valid + 1:t_valid + POOL_PAD, :]
    rc = min(T, 128)
    lane = lax.broadcasted_iota(I32, (1, POOL_CH), 1)
    group = lane // (POOL_CH // len(POOL_WINDOWS))
    wl = jnp.where(group == 0, 2, jnp.where(group == 1, 4, jnp.where(group == 2, 8, 16)))

    def chunk(c, carry):
        base = pl.multiple_of(c * rc, rc)
        win = full_ref[pl.ds(base, rc + POOL_PAD), :]
        x0 = win[POOL_PAD:POOL_PAD + rc, :]
        sums = {}
        acc = x0
        for i in range(1, 16):
            acc = acc + win[POOL_PAD - i:POOL_PAD - i + rc, :]
            if i + 1 in POOL_WINDOWS:
                sums[i + 1] = acc
        sel = jnp.where(group == 0, sums[2], jnp.where(group == 1, sums[4], jnp.where(group == 2, sums[8], sums[16])))
        pos = offset + base + lax.broadcasted_iota(I32, (rc, 1), 0)
        cnt = jnp.minimum(pos + 1, wl).astype(F32)
        d = sel / cnt - x0
        y_ref[0, pl.ds(base, rc), :] = (_dot(d.astype(BF16), w_ref[...]) * sc_ref[...]).astype(BF16)
        return carry

    lax.fori_loop(0, T // rc, chunk, 0)


def _pool_mixer(h3, buf, wblk, scale, t_valid, offset):
    B, T, _ = h3.shape
    return pl.pallas_call(
        functools.partial(_pool_body, T=T, t_valid=t_valid, offset=offset),
        grid=(B,),
        in_specs=[pl.BlockSpec((1, T, POOL_CH), lambda i: (i, 0, C_POOL // POOL_CH)),
                  pl.BlockSpec((1, POOL_BUF, POOL_CH), lambda i: (i, 0, 0)),
                  _const_spec((POOL_CH, POOL_CH)), _const_spec((1, POOL_CH))],
        out_specs=[pl.BlockSpec((1, T, POOL_CH), lambda i: (i, 0, 0)),
                   pl.BlockSpec((1, POOL_BUF, POOL_CH), lambda i: (i, 0, 0))],
        out_shape=[jax.ShapeDtypeStruct((B, T, POOL_CH), BF16),
                   jax.ShapeDtypeStruct((B, POOL_BUF, POOL_CH), F32)],
        scratch_shapes=[pltpu.VMEM((T + POOL_PAD, POOL_CH), F32)],
        compiler_params=_cparams("arbitrary"),
        name="pool_mixer",
    )(h3, buf, wblk, scale)


GDN_PAD = 8
CK = GDN_CHUNK


def _gdn_body(qkv_ref, z_ref, sm_ref, buf_ref, s0_ref, cw_ref, alog_ref, dtb_ref, ng_ref,
              y_ref, newbuf_ref, sout_ref, full_ref, c_ref, g_ref, bt_ref, gi_ref, bi_ref, u_ref, w_ref, a_ref, s_ref,
              *, T, Tp, t_valid):
    full_ref[0:8, :] = jnp.zeros((8, GDN_QKV), F32)
    full_ref[5:GDN_PAD, :] = buf_ref[0]
    full_ref[GDN_PAD:GDN_PAD + T, :] = qkv_ref[0]
    newbuf_ref[0] = full_ref[t_valid + 5:t_valid + GDN_PAD, :]
    if Tp > t_valid:
        c_ref[...] = jnp.zeros((Tp, GDN_QKV), F32)
        g_ref[...] = jnp.zeros((Tp, LANES), F32)
        bt_ref[...] = jnp.zeros((Tp, LANES), F32)

    rc = min(t_valid, 128)

    def conv_chunk(c, carry):
        base = pl.multiple_of(c * rc, rc)
        win = full_ref[pl.ds(base, rc + GDN_PAD), :] if rc % 8 == 0 else full_ref[0:rc + GDN_PAD, :]
        acc = jnp.zeros((rc, GDN_QKV), F32)
        for k in range(GDN_CONV):
            acc = acc + cw_ref[k:k + 1, :] * win[5 + k:5 + k + rc, :]
        sm = sm_ref[0, pl.ds(base, rc), :] if rc % 8 == 0 else sm_ref[0, 0:rc, :]
        x = sm + dtb_ref[...]
        softplus = jnp.maximum(x, 0.0) + jnp.log1p(jnp.exp(-jnp.abs(x)))
        gv = -jnp.exp(alog_ref[...]) * softplus
        bv = _sigmoid(sm)
        if rc % 8 == 0:
            c_ref[pl.ds(base, rc), :] = _silu(acc)
            g_ref[pl.ds(base, rc), :] = gv
            bt_ref[pl.ds(base, rc), :] = bv
        else:
            c_ref[0:rc, :] = _silu(acc)
            g_ref[0:rc, :] = gv
            bt_ref[0:rc, :] = bv
        return carry

    lax.fori_loop(0, t_valid // rc, conv_chunk, 0)

    HW = GDN_HEADS * GDN_DK
    lane = lax.broadcasted_iota(I32, (1, HW), 1)
    hmask = [jnp.where(lane // GDN_DK == h, 1.0, 0.0).astype(BF16) for h in range(GDN_HEADS)]
    row = lax.broadcasted_iota(I32, (CK, 1), 0)
    jl = lane % CK
    incl = row >= jl
    strict = row > jl
    eye_all = jnp.where(row == jl, 1.0, 0.0)
    er = lax.broadcasted_iota(I32, (LANES, HW), 0)
    ec = lax.broadcasted_iota(I32, (LANES, HW), 1) // GDN_DK
    exp_a = jnp.where(er == ec + SM_A, 1.0, 0.0).astype(BF16)
    exp_b = jnp.where(er == ec + SM_B, 1.0, 0.0).astype(BF16)
    br = lax.broadcasted_iota(I32, (HW, HW), 0) // GDN_DK
    bc = lax.broadcasted_iota(I32, (HW, HW), 1) // GDN_DK
    same_head = br == bc
    bones = jnp.where(same_head, 1.0, 0.0).astype(BF16)

    def blockdiag(x):
        return jnp.concatenate([x * m for m in hmask], axis=0)

    def expand3(x, e):
        x1, x2, x3 = _split3(x)
        return _dot(jnp.concatenate([x1, x2, x3], axis=1), jnp.concatenate([e, e, e], axis=0))

    def bd_dot_hl(a, b):
        ah = a.astype(BF16)
        al = (a - ah.astype(F32)).astype(BF16)
        bh = b.astype(BF16)
        bl = (b - bh.astype(F32)).astype(BF16)
        bdh = blockdiag(bh)
        return _dot(jnp.concatenate([ah, al, ah], axis=1), jnp.concatenate([bdh, bdh, blockdiag(bl)], axis=0))

    def prep(c, carry):
        r0 = pl.multiple_of(c * CK, CK)
        g = g_ref[pl.ds(r0, CK), :]
        for s in (1, 2, 4, 8, 16, 32):
            g = g + jnp.where(row >= s, jnp.roll(g, s, axis=0), 0.0)
        gi_ref[pl.ds(r0, CK), :] = expand3(g, exp_a)
        bi_ref[pl.ds(r0, CK), :] = expand3(bt_ref[pl.ds(r0, CK), :], exp_b)
        for part, scale in ((0, GDN_DK ** -0.5), (1, 1.0)):
            x = c_ref[pl.ds(r0, CK), HW * part:HW * (part + 1)]
            ssq = expand3(x * x, bones)
            c_ref[pl.ds(r0, CK), HW * part:HW * (part + 1)] = x * lax.rsqrt(ssq + 1e-6) * scale
        return carry

    lax.fori_loop(0, Tp // CK, prep, 0)

    n_chunks = Tp // CK
    group = 4 if n_chunks % 4 == 0 else 1

    def solve(it, carry):
        r0s = [pl.multiple_of((it * group + k) * CK, CK) for k in range(group)]
        gi = [gi_ref[pl.ds(r0, CK), :] for r0 in r0s]
        kn = [c_ref[pl.ds(r0, CK), HW:2 * HW] for r0 in r0s]
        decay = [jnp.exp(jnp.where(incl, g - jnp.sum(eye_all * g, axis=0, keepdims=True), NEG)) for g in gi]
        kb = [k * bi_ref[pl.ds(r0, CK), :] for k, r0 in zip(kn, r0s)]
        kst = [blockdiag(k.astype(BF16)) for k in kn]
        pw = [-jnp.where(strict, _dot_nt(b.astype(BF16), s) * d, 0.0) for b, s, d in zip(kb, kst, decay)]
        tinv = [eye_all + p for p in pw]
        for _ in range(5):
            pw = [bd_dot_hl(p, p) for p in pw]
            tinv = [t + bd_dot_hl(t, p) for t, p in zip(tinv, pw)]
        for k, r0 in enumerate(r0s):
            vb = c_ref[pl.ds(r0, CK), 2 * HW:3 * HW] * bi_ref[pl.ds(r0, CK), :]
            u_ref[pl.ds(r0, CK), :] = bd_dot_hl(tinv[k], vb)
        for k, r0 in enumerate(r0s):
            w_ref[pl.ds(r0, CK), :] = bd_dot_hl(tinv[k], kb[k] * jnp.exp(gi[k]))
        for k, r0 in enumerate(r0s):
            qn = c_ref[pl.ds(r0, CK), 0:HW]
            a_ref[pl.ds(r0, CK), :] = (_dot_nt(qn.astype(BF16), kst[k]) * decay[k]).astype(BF16)
        return carry

    lax.fori_loop(0, n_chunks // group, solve, 0)

    s_ref[...] = jnp.zeros((HW, HW), F32)
    for h in range(GDN_HEADS):
        s_ref[GDN_DK * h:GDN_DK * (h + 1), GDN_DV * h:GDN_DV * (h + 1)] = s0_ref[0, h]
    rows_out = min(CK, T)
    ng_all = jnp.concatenate([ng_ref[...]] * GDN_HEADS, axis=1)

    def recur(c, carry):
        r0 = pl.multiple_of(c * CK, CK)
        gi = gi_ref[pl.ds(r0, CK), :]
        qn = c_ref[pl.ds(r0, CK), 0:HW]
        kn = c_ref[pl.ds(r0, CK), HW:2 * HW]
        s_all = s_ref[...]
        s_b = s_all.astype(BF16)
        vnew = u_ref[pl.ds(r0, CK), :] - _dot(w_ref[pl.ds(r0, CK), :].astype(BF16), s_b)
        vnb = vnew.astype(BF16)
        o = _dot((qn * jnp.exp(gi)).astype(BF16), s_b) + _dot(a_ref[pl.ds(r0, CK), :], blockdiag(vnb))
        glast = gi[CK - 1:CK, :]
        kd = kn * jnp.exp(glast - gi)
        s_ref[...] = s_all * jnp.exp(glast) + jnp.where(same_head, _dot_tn(kd.astype(BF16), vnb), 0.0)
        on = o * lax.rsqrt(expand3(o * o, bones) * (1.0 / GDN_DV) + LN_EPS) * ng_all
        if T >= CK:
            y_ref[0, pl.ds(r0, CK), :] = (on * _silu(z_ref[0, pl.ds(r0, CK), :])).astype(BF16)
        else:
            y_ref[0] = (on[0:rows_out] * _silu(z_ref[0])).astype(BF16)
        return carry

    lax.fori_loop(0, Tp // CK, recur, 0)
    for h in range(GDN_HEADS):
        sout_ref[0, h] = s_ref[GDN_DK * h:GDN_DK * (h + 1), GDN_DV * h:GDN_DV * (h + 1)]


def _gdn_mixer(h3, buf, s0, cw, alog_l, dtb_l, ng, t_valid):
    B, T, _ = h3.shape
    Tp = -(-T // CK) * CK
    return pl.pallas_call(
        functools.partial(_gdn_body, T=T, Tp=Tp, t_valid=t_valid),
        grid=(B,),
        in_specs=[pl.BlockSpec((1, T, GDN_QKV), lambda i: (i, 0, C_GQKV // GDN_QKV)),
                  pl.BlockSpec((1, T, 256), lambda i: (i, 0, C_Z // 256)),
                  pl.BlockSpec((1, T, LANES), lambda i: (i, 0, C_SM // LANES)),
                  pl.BlockSpec((1, GDN_CONV - 1, GDN_QKV), lambda i: (i, 0, 0)),
                  pl.BlockSpec((1, GDN_HEADS, GDN_DK, GDN_DV), lambda i: (i, 0, 0, 0)),
                  _const_spec((GDN_CONV, GDN_QKV)), _const_spec((1, LANES)), _const_spec((1, LANES)),
                  _const_spec((1, GDN_DV))],
        out_specs=[pl.BlockSpec((1, T, 256), lambda i: (i, 0, 0)),
                   pl.BlockSpec((1, GDN_CONV - 1, GDN_QKV), lambda i: (i, 0, 0)),
                   pl.BlockSpec((1, GDN_HEADS, GDN_DK, GDN_DV), lambda i: (i, 0, 0, 0))],
        out_shape=[jax.ShapeDtypeStruct((B, T, 256), BF16),
                   jax.ShapeDtypeStruct((B, GDN_CONV - 1, GDN_QKV), F32),
                   jax.ShapeDtypeStruct((B, GDN_HEADS, GDN_DK, GDN_DV), F32)],
        scratch_shapes=[pltpu.VMEM((T + GDN_PAD, GDN_QKV), F32), pltpu.VMEM((Tp, GDN_QKV), F32),
                        pltpu.VMEM((Tp, LANES), F32), pltpu.VMEM((Tp, LANES), F32),
                        pltpu.VMEM((Tp, 256), F32), pltpu.VMEM((Tp, 256), F32),
                        pltpu.VMEM((Tp, 256), F32), pltpu.VMEM((Tp, 256), F32), pltpu.VMEM((Tp, 256), BF16),
                        pltpu.VMEM((GDN_HEADS * GDN_DK, GDN_HEADS * GDN_DV), F32)],
        compiler_params=_cparams("arbitrary"),
        name="gdn_mixer",
    )(h3, h3, h3, buf, s0, cw, alog_l, dtb_l, ng)


def _peterm_body(pe_ref, w1_ref, o_ref):
    pe = jnp.broadcast_to(pe_ref[0], (8, CMP_BLOCK * HEAD_DIM)).astype(BF16)
    o_ref[0] = _dot(pe, w1_ref[0].astype(BF16))


def _peterm(pe_flat, w1):
    n = pe_flat.shape[0]
    return pl.pallas_call(
        _peterm_body,
        grid=(n,),
        in_specs=[pl.BlockSpec((1, 1, CMP_BLOCK * HEAD_DIM), lambda i: (i, 0, 0)),
                  pl.BlockSpec((1, CMP_BLOCK * HEAD_DIM, HEAD_DIM), lambda i: (i, 0, 0))],
        out_specs=pl.BlockSpec((1, 8, HEAD_DIM), lambda i: (i, 0, 0)),
        out_shape=jax.ShapeDtypeStruct((n, 8, HEAD_DIM), F32),
        compiler_params=_cparams("arbitrary"),
        name="cmp_pe_term",
    )(pe_flat, w1)


QT = 128
WBAND = WINDOW + QT


def _tabw_body(rb_ref, o_ref):
    i = lax.broadcasted_iota(I32, (QT, WBAND), 0)
    j = lax.broadcasted_iota(I32, (QT, WBAND), 1)
    bk = _t5_bucket(WINDOW + i - j)
    for h in range(ATT_HEADS):
        o_ref[h] = _bias_lookup(bk, lambda k: rb_ref[k, h])


def _tabc_body(rb_ref, o_ref):
    p0 = pl.program_id(0) * QT
    ns = o_ref.shape[-1]
    t = p0 + lax.broadcasted_iota(I32, (QT, ns), 0)
    n = lax.broadcasted_iota(I32, (QT, ns), 1)
    bk = _t5_bucket(t - (n * CMP_STRIDE + CMP_BLOCK - 1))
    for h in range(ATT_HEADS):
        o_ref[h] = _bias_lookup(bk, lambda k: rb_ref[k, h])


def _bias_tables(rel_bias, T):
    ns = T // CMP_STRIDE
    tabw = pl.pallas_call(
        _tabw_body, in_specs=[_smem_spec()],
        out_shape=jax.ShapeDtypeStruct((ATT_HEADS, QT, WBAND), F32), name="bias_window_table")(rel_bias)
    tabc = pl.pallas_call(
        _tabc_body, grid=(T // QT,), in_specs=[_smem_spec()],
        out_specs=pl.BlockSpec((ATT_HEADS, QT, ns), lambda i: (0, i, 0)),
        out_shape=jax.ShapeDtypeStruct((ATT_HEADS, T, ns), F32),
        compiler_params=_cparams("arbitrary"), name="bias_cmp_table")(rel_bias)
    return tabw, tabc


def _compress_pre(load_rows, wcat_ref, c):
    acc = None
    for r in range(CMP_STRIDE):
        part = _dot(load_rows(r).astype(BF16), wcat_ref[c, r])
        acc = part if acc is None else acc + part
    return acc


def _compress_finish(hcat, pt, w2):
    pre = hcat[:, :LANES] + jnp.roll(hcat[:, LANES:], -1, axis=0) + pt
    return _dot(_silu(pre).astype(BF16), w2)


def _topn_select(score, n_cols):
    j = lax.broadcasted_iota(I32, score.shape, 1)
    rank = jnp.zeros(score.shape, F32)
    for jp in range(n_cols):
        col = score[:, jp:jp + 1]
        ahead = (col > score) | ((col == score) & (jp < j))
        rank = rank + jnp.where(ahead, 1.0, 0.0)
    return jnp.where((rank < SEL_TOPN) & (score > 0.5 * NEG), 1.0, 0.0)


def _topn_select_rows(score_t, n_rows):
    j = lax.broadcasted_iota(I32, score_t.shape, 0)
    rank = jnp.zeros(score_t.shape, F32)
    for jp in range(n_rows):
        r = score_t[jp:jp + 1, :]
        ahead = (r > score_t) | ((r == score_t) & (jp < j))
        rank = rank + jnp.where(ahead, 1.0, 0.0)
    return jnp.where((rank < SEL_TOPN) & (score_t > 0.5 * NEG), 1.0, 0.0)


def _nsa_prompt_body(rb_ref, q_ref, sm_ref, kv_ref, wcat_ref, w2_ref, pt_ref, tabw_ref, tabc_ref, covert_ref,
                     expneg_ref, y_ref, kvp_ref, kc_ref, vc_ref, cmp_ref, ma_ref, *, T):
    ns = T // CMP_STRIDE
    n_sel = T // SEL_BLOCK
    qt = pl.program_id(1)
    p0 = pl.multiple_of(qt * QT, QT)

    @pl.when(qt == 0)
    def _():
        kvp_ref[0:WINDOW, :] = jnp.zeros((WINDOW, 512), BF16)
        kvp_ref[WINDOW:WINDOW + T, :] = kv_ref[0, :, 256:768].astype(BF16)
        for c, dst in ((0, kc_ref), (1, vc_ref)):
            cmp_ref[c] = kv_ref[0, :, LANES * c:LANES * (c + 1)]
            hcat = _compress_pre(lambda r: cmp_ref[c, pl.ds(r, ns, stride=CMP_STRIDE), :], wcat_ref, c)
            dst[...] = _compress_finish(hcat, pt_ref[c], w2_ref[c]).astype(BF16)

    FT = ma_ref.shape[-1]
    gates = _sigmoid(sm_ref[0])
    t = p0 + lax.broadcasted_iota(I32, (QT, 1), 0)
    n_i = lax.broadcasted_iota(I32, (1, ns), 1)
    ok_cmp = (t - (n_i * CMP_STRIDE + CMP_BLOCK - 1) >= 0) & (n_i < ns - 1)
    ok_cmp2 = jnp.concatenate([ok_cmp, ok_cmp], axis=0)
    t_l = p0 + lax.broadcasted_iota(I32, (1, QT), 1)
    j_s = lax.broadcasted_iota(I32, (n_sel, 1), 0)
    cur = t_l // SEL_BLOCK
    forced = (j_s == 0) | (j_s == cur) | (j_s == cur - 1)
    avail = j_s * SEL_BLOCK <= t_l
    m_near = p0 - QT + lax.broadcasted_iota(I32, (1, 2 * QT), 1)
    near_blk = (p0 - QT + lax.broadcasted_iota(I32, (n_sel, 2 * QT), 1)) // SEL_BLOCK
    e_near_neg = jnp.where(near_blk == j_s, NEG, 0.0).astype(BF16)
    causal_near = jnp.where((m_near >= 0) & (m_near <= t), 0.0, NEG)
    m_win = p0 - WINDOW + lax.broadcasted_iota(I32, (1, WBAND), 1)
    d_win = t - m_win
    add_win = jnp.where((m_win >= 0) & (d_win >= 0) & (d_win < WINDOW), 0.0, NEG)
    two = lambda x: jnp.concatenate([x, x], axis=0)

    KH = range(ATT_KV_HEADS)
    kcol = lambda kh, base: slice(base + 64 * kh, base + 64 * kh + 64)
    q2 = [(jnp.concatenate([q_ref[0, :, kcol(2 * kh, 0)], q_ref[0, :, kcol(2 * kh + 1, 0)]], axis=0)
           * ATT_SCALE).astype(BF16) for kh in KH]
    s_c = [_dot_nt(q2[kh], kc_ref[:, kcol(kh, 0)]) + jnp.concatenate([tabc_ref[2 * kh], tabc_ref[2 * kh + 1]], axis=0)
           for kh in KH]
    p_c = [_masked_softmax_parts([(s, ok_cmp2)])[0] for s in s_c]
    o_cmp = [_dot(p_c[kh].astype(BF16), vc_ref[:, kcol(kh, 0)]) for kh in KH]
    nsel_t = []
    for kh in KH:
        p3 = _split3(p_c[kh][0:QT] + p_c[kh][QT:2 * QT])
        imp_t = (_dot_nt(covert_ref[...], p3[0]) + _dot_nt(covert_ref[...], p3[1])
                 + _dot_nt(covert_ref[...], p3[2]))
        score_t = jnp.where(avail, imp_t + jnp.where(forced, FORCE, 0.0), NEG)
        nsel_t.append((1.0 - _topn_select_rows(score_t, n_sel)).astype(BF16))
    for kh in KH:
        ma_all = _dot_tn(nsel_t[kh], expneg_ref[...])
        for i in range(T // FT):
            ma_ref[kh, i] = ma_all[:, i * FT:(i + 1) * FT]
    add_near = [_dot_tn(nsel_t[kh], e_near_neg) + causal_near for kh in KH]
    near_rows = pl.ds(WINDOW + p0 - QT, 2 * QT)
    bias_near = [jnp.concatenate(
        [tabw_ref[2 * kh + g, :, WINDOW - QT:WINDOW + QT] - rb_ref[N_BUCKETS - 1, 2 * kh + g] + add_near[kh]
         for g in range(2)], axis=0) for kh in KH]
    s_n = [_dot_nt(q2[kh], kvp_ref[near_rows, kcol(kh, 0)]) + bias_near[kh] for kh in KH]
    m0 = [s.max(-1, keepdims=True) for s in s_n]
    e_n = [jnp.exp(s - m) for s, m in zip(s_n, m0)]
    init = tuple((m0[kh], e_n[kh].sum(-1, keepdims=True), _dot(e_n[kh].astype(BF16), kvp_ref[near_rows, kcol(kh, 128)]))
                 for kh in KH)

    def far(i, carry):
        k0 = pl.multiple_of(i * FT, FT)
        rows = pl.ds(WINDOW + k0, FT)
        lim = jnp.where(k0 + lax.broadcasted_iota(I32, (1, FT), 1) < p0 - QT, 0.0, NEG)
        s = [_dot_nt(q2[kh], kvp_ref[rows, kcol(kh, 0)]) + two(ma_ref[kh, i] + lim) for kh in KH]
        m_new = [jnp.maximum(carry[kh][0], s[kh].max(-1, keepdims=True)) for kh in KH]
        alpha = [jnp.exp(carry[kh][0] - m_new[kh]) for kh in KH]
        e = [jnp.exp(s[kh] - m_new[kh]) for kh in KH]
        return tuple((m_new[kh], alpha[kh] * carry[kh][1] + e[kh].sum(-1, keepdims=True),
                      alpha[kh] * carry[kh][2] + _dot(e[kh].astype(BF16), kvp_ref[rows, kcol(kh, 128)])) for kh in KH)

    n_far = (jnp.maximum(p0 - QT, 0) + FT - 1) // FT
    fin = lax.fori_loop(0, n_far, far, init)
    o_sel = [fin[kh][2] / jnp.maximum(fin[kh][1], 1e-30) for kh in KH]
    win_rows = pl.ds(p0, WBAND)
    s_w = [_dot_nt(q2[kh], kvp_ref[win_rows, kcol(kh, 256)])
           + jnp.concatenate([tabw_ref[2 * kh] + add_win, tabw_ref[2 * kh + 1] + add_win], axis=0) for kh in KH]
    e_w = [jnp.exp(s - s.max(-1, keepdims=True)) for s in s_w]
    o_win = [_dot(e_w[kh].astype(BF16), kvp_ref[win_rows, kcol(kh, 384)])
             / jnp.maximum(e_w[kh].sum(-1, keepdims=True), 1e-30) for kh in KH]
    for h in range(ATT_HEADS):
        kh, g = divmod(h, 2)
        rows = slice(QT * g, QT * (g + 1))
        out = (gates[:, h:h + 1] * o_cmp[kh][rows] + gates[:, 4 + h:5 + h] * o_sel[kh][rows]
               + gates[:, 8 + h:9 + h] * o_win[kh][rows])
        y_ref[0, :, 64 * h:64 * h + 64] = out.astype(BF16)


def _nsa_prompt(h3, rel_bias, wcat, w2bd, pt, tabw, tabc, covert, expneg):
    B, T, _ = h3.shape
    ns = T // CMP_STRIDE
    n_sel = T // SEL_BLOCK
    ft = min(512, T)
    return pl.pallas_call(
        functools.partial(_nsa_prompt_body, T=T),
        grid=(B, T // QT),
        in_specs=[_smem_spec(),
                  pl.BlockSpec((1, QT, 256), lambda b, i: (b, i, C_Q // 256)),
                  pl.BlockSpec((1, QT, LANES), lambda b, i: (b, i, C_SM // LANES)),
                  pl.BlockSpec((1, T, 768), lambda b, i: (b, 0, C_KV // 768)),
                  _const_spec((2, CMP_STRIDE, LANES, 2 * LANES)), _const_spec((2, LANES, LANES)),
                  _const_spec((2, 1, LANES)), _const_spec((ATT_HEADS, QT, WBAND)),
                  pl.BlockSpec((ATT_HEADS, QT, ns), lambda b, i: (0, i, 0)),
                  _const_spec((n_sel, ns)), _const_spec((n_sel, T))],
        out_specs=pl.BlockSpec((1, QT, 256), lambda b, i: (b, i, 0)),
        out_shape=jax.ShapeDtypeStruct((B, T, 256), BF16),
        scratch_shapes=[pltpu.VMEM((WINDOW + T, 512), BF16), pltpu.VMEM((ns, LANES), BF16),
                        pltpu.VMEM((ns, LANES), BF16), pltpu.VMEM((2, T, LANES), F32),
                        pltpu.VMEM((ATT_KV_HEADS, T // ft, QT, ft), F32)],
        compiler_params=_cparams("arbitrary", "arbitrary"),
        name="nsa_prompt",
    )(rel_bias, h3, h3, h3, wcat, w2bd, pt, tabw, tabc, covert, expneg)


def _nsa_s1_body(pt_ref, cache_ref, wcat_ref, o_ref, buf_ref, row_ref, sem, *, layer, CH):
    s = pl.program_id(0)
    nsteps = pl.num_programs(0)
    slot = s % 2

    def page_copy(step, p, sl):
        phys = pt_ref[step * CH + p]
        return pltpu.make_async_copy(cache_ref.at[layer, phys, pl.ds(0, 2)], buf_ref.at[sl, p], sem.at[sl])

    def issue(step, sl):
        def one(p, carry):
            page_copy(step, p, sl).start()
            return carry
        lax.fori_loop(0, CH, one, 0)

    @pl.when(s == 0)
    def _():
        issue(0, 0)

    @pl.when(s + 1 < nsteps)
    def _():
        issue(s + 1, 1 - slot)

    def wait_one(p, carry):
        page_copy(s, p, slot).wait()
        return carry
    lax.fori_loop(0, CH, wait_one, 0)

    def xpose(p, carry):
        base = pl.multiple_of(p * LANES, LANES)
        for c in range(2):
            row_ref[c, pl.ds(base, LANES), :] = buf_ref[slot, p, c].reshape(2 * HEAD_DIM, LANES).T
        return carry
    lax.fori_loop(0, CH, xpose, 0, unroll=4 if CH % 4 == 0 else 1)

    nsub = CH * LANES // CMP_STRIDE
    for c in range(2):
        hcat = _compress_pre(lambda r: row_ref[c, pl.ds(r, nsub, stride=CMP_STRIDE), :], wcat_ref, c)
        o_ref[0, :, 2 * LANES * c:2 * LANES * (c + 1)] = hcat


def _nsa_s1(page_flat, cache_t, wcat, layer, B, NP):
    CH = min(64, NP)
    nsub = CH * LANES // CMP_STRIDE
    per_b = NP // CH
    grid_spec = pltpu.PrefetchScalarGridSpec(
        num_scalar_prefetch=1,
        grid=(B * per_b,),
        in_specs=[pl.BlockSpec(memory_space=pl.ANY),
                  pl.BlockSpec((2, CMP_STRIDE, LANES, 2 * LANES), lambda s, pt: (0, 0, 0, 0),
                               pipeline_mode=pl.Buffered(1))],
        out_specs=pl.BlockSpec((1, nsub, 4 * LANES), lambda s, pt: (s // per_b, s % per_b, 0)),
        scratch_shapes=[pltpu.VMEM((2, CH, 2, 2, HEAD_DIM, LANES), F32), pltpu.VMEM((2, CH * LANES, LANES), F32),
                        pltpu.SemaphoreType.DMA((2,))],
    )
    return pl.pallas_call(
        functools.partial(_nsa_s1_body, layer=layer, CH=CH),
        grid_spec=grid_spec,
        out_shape=jax.ShapeDtypeStruct((B, NP * LANES // CMP_STRIDE, 4 * LANES), F32),
        compiler_params=_cparams("arbitrary"),
        name="nsa_sample_compress",
    )(page_flat, cache_t, wcat)


def _row_bias(bk, rb_ref, kh, g_of_row):
    return _bias_lookup(bk, lambda k: jnp.where(g_of_row == 1, rb_ref[k, 2 * kh + 1], rb_ref[k, 2 * kh]))


def _nsa_s2_body(rb_ref, hc_ref, q_ref, w2_ref, pt_ref, cover_ref, pair_ref, ocmp_ref, u_ref, *, P, t_valid, NSELP):
    nsub = hc_ref.shape[1]
    kc = _compress_finish(hc_ref[0, :, 0:2 * LANES], pt_ref[0], w2_ref[0]).astype(BF16)
    vc = _compress_finish(hc_ref[0, :, 2 * LANES:4 * LANES], pt_ref[1], w2_ref[1]).astype(BF16)
    n_sel = P // SEL_BLOCK + 1
    row = lax.broadcasted_iota(I32, (16, 1), 0)
    t16 = row % 8
    g16 = row // 8
    n_i = lax.broadcasted_iota(I32, (1, nsub), 1)
    d_cmp = (P + t16) - (n_i * CMP_STRIDE + CMP_BLOCK - 1)
    ok_cmp = d_cmp >= 0
    bk_cmp = _t5_bucket(d_cmp)
    t8 = lax.broadcasted_iota(I32, (8, 1), 0)
    j_i = lax.broadcasted_iota(I32, (1, NSELP), 1)
    qpos = P + t8
    cur = qpos // SEL_BLOCK
    forced = (j_i == 0) | (j_i == cur) | (j_i == cur - 1)
    avail = (j_i * SEL_BLOCK <= qpos) & (j_i < n_sel)
    wts = jnp.where(t8 < t_valid, jnp.left_shift(1, t8 + 4 * (j_i % 2)), 0).astype(F32)
    u_rows = []
    for kh in range(ATT_KV_HEADS):
        q16 = jnp.concatenate([q_ref[0, :, 128 * kh:128 * kh + 64], q_ref[0, :, 128 * kh + 64:128 * kh + 128]],
                              axis=0).astype(BF16)
        s = _dot_nt(q16, kc[:, 64 * kh:64 * kh + 64]) * ATT_SCALE + _row_bias(bk_cmp, rb_ref, kh, g16)
        (p,) = _masked_softmax_parts([(s, ok_cmp)])
        o = _dot(p.astype(BF16), vc[:, 64 * kh:64 * kh + 64])
        ocmp_ref[0, :, 128 * kh:128 * kh + 64] = o[0:8]
        ocmp_ref[0, :, 128 * kh + 64:128 * kh + 128] = o[8:16]
        p3 = _split3(p[0:8] + p[8:16])
        imp = _dot(p3[0], cover_ref[...]) + _dot(p3[1], cover_ref[...]) + _dot(p3[2], cover_ref[...])
        score = jnp.where(avail, imp + jnp.where(forced, FORCE, 0.0), NEG)
        sel = _topn_select(score, n_sel)
        colsum = jnp.sum(sel * wts, axis=0, keepdims=True)
        u_rows.append(_dot(jnp.broadcast_to(colsum, (8, NSELP)).astype(BF16), pair_ref[...])[0:1])
    u_ref[0] = jnp.concatenate(u_rows + [jnp.zeros((6, 2 * LANES), F32)], axis=0).astype(I32)


def _nsa_s2(hs3, hc, rel_bias, w2bd, pt, cover, pair, P, t_valid):
    B, T, _ = hs3.shape
    nsub = hc.shape[1]
    nselp = cover.shape[1]
    return pl.pallas_call(
        functools.partial(_nsa_s2_body, P=P, t_valid=t_valid, NSELP=nselp),
        grid=(B,),
        in_specs=[_smem_spec(),
                  pl.BlockSpec((1, nsub, 4 * LANES), lambda b: (b, 0, 0)),
                  pl.BlockSpec((1, T, 256), lambda b: (b, 0, C_Q // 256)),
                  _const_spec((2, LANES, LANES)), _const_spec((2, 1, LANES)),
                  _const_spec((nsub, nselp)), _const_spec((nselp, 2 * LANES))],
        out_specs=[pl.BlockSpec((1, 8, 256), lambda b: (b, 0, 0)), pl.BlockSpec((1, 8, 2 * LANES), lambda b: (b, 0, 0))],
        out_shape=[jax.ShapeDtypeStruct((B, 8, 256), F32), jax.ShapeDtypeStruct((B, 8, 2 * LANES), I32)],
        compiler_params=_cparams("arbitrary"),
        name="nsa_sample_cmp_select",
    )(rel_bias, hc, hs3, w2bd, pt, cover, pair)


MAX_SEL_PAGES = 64


def _nsa_s3_body(pt_ref, u_ref, rb_ref, q_ref, sm_ref, kvn_ref, ocmp_ref, cache_ref, win_ref, y_ref,
                 kcat_ref, vcat_ref, rec_ref, slot_u, sem, *, layer, P, NP, t_valid):
    b = pl.program_id(0)
    wb = win_ref.shape[-1]
    ncat = MAX_SEL_PAGES * LANES
    row = lax.broadcasted_iota(I32, (16, 1), 0)
    t16 = row % 8
    g16 = row // 8
    valid_row = t16 < t_valid
    gates = _sigmoid(sm_ref[0])

    @pl.when(b == 0)
    def _():
        kcat_ref[...] = jnp.zeros(kcat_ref.shape, F32)
        vcat_ref[...] = jnp.zeros(vcat_ref.shape, F32)

    def clear(i, carry):
        slot_u[i] = 0
        return carry
    lax.fori_loop(0, 2 * MAX_SEL_PAGES, clear, 0)

    rec_copy = pltpu.make_async_copy(cache_ref.at[layer, pt_ref[b * NP + NP - 1], pl.ds(2, 2)], rec_ref, sem.at[1])
    rec_copy.start()

    def kv_copies(kh, phys, i):
        dst = pl.ds(pl.multiple_of(i * LANES, LANES), LANES)
        return (pltpu.make_async_copy(cache_ref.at[layer, phys, 2, kh], kcat_ref.at[kh, :, dst], sem.at[0]),
                pltpu.make_async_copy(cache_ref.at[layer, phys, 3, kh], vcat_ref.at[kh, :, dst], sem.at[0]))

    def scan(p, cnts):
        new = []
        for kh in range(ATT_KV_HEADS):
            cnt = cnts[kh]
            u = u_ref[(b * 2 + kh) * 2 * LANES + p]

            @pl.when(u != 0)
            def _(kh=kh, cnt=cnt, u=u):
                ck, cv = kv_copies(kh, pt_ref[b * NP + p], cnt)
                ck.start()
                cv.start()
                slot_u[kh * MAX_SEL_PAGES + cnt] = u
            new.append(cnt + jnp.where(u != 0, 1, 0))
        return tuple(new)
    counts = lax.fori_loop(0, NP - 1, scan, (0, 0))

    for kh in range(ATT_KV_HEADS):
        def wait_pair(i, carry, kh=kh):
            ck, cv = kv_copies(kh, 0, i)
            ck.wait()
            cv.wait()
            return carry
        lax.fori_loop(0, counts[kh], wait_pair, 0)
    rec_copy.wait()

    lane_c = lax.broadcasted_iota(I32, (1, ncat), 1)
    shift_c = t16 + 4 * ((lane_c % LANES) // SEL_BLOCK)
    lane = lax.broadcasted_iota(I32, (1, LANES), 1)
    shift = t16 + 4 * (lane // SEL_BLOCK)
    for kh in range(ATT_KV_HEADS):
        q16f = jnp.concatenate([q_ref[0, :, 128 * kh:128 * kh + 64], q_ref[0, :, 128 * kh + 64:128 * kh + 128]], axis=0)
        q16 = q16f.astype(BF16)
        c_far = jnp.where(g16 == 1, rb_ref[N_BUCKETS - 1, 2 * kh + 1], rb_ref[N_BUCKETS - 1, 2 * kh])

        u_vec = jnp.concatenate([jnp.full((1, LANES), slot_u[kh * MAX_SEL_PAGES + i], I32)
                                 for i in range(MAX_SEL_PAGES)], axis=1)
        ok_c = (jnp.bitwise_and(jnp.right_shift(u_vec, shift_c), 1) == 1) & valid_row
        s_c = jnp.where(ok_c, _dot(q16, kcat_ref[kh].astype(BF16)) * ATT_SCALE + c_far, NEG)
        u_rec = u_ref[(b * 2 + kh) * 2 * LANES + NP - 1]
        d_r = (P + t16) - ((NP - 1) * LANES + lane)
        ok_r = (jnp.bitwise_and(jnp.right_shift(u_rec, shift), 1) == 1) & valid_row
        s_r = _dot(q16, rec_ref[0, kh].astype(BF16)) * ATT_SCALE + _row_bias(_t5_bucket(d_r), rb_ref, kh, g16)
        s_r = jnp.where(ok_r, s_r, NEG)
        u_new = u_ref[(b * 2 + kh) * 2 * LANES + NP]
        sel_new = jnp.bitwise_and(jnp.right_shift(u_new, t16), 1) == 1
        m_fin = jnp.maximum(s_c.max(-1, keepdims=True), s_r.max(-1, keepdims=True))
        cols = []
        for c in range(t_valid):
            kn = kvn_ref[0, c:c + 1, 256 + 64 * kh:256 + 64 * kh + 64]
            sc = jnp.sum(q16f * kn, axis=-1, keepdims=True) * ATT_SCALE + _row_bias(_t5_bucket(t16 - c), rb_ref, kh, g16)
            ok = sel_new & (t16 >= c) & valid_row
            sc = jnp.where(ok, sc, NEG)
            cols.append((sc, ok))
            m_fin = jnp.maximum(m_fin, sc)
        e_c = jnp.where(ok_c, jnp.exp(s_c - m_fin), 0.0)
        e_r = jnp.where(ok_r, jnp.exp(s_r - m_fin), 0.0)
        l_fin = e_c.sum(-1, keepdims=True) + e_r.sum(-1, keepdims=True)
        acc = (_dot_nt(e_c.astype(BF16), vcat_ref[kh].astype(BF16))
               + _dot_nt(e_r.astype(BF16), rec_ref[1, kh].astype(BF16)))
        for c, (sc, ok) in enumerate(cols):
            e = jnp.where(ok, jnp.exp(sc - m_fin), 0.0)
            l_fin = l_fin + e
            acc = acc + e * kvn_ref[0, c:c + 1, 384 + 64 * kh:384 + 64 * kh + 64]
        o_sel = acc / jnp.maximum(l_fin, 1e-30)

        col_w = lax.broadcasted_iota(I32, (1, wb), 1)
        d_w = wb + t16 - col_w
        ok_w = (d_w < WINDOW) & valid_row
        s_w = _dot(q16, win_ref[0, 0, kh].astype(BF16)) * ATT_SCALE + _row_bias(_t5_bucket(d_w), rb_ref, kh, g16)
        s_w = jnp.where(ok_w, s_w, NEG)
        mw = s_w.max(-1, keepdims=True)
        cols = []
        for c in range(t_valid):
            kn = kvn_ref[0, c:c + 1, 512 + 64 * kh:512 + 64 * kh + 64]
            sc = jnp.sum(q16f * kn, axis=-1, keepdims=True) * ATT_SCALE + _row_bias(_t5_bucket(t16 - c), rb_ref, kh, g16)
            ok = (t16 >= c) & valid_row
            sc = jnp.where(ok, sc, NEG)
            cols.append((sc, ok))
            mw = jnp.maximum(mw, sc)
        e_w = jnp.where(ok_w, jnp.exp(s_w - mw), 0.0)
        l_w = e_w.sum(-1, keepdims=True)
        acc_w = _dot_nt(e_w.astype(BF16), win_ref[0, 1, kh].astype(BF16))
        for c, (sc, ok) in enumerate(cols):
            e = jnp.where(ok, jnp.exp(sc - mw), 0.0)
            l_w = l_w + e
            acc_w = acc_w + e * kvn_ref[0, c:c + 1, 640 + 64 * kh:640 + 64 * kh + 64]
        o_win = acc_w / jnp.maximum(l_w, 1e-30)

        for g in range(2):
            h = 2 * kh + g
            out = (gates[:, h:h + 1] * ocmp_ref[0, :, 64 * h:64 * h + 64]
                   + gates[:, 4 + h:5 + h] * o_sel[8 * g:8 * g + 8]
                   + gates[:, 8 + h:9 + h] * o_win[8 * g:8 * g + 8])
            y_ref[0, :, 64 * h:64 * h + 64] = out.astype(BF16)


def _nsa_s3(page_flat, u_flat, rel_bias, hs3, ocmp, cache_t, win_t, layer, P, NP, t_valid):
    B, T, _ = hs3.shape
    wb = win_t.shape[-1]
    grid_spec = pltpu.PrefetchScalarGridSpec(
        num_scalar_prefetch=2,
        grid=(B,),
        in_specs=[_smem_spec(),
                  pl.BlockSpec((1, T, 256), lambda b, *_: (b, 0, C_Q // 256)),
                  pl.BlockSpec((1, T, LANES), lambda b, *_: (b, 0, C_SM // LANES)),
                  pl.BlockSpec((1, T, 768), lambda b, *_: (b, 0, C_KV // 768)),
                  pl.BlockSpec((1, 8, 256), lambda b, *_: (b, 0, 0)),
                  pl.BlockSpec(memory_space=pl.ANY),
                  pl.BlockSpec((1, 2, 2, HEAD_DIM, wb), lambda b, *_: (b, 0, 0, 0, 0))],
        out_specs=pl.BlockSpec((1, T, 256), lambda b, *_: (b, 0, 0)),
        scratch_shapes=[pltpu.VMEM((2, HEAD_DIM, MAX_SEL_PAGES * LANES), F32),
                        pltpu.VMEM((2, HEAD_DIM, MAX_SEL_PAGES * LANES), F32),
                        pltpu.VMEM((2, 2, HEAD_DIM, LANES), F32),
                        pltpu.SMEM((2 * MAX_SEL_PAGES,), I32),
                        pltpu.SemaphoreType.DMA((2,))],
    )
    return pl.pallas_call(
        functools.partial(_nsa_s3_body, layer=layer, P=P, NP=NP, t_valid=t_valid),
        grid_spec=grid_spec,
        out_shape=jax.ShapeDtypeStruct((B, T, 256), BF16),
        compiler_params=_cparams("arbitrary"),
        name="nsa_sample_select_window",
    )(page_flat, u_flat, rel_bias, hs3, hs3, hs3, ocmp, cache_t, win_t)


def _blockdiag2(w):
    z = jnp.zeros_like(w)
    return jnp.concatenate([jnp.concatenate([w, z], axis=-1), jnp.concatenate([z, w], axis=-1)], axis=-2)


def _prep_layer(w_in, pool_w, cmp_w1, cmp_w2, alog, dtb):
    glu, pool, q, kv, gate, gqkv, z, a, b = jnp.split(w_in, [512, 768, 1024, 1792, 1804, 2572, 2828, 2832], axis=1)
    pad = jnp.zeros((D_MODEL, IN_PAD - C_SM - 20), F32)
    w_in_p = jnp.concatenate([kv, gqkv, glu, pool, q, z, gate, a, b, pad], axis=1).astype(BF16)
    wblk = jnp.zeros((POOL_CH, POOL_CH), F32)
    for gi in range(4):
        wblk = wblk.at[64 * gi:64 * gi + 64, 64 * gi:64 * gi + 64].set(pool_w[gi])
    w1 = cmp_w1.reshape(2, CMP_BLOCK, HEAD_DIM, HEAD_DIM)
    wcat = jnp.concatenate([_blockdiag2(w1[:, :CMP_STRIDE]), _blockdiag2(w1[:, CMP_STRIDE:])], axis=-1).astype(BF16)
    w2bd = _blockdiag2(cmp_w2).astype(BF16)
    lane_pad = lambda v: jnp.zeros((1, LANES), F32).at[0, SM_A:SM_A + GDN_HEADS].set(v)
    return w_in_p, wblk.astype(BF16), wcat, w2bd, lane_pad(alog), lane_pad(dtb)


def _cover_matrix(nsub, n_sel, cols):
    n = jnp.arange(nsub)[:, None] * CMP_STRIDE
    j = jnp.arange(cols)[None, :] * SEL_BLOCK
    cov = (n < j + SEL_BLOCK) & (n + CMP_BLOCK > j) & (jnp.arange(cols)[None, :] < n_sel) & (jnp.arange(nsub)[:, None] < nsub - 1)
    return cov.astype(BF16)


def _mixers_to_x(x2d, mixes, lw, tm):
    x1 = _proj_out(x2d, mixes, lw["w_out"], lw["ln1_g"], lw["ln1_b"], tm)
    return _ffn(x1, lw["w_up"], lw["w_down"], lw["ln2_g"], lw["ln2_b"], tm)


def kernel(x_prompt, x_sample, cache_nsa_kv, cache_win_kv, state_conv, state_pool, state_gdn_conv, state_gdn,
           page_table, w_in, conv_dw, conv_dw_b, conv_ln_g, conv_ln_b, conv_pw, pool_w, pool_scale,
           cmp_pe, cmp_w1, cmp_w2, gdn_conv_w, gdn_a_log, gdn_dt_bias, gdn_norm_g,
           w_out, ln1_g, ln1_b, w_up, w_down, ln2_g, ln2_b, rel_bias):
    depth = w_in.shape[0]
    BP, T, _ = x_prompt.shape
    BS, TS, _ = x_sample.shape
    NP = page_table.shape[1]
    page = cache_nsa_kv.shape[2]
    P = NP * page
    TSP = 8
    assert page == LANES and TS <= TSP and TS < CMP_STRIDE and T % QT == 0 and P % SEL_BLOCK == 0

    cache_t = jnp.transpose(cache_nsa_kv, (0, 1, 3, 4, 5, 2))
    win_t = jnp.transpose(cache_win_kv, (0, 1, 3, 4, 5, 2))
    page_flat = page_table.reshape(-1)

    tabw, tabc = _bias_tables(rel_bias, T)
    pterm = _peterm(cmp_pe.reshape(depth * 2, 1, CMP_BLOCK * HEAD_DIM), cmp_w1.reshape(depth * 2, CMP_BLOCK * HEAD_DIM, HEAD_DIM))
    pterm = pterm[:, 0:1, :].reshape(depth, 2, 1, HEAD_DIM)
    pterm = jnp.concatenate([pterm, pterm], axis=-1)

    ns_p, nsel_p = T // CMP_STRIDE, T // SEL_BLOCK
    covert_p = _cover_matrix(ns_p, nsel_p, nsel_p).T
    expneg_p = jnp.where(jnp.arange(nsel_p)[:, None] == (jnp.arange(T)[None, :] // SEL_BLOCK), NEG, 0.0).astype(BF16)
    ns_s = P // CMP_STRIDE
    nsel_s = P // SEL_BLOCK + 1
    nselp_s = -(-nsel_s // LANES) * LANES
    cover_s = _cover_matrix(ns_s + 1, nsel_s, nselp_s)[:ns_s]
    jj = jnp.arange(nselp_s)[:, None]
    pp = jnp.arange(2 * LANES)[None, :]
    pair = (((jj // 2 == pp) & (jj < 2 * NP)) | ((jj == 2 * NP) & (pp == NP))).astype(BF16)

    yp = x_prompt.reshape(BP * T, D_MODEL)
    ys = jnp.pad(x_sample, ((0, 0), (0, TSP - TS), (0, 0))).reshape(BS * TSP, D_MODEL)
    zeros = lambda *s: jnp.zeros(s, F32)
    outs_p = [[] for _ in range(6)]
    outs_s = [[] for _ in range(6)]
    for l in range(depth):
        w_in_p, wblk, wcat, w2bd, alog_l, dtb_l = _prep_layer(w_in[l], pool_w[l], cmp_w1[l], cmp_w2[l],
                                                               gdn_a_log[l], gdn_dt_bias[l])
        lw = dict(w_out=w_out[l].astype(BF16), ln1_g=ln1_g[l][None], ln1_b=ln1_b[l][None],
                  w_up=w_up[l].astype(BF16), w_down=w_down[l].astype(BF16), ln2_g=ln2_g[l][None], ln2_b=ln2_b[l][None])
        conv_w = (conv_dw[l], conv_dw_b[l][None], conv_ln_g[l][None], conv_ln_b[l][None], conv_pw[l].astype(BF16))
        gdn_w = (gdn_conv_w[l], alog_l, dtb_l, gdn_norm_g[l][None])

        hp = _proj_in(yp, w_in_p, 512).reshape(BP, T, IN_PAD)
        m_conv, conv_new = _conv_mixer(hp, zeros(BP, CONV_WIDTH - 1, CONV_CH), *conv_w, t_valid=T)
        m_pool, pool_new = _pool_mixer(hp, zeros(BP, POOL_BUF, POOL_CH), wblk, pool_scale[l][None], t_valid=T, offset=0)
        m_att = _nsa_prompt(hp, rel_bias, wcat, w2bd, pterm[l], tabw, tabc, covert_p, expneg_p)
        m_gdn, gbuf_new, s_new = _gdn_mixer(hp, zeros(BP, GDN_CONV - 1, GDN_QKV), zeros(BP, GDN_HEADS, GDN_DK, GDN_DV),
                                            *gdn_w, t_valid=T)
        mixes = [m.reshape(BP * T, GROUP_WIDTH) for m in (m_conv, m_pool, m_att, m_gdn)]
        yp = _mixers_to_x(yp, mixes, lw, 512)
        outs_p[0].append(hp[:, :, C_KV:C_KV + 512].reshape(BP, T, 4, ATT_KV_HEADS, HEAD_DIM))
        wrows = min(WINDOW, T)
        outs_p[1].append(hp[:, T - wrows:, C_KV + 512:C_KV + 768].reshape(BP, wrows, 2, ATT_KV_HEADS, HEAD_DIM))
        for lst, arr in zip(outs_p[2:], (conv_new, pool_new, gbuf_new, s_new)):
            lst.append(arr)

        hs = _proj_in(ys, w_in_p, BS * TSP).reshape(BS, TSP, IN_PAD)
        m_conv, conv_new = _conv_mixer(hs, state_conv[l], *conv_w, t_valid=TS)
        m_pool, pool_new = _pool_mixer(hs, state_pool[l], wblk, pool_scale[l][None], t_valid=TS, offset=P)
        hc = _nsa_s1(page_flat, cache_t, wcat, l, BS, NP)
        ocmp, u = _nsa_s2(hs, hc, rel_bias, w2bd, pterm[l], cover_s, pair, P, TS)
        m_att = _nsa_s3(page_flat, u[:, :2, :].reshape(-1), rel_bias, hs, ocmp, cache_t, win_t[l], l, P, NP, TS)
        m_gdn, gbuf_new, s_new = _gdn_mixer(hs, state_gdn_conv[l], state_gdn[l], *gdn_w, t_valid=TS)
        mixes = [m.reshape(BS * TSP, GROUP_WIDTH) for m in (m_conv, m_pool, m_att, m_gdn)]
        ys = _mixers_to_x(ys, mixes, lw, BS * TSP)
        kv_new = hs[:, :TS, C_KV:C_KV + 768].reshape(BS, TS, 6, ATT_KV_HEADS, HEAD_DIM)
        outs_s[0].append(kv_new[:, :, :4])
        kw_all = jnp.concatenate([cache_win_kv[l], kv_new[:, :, 4:]], axis=1)
        outs_s[1].append(kw_all[:, -min(WINDOW, kw_all.shape[1]):])
        for lst, arr in zip(outs_s[2:], (conv_new, pool_new, gbuf_new, s_new)):
            lst.append(arr)

    p_nsa = jnp.stack(outs_p[0]).reshape(depth, -1, page, 4, ATT_KV_HEADS, HEAD_DIM)
    p_rest = [jnp.stack(a) for a in outs_p[1:]]
    s_all = [jnp.stack(a) for a in outs_s]
    y_s = ys.reshape(BS, TSP, D_MODEL)[:, :TS]
    return (yp.reshape(BP, T, D_MODEL), y_s, p_nsa, *p_rest, *s_all)
```

```python
import functools
import math

import jax
import jax.numpy as jnp
from jax import lax
from jax.experimental import pallas as pl
from jax.experimental.pallas import tpu as pltpu

F32 = jnp.float32
BF16 = jnp.bfloat16
I32 = jnp.int32

D_MODEL = 1024
GROUP_WIDTH = 256
CONV_CH = 256
CONV_WIDTH = 31
POOL_CH = 256
POOL_WINDOWS = (2, 4, 8, 16)
POOL_BUF = 15
ATT_HEADS = 4
ATT_KV_HEADS = 2
HEAD_DIM = 64
ATT_SCALE = HEAD_DIM ** -0.5
CMP_STRIDE = 16
CMP_BLOCK = 32
SEL_BLOCK = 64
SEL_TOPN = 16
WINDOW = 512
N_BUCKETS = 32
GDN_HEADS = 4
GDN_DK = 64
GDN_DV = 64
GDN_QKV = 768
GDN_CONV = 4
GDN_CHUNK = 64
D_FF = 4096
DEPTH = 2
DN_ALPHA = (2 * DEPTH) ** 0.25
LN_EPS = 1e-5
NEG = -1e30
FORCE = 1e4

LANES = 128
VMEM_LIMIT_BYTES = 56 * 1024 * 1024

C_KV, C_GQKV, C_GLU, C_POOL, C_Q, C_Z, C_SM = 0, 768, 1536, 2048, 2304, 2560, 2816
IN_PAD = 2944
SM_GATE, SM_A, SM_B = 0, 12, 16


def _cparams(*sem):
    return pltpu.CompilerParams(dimension_semantics=sem, vmem_limit_bytes=VMEM_LIMIT_BYTES)


def _const_spec(shape):
    nd = len(shape)
    return pl.BlockSpec(shape, lambda *_: (0,) * nd, pipeline_mode=pl.Buffered(1))


def _smem_spec():
    return pl.BlockSpec(memory_space=pltpu.SMEM)


def _sigmoid(x):
    return jax.nn.sigmoid(x)


def _silu(x):
    return x * jax.nn.sigmoid(x)


def _layer_norm(y, g, b):
    mu = jnp.mean(y, axis=-1, keepdims=True)
    yc = y - mu
    var = jnp.mean(yc * yc, axis=-1, keepdims=True)
    return yc * lax.rsqrt(var + LN_EPS) * g + b


def _dot(a, b):
    return jnp.dot(a, b, preferred_element_type=F32)


def _dot_nt(a, b):
    return lax.dot_general(a, b, (((1,), (1,)), ((), ())), preferred_element_type=F32)


def _dot_tn(a, b):
    return lax.dot_general(a, b, (((0,), (0,)), ((), ())), preferred_element_type=F32)


def _split3(x):
    x1 = x.astype(BF16)
    r = x - x1.astype(F32)
    x2 = r.astype(BF16)
    x3 = (r - x2.astype(F32)).astype(BF16)
    return x1, x2, x3


def _dot_hl(a, b):
    ah = a.astype(BF16)
    al = (a - ah.astype(F32)).astype(BF16)
    bh = b.astype(BF16)
    bl = (b - bh.astype(F32)).astype(BF16)
    lhs = jnp.concatenate([ah, al, ah], axis=1)
    rhs = jnp.concatenate([bh, bh, bl], axis=0)
    return _dot(lhs, rhs)


def _t5_bucket(d):
    d = jnp.maximum(d, 0)
    logd = jnp.log(jnp.maximum(d, 1).astype(F32) / 16.0) / math.log(8.0)
    large = jnp.minimum(16 + (logd * 16.0).astype(I32), N_BUCKETS - 1)
    return jnp.where(d < 16, d, large)


def _bias_lookup(bk, value_of_bucket):
    out = jnp.zeros(bk.shape, F32)
    for k in range(N_BUCKETS):
        out = jnp.where(bk == k, value_of_bucket(k), out)
    return out


def _masked_softmax_parts(parts):
    ss = [jnp.where(ok, s, NEG) for s, ok in parts]
    mx = ss[0].max(-1, keepdims=True)
    for s in ss[1:]:
        mx = jnp.maximum(mx, s.max(-1, keepdims=True))
    es = [jnp.where(ok, jnp.exp(s - mx), 0.0) for s, (_, ok) in zip(ss, parts)]
    tot = es[0].sum(-1, keepdims=True)
    for e in es[1:]:
        tot = tot + e.sum(-1, keepdims=True)
    inv = 1.0 / jnp.maximum(tot, 1e-30)
    return [e * inv for e in es]


def _proj_in_body(x_ref, w_ref, o_ref):
    xb = x_ref[...].astype(BF16)
    for a in range(0, IN_PAD, 512):
        b = min(a + 512, IN_PAD)
        o_ref[:, a:b] = _dot(xb, w_ref[:, a:b])


def _proj_in(x, w, tm):
    n = x.shape[0]
    return pl.pallas_call(
        _proj_in_body,
        grid=(n // tm,),
        in_specs=[pl.BlockSpec((tm, D_MODEL), lambda i: (i, 0)), _const_spec((D_MODEL, IN_PAD))],
        out_specs=pl.BlockSpec((tm, IN_PAD), lambda i: (i, 0)),
        out_shape=jax.ShapeDtypeStruct((n, IN_PAD), F32),
        compiler_params=_cparams("arbitrary"),
        name="proj_in",
    )(x, w)


FF_CHUNK = 1024


def _out_ffn_body(x_ref, m0_ref, m1_ref, m2_ref, m3_ref, wo_ref, g1_ref, b1_ref, wu_ref, wd_ref, g2_ref, b2_ref, o_ref):
    acc = _dot(m0_ref[...], wo_ref[0:256, :])
    acc += _dot(m1_ref[...], wo_ref[256:512, :])
    acc += _dot(m2_ref[...], wo_ref[512:768, :])
    acc += _dot(m3_ref[...], wo_ref[768:1024, :])
    x1 = _layer_norm(DN_ALPHA * x_ref[...] + acc, g1_ref[...], b1_ref[...])
    xb = x1.astype(BF16)
    acc = jnp.zeros(x1.shape, F32)
    for c in range(0, D_FF, FF_CHUNK):
        h = _dot(xb, wu_ref[:, c:c + FF_CHUNK])
        a = jnp.square(jnp.maximum(h, 0.0)).astype(BF16)
        acc += _dot(a, wd_ref[c:c + FF_CHUNK, :])
    o_ref[...] = _layer_norm(DN_ALPHA * x1 + acc, g2_ref[...], b2_ref[...])


def _out_ffn(x, mixes, lw, tm):
    n = x.shape[0]
    row = lambda i: (i, 0)
    vec = _const_spec((1, D_MODEL))
    return pl.pallas_call(
        _out_ffn_body,
        grid=(n // tm,),
        in_specs=[pl.BlockSpec((tm, D_MODEL), row)] + [pl.BlockSpec((tm, GROUP_WIDTH), row)] * 4
        + [_const_spec((D_MODEL, D_MODEL)), vec, vec, _const_spec((D_MODEL, D_FF)), _const_spec((D_FF, D_MODEL)), vec, vec],
        out_specs=pl.BlockSpec((tm, D_MODEL), row),
        out_shape=jax.ShapeDtypeStruct((n, D_MODEL), F32),
        compiler_params=_cparams("arbitrary"),
        name="proj_out_ffn",
    )(x, *mixes, lw["w_out"], lw["ln1_g"], lw["ln1_b"], lw["w_up"], lw["w_down"], lw["ln2_g"], lw["ln2_b"])


CONV_PAD = 32


def _conv_body(h_ref, buf_ref, dw_ref, dwb_ref, g_ref, b_ref, pw_ref, y_ref, new_ref, full_ref, *, T, t_valid):
    hh = h_ref[0]
    full_ref[0:8, :] = jnp.zeros((8, CONV_CH), F32)
    full_ref[2:CONV_PAD, :] = buf_ref[0]
    full_ref[CONV_PAD:CONV_PAD + T, :] = hh[:, :CONV_CH] * _sigmoid(hh[:, CONV_CH:])
    new_ref[0] = full_ref[t_valid + 2:t_valid + CONV_PAD, :]
    rc = min(T, 128)

    def chunk(c, carry):
        base = pl.multiple_of(c * rc, rc)
        win = full_ref[pl.ds(base, rc + CONV_PAD), :]
        acc = jnp.zeros((rc, CONV_CH), F32) + dwb_ref[...]
        for r in range(8):
            shifted = win[r:r + (rc + CONV_PAD - r) // 8 * 8, :]
            for k in range(CONV_WIDTH):
                if (2 + k) % 8 == r:
                    a = (2 + k) // 8 * 8
                    acc = acc + dw_ref[k:k + 1, :] * shifted[a:a + rc, :]
        y = _silu(_layer_norm(acc, g_ref[...], b_ref[...]))
        y_ref[0, pl.ds(base, rc), :] = _dot(y.astype(BF16), pw_ref[...]).astype(BF16)
        return carry

    lax.fori_loop(0, T // rc, chunk, 0)


def _conv_mixer(h3, buf, dw, dwb, g, b, pw, t_valid):
    B, T, _ = h3.shape
    return pl.pallas_call(
        functools.partial(_conv_body, T=T, t_valid=t_valid),
        grid=(B,),
        in_specs=[pl.BlockSpec((1, T, 2 * CONV_CH), lambda i: (i, 0, C_GLU // (2 * CONV_CH))),
                  pl.BlockSpec((1, CONV_WIDTH - 1, CONV_CH), lambda i: (i, 0, 0)),
                  _const_spec((CONV_WIDTH, CONV_CH)), _const_spec((1, CONV_CH)), _const_spec((1, CONV_CH)),
                  _const_spec((1, CONV_CH)), _const_spec((CONV_CH, CONV_CH))],
        out_specs=[pl.BlockSpec((1, T, CONV_CH), lambda i: (i, 0, 0)),
                   pl.BlockSpec((1, CONV_WIDTH - 1, CONV_CH), lambda i: (i, 0, 0))],
        out_shape=[jax.ShapeDtypeStruct((B, T, CONV_CH), BF16),
                   jax.ShapeDtypeStruct((B, CONV_WIDTH - 1, CONV_CH), F32)],
        scratch_shapes=[pltpu.VMEM((T + CONV_PAD, CONV_CH), F32)],
        compiler_params=_cparams("arbitrary"),
        name="conv_mixer",
    )(h3, buf, dw, dwb, g, b, pw)


POOL_PAD = 16


def _pool_body(h_ref, buf_ref, w_ref, sc_ref, y_ref, new_ref, full_ref, *, T, t_valid, offset):
    full_ref[0:8, :] = jnp.zeros((8, POOL_CH), F32)
    full_ref[1:POOL_PAD, :] = buf_ref[0]
    full_ref[POOL_PAD:POOL_PAD + T, :] = h_ref[0]
    new_ref[0] = full_ref[t_valid + 1:t_valid + POOL_PAD, :]
    rc = min(T, 128)
    lane = lax.broadcasted_iota(I32, (1, POOL_CH), 1)
    group = lane // (POOL_CH // len(POOL_WINDOWS))
    wl = jnp.where(group == 0, 2, jnp.where(group == 1, 4, jnp.where(group == 2, 8, 16)))

    def chunk(c, carry):
        base = pl.multiple_of(c * rc, rc)
        win = full_ref[pl.ds(base, rc + POOL_PAD), :]
        x0 = win[POOL_PAD:POOL_PAD + rc, :]
        sums = {}
        acc = x0
        for i in range(1, 16):
            acc = acc + win[POOL_PAD - i:POOL_PAD - i + rc, :]
            if i + 1 in POOL_WINDOWS:
                sums[i + 1] = acc
        sel = jnp.where(group == 0, sums[2], jnp.where(group == 1, sums[4], jnp.where(group == 2, sums[8], sums[16])))
        pos = offset + base + lax.broadcasted_iota(I32, (rc, 1), 0)
        cnt = jnp.minimum(pos + 1, wl).astype(F32)
        d = sel / cnt - x0
        y_ref[0, pl.ds(base, rc), :] = (_dot(d.astype(BF16), w_ref[...]) * sc_ref[...]).astype(BF16)
        return carry

    lax.fori_loop(0, T // rc, chunk, 0)


def _pool_mixer(h3, buf, wblk, scale, t_valid, offset):
    B, T, _ = h3.shape
    return pl.pallas_call(
        functools.partial(_pool_body, T=T, t_valid=t_valid, offset=offset),
        grid=(B,),
        in_specs=[pl.BlockSpec((1, T, POOL_CH), lambda i: (i, 0, C_POOL // POOL_CH)),
                  pl.BlockSpec((1, POOL_BUF, POOL_CH), lambda i: (i, 0, 0)),
                  _const_spec((POOL_CH, POOL_CH)), _const_spec((1, POOL_CH))],
        out_specs=[pl.BlockSpec((1, T, POOL_CH), lambda i: (i, 0, 0)),
                   pl.BlockSpec((1, POOL_BUF, POOL_CH), lambda i: (i, 0, 0))],
        out_shape=[jax.ShapeDtypeStruct((B, T, POOL_CH), BF16),
                   jax.ShapeDtypeStruct((B, POOL_BUF, POOL_CH), F32)],
        scratch_shapes=[pltpu.VMEM((T + POOL_PAD, POOL_CH), F32)],
        compiler_params=_cparams("arbitrary"),
        name="pool_mixer",
    )(h3, buf, wblk, scale)


GDN_PAD = 8
CK = GDN_CHUNK


def _gdn_body(qkv_ref, z_ref, sm_ref, buf_ref, s0_ref, cw_ref, alog_ref, dtb_ref, ng_ref,
              y_ref, newbuf_ref, sout_ref, full_ref, c_ref, g_ref, bt_ref, gi_ref, bi_ref, u_ref, w_ref, a_ref, s_ref,
              *, T, Tp, t_valid):
    full_ref[0:8, :] = jnp.zeros((8, GDN_QKV), F32)
    full_ref[5:GDN_PAD, :] = buf_ref[0]
    full_ref[GDN_PAD:GDN_PAD + T, :] = qkv_ref[0]
    newbuf_ref[0] = full_ref[t_valid + 5:t_valid + GDN_PAD, :]
    if Tp > t_valid:
        c_ref[...] = jnp.zeros((Tp, GDN_QKV), F32)
        g_ref[...] = jnp.zeros((Tp, LANES), F32)
        bt_ref[...] = jnp.zeros((Tp, LANES), F32)

    rc = min(t_valid, 128)

    def conv_chunk(c, carry):
        base = pl.multiple_of(c * rc, rc)
        win = full_ref[pl.ds(base, rc + GDN_PAD), :] if rc % 8 == 0 else full_ref[0:rc + GDN_PAD, :]
        acc = jnp.zeros((rc, GDN_QKV), F32)
        for k in range(GDN_CONV):
            acc = acc + cw_ref[k:k + 1, :] * win[5 + k:5 + k + rc, :]
        sm = sm_ref[0, pl.ds(base, rc), :] if rc % 8 == 0 else sm_ref[0, 0:rc, :]
        x = sm + dtb_ref[...]
        softplus = jnp.maximum(x, 0.0) + jnp.log1p(jnp.exp(-jnp.abs(x)))
        gv = -jnp.exp(alog_ref[...]) * softplus
        bv = _sigmoid(sm)
        if rc % 8 == 0:
            c_ref[pl.ds(base, rc), :] = _silu(acc)
            g_ref[pl.ds(base, rc), :] = gv
            bt_ref[pl.ds(base, rc), :] = bv
        else:
            c_ref[0:rc, :] = _silu(acc)
            g_ref[0:rc, :] = gv
            bt_ref[0:rc, :] = bv
        return carry

    lax.fori_loop(0, t_valid // rc, conv_chunk, 0)

    HW = GDN_HEADS * GDN_DK
    lane = lax.broadcasted_iota(I32, (1, HW), 1)
    hmask = [jnp.where(lane // GDN_DK == h, 1.0, 0.0).astype(BF16) for h in range(GDN_HEADS)]
    row = lax.broadcasted_iota(I32, (CK, 1), 0)
    jl = lane % CK
    incl = row >= jl
    strict = row > jl
    eye_all = jnp.where(row == jl, 1.0, 0.0)
    er = lax.broadcasted_iota(I32, (LANES, HW), 0)
    ec = lax.broadcasted_iota(I32, (LANES, HW), 1) // GDN_DK
    exp_a = jnp.where(er == ec + SM_A, 1.0, 0.0).astype(BF16)
    exp_b = jnp.where(er == ec + SM_B, 1.0, 0.0).astype(BF16)
    br = lax.broadcasted_iota(I32, (HW, HW), 0) // GDN_DK
    bc = lax.broadcasted_iota(I32, (HW, HW), 1) // GDN_DK
    same_head = br == bc
    bones = jnp.where(same_head, 1.0, 0.0).astype(BF16)

    def blockdiag(x):
        return jnp.concatenate([x * m for m in hmask], axis=0)

    def expand3(x, e):
        x1, x2, x3 = _split3(x)
        return _dot(jnp.concatenate([x1, x2, x3], axis=1), jnp.concatenate([e, e, e], axis=0))

    def bd_dot_hl(a, b):
        ah = a.astype(BF16)
        al = (a - ah.astype(F32)).astype(BF16)
        bh = b.astype(BF16)
        bl = (b - bh.astype(F32)).astype(BF16)
        bdh = blockdiag(bh)
        return _dot(jnp.concatenate([ah, al, ah], axis=1), jnp.concatenate([bdh, bdh, blockdiag(bl)], axis=0))

    rb = min(Tp, 4 * CK)
    row_in_chunk = lax.broadcasted_iota(I32, (rb, 1), 0) % CK

    def prep(c, carry):
        r0 = pl.multiple_of(c * rb, rb)
        g = g_ref[pl.ds(r0, rb), :]
        for s in (1, 2, 4, 8, 16, 32):
            g = g + jnp.where(row_in_chunk >= s, jnp.roll(g, s, axis=0), 0.0)
        gi_ref[pl.ds(r0, rb), :] = expand3(g, exp_a)
        bi_ref[pl.ds(r0, rb), :] = expand3(bt_ref[pl.ds(r0, rb), :], exp_b)
        for part, scale in ((0, GDN_DK ** -0.5), (1, 1.0)):
            x = c_ref[pl.ds(r0, rb), HW * part:HW * (part + 1)]
            ssq = expand3(x * x, bones)
            c_ref[pl.ds(r0, rb), HW * part:HW * (part + 1)] = x * lax.rsqrt(ssq + 1e-6) * scale
        return carry

    lax.fori_loop(0, Tp // rb, prep, 0)

    n_chunks = Tp // CK
    group = 4 if n_chunks % 4 == 0 else 1

    def solve(it, carry):
        r0s = [pl.multiple_of((it * group + k) * CK, CK) for k in range(group)]
        gi = [gi_ref[pl.ds(r0, CK), :] for r0 in r0s]
        kn = [c_ref[pl.ds(r0, CK), HW:2 * HW] for r0 in r0s]
        decay = [jnp.exp(jnp.where(incl, g - jnp.sum(eye_all * g, axis=0, keepdims=True), NEG)) for g in gi]
        kb = [k * bi_ref[pl.ds(r0, CK), :] for k, r0 in zip(kn, r0s)]
        kst = [blockdiag(k.astype(BF16)) for k in kn]
        pw = [-jnp.where(strict, _dot_nt(b.astype(BF16), s) * d, 0.0) for b, s, d in zip(kb, kst, decay)]
        tinv = [eye_all + p for p in pw]
        pw = [bd_dot_hl(p, p) for p in pw]
        for _ in range(4):
            both = [bd_dot_hl(jnp.concatenate([t, p], axis=0), p) for t, p in zip(tinv, pw)]
            tinv = [t + b[:CK] for t, b in zip(tinv, both)]
            pw = [b[CK:] for b in both]
        tinv = [t + bd_dot_hl(t, p) for t, p in zip(tinv, pw)]
        for k, r0 in enumerate(r0s):
            vb = c_ref[pl.ds(r0, CK), 2 * HW:3 * HW] * bi_ref[pl.ds(r0, CK), :]
            u_ref[pl.ds(r0, CK), :] = bd_dot_hl(tinv[k], vb)
        for k, r0 in enumerate(r0s):
            w_ref[pl.ds(r0, CK), :] = bd_dot_hl(tinv[k], kb[k] * jnp.exp(gi[k]))
        for k, r0 in enumerate(r0s):
            qn = c_ref[pl.ds(r0, CK), 0:HW]
            a_ref[pl.ds(r0, CK), :] = (_dot_nt(qn.astype(BF16), kst[k]) * decay[k]).astype(BF16)
        return carry

    lax.fori_loop(0, n_chunks // group, solve, 0)

    s_ref[...] = jnp.zeros((HW, HW), F32)
    for h in range(GDN_HEADS):
        s_ref[GDN_DK * h:GDN_DK * (h + 1), GDN_DV * h:GDN_DV * (h + 1)] = s0_ref[0, h]
    rows_out = min(CK, T)
    ng_all = jnp.concatenate([ng_ref[...]] * GDN_HEADS, axis=1)

    def recur(c, carry):
        r0 = pl.multiple_of(c * CK, CK)
        gi = gi_ref[pl.ds(r0, CK), :]
        qn = c_ref[pl.ds(r0, CK), 0:HW]
        kn = c_ref[pl.ds(r0, CK), HW:2 * HW]
        s_all = s_ref[...]
        s_b = s_all.astype(BF16)
        vnew = u_ref[pl.ds(r0, CK), :] - _dot(w_ref[pl.ds(r0, CK), :].astype(BF16), s_b)
        vnb = vnew.astype(BF16)
        glast = gi[CK - 1:CK, :]
        kd = kn * jnp.exp(glast - gi)
        s_ref[...] = s_all * jnp.exp(glast) + jnp.where(same_head, _dot_tn(kd.astype(BF16), vnb), 0.0)
        u_ref[pl.ds(r0, CK), :] = (_dot((qn * jnp.exp(gi)).astype(BF16), s_b)
                                   + _dot(a_ref[pl.ds(r0, CK), :], blockdiag(vnb)))
        return carry

    lax.fori_loop(0, Tp // CK, recur, 0)
    for h in range(GDN_HEADS):
        sout_ref[0, h] = s_ref[GDN_DK * h:GDN_DK * (h + 1), GDN_DV * h:GDN_DV * (h + 1)]

    def finish(c, carry):
        r0 = pl.multiple_of(c * rb, rb)
        o = u_ref[pl.ds(r0, rb), :]
        on = o * lax.rsqrt(expand3(o * o, bones) * (1.0 / GDN_DV) + LN_EPS) * ng_all
        if T >= CK:
            y_ref[0, pl.ds(r0, rb), :] = (on * _silu(z_ref[0, pl.ds(r0, rb), :])).astype(BF16)
        else:
            y_ref[0] = (on[0:rows_out] * _silu(z_ref[0])).astype(BF16)
        return carry

    lax.fori_loop(0, Tp // rb, finish, 0)


def _gdn_mixer(h3, buf, s0, cw, alog_l, dtb_l, ng, t_valid):
    B, T, _ = h3.shape
    Tp = -(-T // CK) * CK
    return pl.pallas_call(
        functools.partial(_gdn_body, T=T, Tp=Tp, t_valid=t_valid),
        grid=(B,),
        in_specs=[pl.BlockSpec((1, T, GDN_QKV), lambda i: (i, 0, C_GQKV // GDN_QKV)),
                  pl.BlockSpec((1, T, 256), lambda i: (i, 0, C_Z // 256)),
                  pl.BlockSpec((1, T, LANES), lambda i: (i, 0, C_SM // LANES)),
                  pl.BlockSpec((1, GDN_CONV - 1, GDN_QKV), lambda i: (i, 0, 0)),
                  pl.BlockSpec((1, GDN_HEADS, GDN_DK, GDN_DV), lambda i: (i, 0, 0, 0)),
                  _const_spec((GDN_CONV, GDN_QKV)), _const_spec((1, LANES)), _const_spec((1, LANES)),
                  _const_spec((1, GDN_DV))],
        out_specs=[pl.BlockSpec((1, T, 256), lambda i: (i, 0, 0)),
                   pl.BlockSpec((1, GDN_CONV - 1, GDN_QKV), lambda i: (i, 0, 0)),
                   pl.BlockSpec((1, GDN_HEADS, GDN_DK, GDN_DV), lambda i: (i, 0, 0, 0))],
        out_shape=[jax.ShapeDtypeStruct((B, T, 256), BF16),
                   jax.ShapeDtypeStruct((B, GDN_CONV - 1, GDN_QKV), F32),
                   jax.ShapeDtypeStruct((B, GDN_HEADS, GDN_DK, GDN_DV), F32)],
        scratch_shapes=[pltpu.VMEM((T + GDN_PAD, GDN_QKV), F32), pltpu.VMEM((Tp, GDN_QKV), F32),
                        pltpu.VMEM((Tp, LANES), F32), pltpu.VMEM((Tp, LANES), F32),
                        pltpu.VMEM((Tp, 256), F32), pltpu.VMEM((Tp, 256), F32),
                        pltpu.VMEM((Tp, 256), F32), pltpu.VMEM((Tp, 256), F32), pltpu.VMEM((Tp, 256), BF16),
                        pltpu.VMEM((GDN_HEADS * GDN_DK, GDN_HEADS * GDN_DV), F32)],
        compiler_params=_cparams("arbitrary"),
        name="gdn_mixer",
    )(h3, h3, h3, buf, s0, cw, alog_l, dtb_l, ng)


def _peterm_body(pe_ref, w1_ref, o_ref):
    pe = jnp.broadcast_to(pe_ref[0], (8, CMP_BLOCK * HEAD_DIM)).astype(BF16)
    o_ref[0] = _dot(pe, w1_ref[0].astype(BF16))


def _peterm(pe_flat, w1):
    n = pe_flat.shape[0]
    return pl.pallas_call(
        _peterm_body,
        grid=(n,),
        in_specs=[pl.BlockSpec((1, 1, CMP_BLOCK * HEAD_DIM), lambda i: (i, 0, 0)),
                  pl.BlockSpec((1, CMP_BLOCK * HEAD_DIM, HEAD_DIM), lambda i: (i, 0, 0))],
        out_specs=pl.BlockSpec((1, 8, HEAD_DIM), lambda i: (i, 0, 0)),
        out_shape=jax.ShapeDtypeStruct((n, 8, HEAD_DIM), F32),
        compiler_params=_cparams("arbitrary"),
        name="cmp_pe_term",
    )(pe_flat, w1)


QT = 128
WBAND = WINDOW + QT


def _tabw_body(rb_ref, o_ref):
    i = lax.broadcasted_iota(I32, (QT, WBAND), 0)
    j = lax.broadcasted_iota(I32, (QT, WBAND), 1)
    bk = _t5_bucket(WINDOW + i - j)
    for h in range(ATT_HEADS):
        o_ref[h] = _bias_lookup(bk, lambda k: rb_ref[k, h])


def _tabc_body(rb_ref, o_ref):
    p0 = pl.program_id(0) * QT
    ns = o_ref.shape[-1]
    t = p0 + lax.broadcasted_iota(I32, (QT, ns), 0)
    n = lax.broadcasted_iota(I32, (QT, ns), 1)
    bk = _t5_bucket(t - (n * CMP_STRIDE + CMP_BLOCK - 1))
    for h in range(ATT_HEADS):
        o_ref[h] = _bias_lookup(bk, lambda k: rb_ref[k, h])


def _bias_tables(rel_bias, T):
    ns = T // CMP_STRIDE
    tabw = pl.pallas_call(
        _tabw_body, in_specs=[_smem_spec()],
        out_shape=jax.ShapeDtypeStruct((ATT_HEADS, QT, WBAND), F32), name="bias_window_table")(rel_bias)
    tabc = pl.pallas_call(
        _tabc_body, grid=(T // QT,), in_specs=[_smem_spec()],
        out_specs=pl.BlockSpec((ATT_HEADS, QT, ns), lambda i: (0, i, 0)),
        out_shape=jax.ShapeDtypeStruct((ATT_HEADS, T, ns), F32),
        compiler_params=_cparams("arbitrary"), name="bias_cmp_table")(rel_bias)
    return tabw, tabc


def _compress_pre(load_rows, wcat_ref, c):
    acc = None
    for r in range(0, CMP_STRIDE, 2):
        lhs = jnp.concatenate([load_rows(r), load_rows(r + 1)], axis=1).astype(BF16)
        part = _dot(lhs, wcat_ref[c, r // 2])
        acc = part if acc is None else acc + part
    return acc


def _compress_finish(hcat, pt, w2):
    pre = hcat[:, :LANES] + jnp.roll(hcat[:, LANES:], -1, axis=0) + pt
    return _dot(_silu(pre).astype(BF16), w2)


def _topn_select(score, n_cols):
    j = lax.broadcasted_iota(I32, score.shape, 1)
    rank = jnp.zeros(score.shape, F32)
    for jp in range(n_cols):
        col = score[:, jp:jp + 1]
        ahead = (col > score) | ((col == score) & (jp < j))
        rank = rank + jnp.where(ahead, 1.0, 0.0)
    return jnp.where((rank < SEL_TOPN) & (score > 0.5 * NEG), 1.0, 0.0)


def _topn_select_rows(score_t, n_rows):
    j = lax.broadcasted_iota(I32, score_t.shape, 0)
    rank = jnp.zeros(score_t.shape, F32)
    for jp in range(n_rows):
        r = score_t[jp:jp + 1, :]
        ahead = (r > score_t) | ((r == score_t) & (jp < j))
        rank = rank + jnp.where(ahead, 1.0, 0.0)
    return jnp.where((rank < SEL_TOPN) & (score_t > 0.5 * NEG), 1.0, 0.0)


def _nsa_prompt_body(rb_ref, q_ref, sm_ref, kv_ref, wcat_ref, w2_ref, pt_ref, tabw_ref, tabc_ref, covert_ref,
                     expneg_ref, y_ref, kvp_ref, kc_ref, vc_ref, cmp_ref, ma_ref, *, T):
    ns = T // CMP_STRIDE
    n_sel = T // SEL_BLOCK
    qt = pl.program_id(1)
    p0 = pl.multiple_of(qt * QT, QT)

    @pl.when(qt == 0)
    def _():
        kvp_ref[0:WINDOW, :] = jnp.zeros((WINDOW, 512), BF16)
        kvp_ref[WINDOW:WINDOW + T, :] = kv_ref[0, :, 256:768].astype(BF16)
        for c, dst in ((0, kc_ref), (1, vc_ref)):
            cmp_ref[c] = kv_ref[0, :, LANES * c:LANES * (c + 1)]
            hcat = _compress_pre(lambda r: cmp_ref[c, pl.ds(r, ns, stride=CMP_STRIDE), :], wcat_ref, c)
            dst[...] = _compress_finish(hcat, pt_ref[c], w2_ref[c]).astype(BF16)

    FT = ma_ref.shape[-1]
    gates = _sigmoid(sm_ref[0])
    t = p0 + lax.broadcasted_iota(I32, (QT, 1), 0)
    n_i = lax.broadcasted_iota(I32, (1, ns), 1)
    ok_cmp = (t - (n_i * CMP_STRIDE + CMP_BLOCK - 1) >= 0) & (n_i < ns - 1)
    ok_cmp2 = jnp.concatenate([ok_cmp, ok_cmp], axis=0)
    t_l = p0 + lax.broadcasted_iota(I32, (1, QT), 1)
    j_s = lax.broadcasted_iota(I32, (n_sel, 1), 0)
    cur = t_l // SEL_BLOCK
    forced = (j_s == 0) | (j_s == cur) | (j_s == cur - 1)
    avail = j_s * SEL_BLOCK <= t_l
    m_near = p0 - QT + lax.broadcasted_iota(I32, (1, 2 * QT), 1)
    near_blk = (p0 - QT + lax.broadcasted_iota(I32, (n_sel, 2 * QT), 1)) // SEL_BLOCK
    e_near_neg = jnp.where(near_blk == j_s, NEG, 0.0).astype(BF16)
    causal_near = jnp.where((m_near >= 0) & (m_near <= t), 0.0, NEG)
    m_win = p0 - WINDOW + lax.broadcasted_iota(I32, (1, WBAND), 1)
    d_win = t - m_win
    add_win = jnp.where((m_win >= 0) & (d_win >= 0) & (d_win < WINDOW), 0.0, NEG)
    two = lambda x: jnp.concatenate([x, x], axis=0)

    KH = range(ATT_KV_HEADS)
    kcol = lambda kh, base: slice(base + 64 * kh, base + 64 * kh + 64)
    q2 = [(jnp.concatenate([q_ref[0, :, kcol(2 * kh, 0)], q_ref[0, :, kcol(2 * kh + 1, 0)]], axis=0)
           * ATT_SCALE).astype(BF16) for kh in KH]
    s_c = [_dot_nt(q2[kh], kc_ref[:, kcol(kh, 0)]) + jnp.concatenate([tabc_ref[2 * kh], tabc_ref[2 * kh + 1]], axis=0)
           for kh in KH]
    p_c = [_masked_softmax_parts([(s, ok_cmp2)])[0] for s in s_c]
    o_cmp = [_dot(p_c[kh].astype(BF16), vc_ref[:, kcol(kh, 0)]) for kh in KH]
    nsel_t = []
    for kh in KH:
        p3 = _split3(p_c[kh][0:QT] + p_c[kh][QT:2 * QT])
        imp_t = (_dot_nt(covert_ref[...], p3[0]) + _dot_nt(covert_ref[...], p3[1])
                 + _dot_nt(covert_ref[...], p3[2]))
        score_t = jnp.where(avail, imp_t + jnp.where(forced, FORCE, 0.0), NEG)
        nsel_t.append((1.0 - _topn_select_rows(score_t, n_sel)).astype(BF16))
    for kh in KH:
        ma_all = _dot_tn(nsel_t[kh], expneg_ref[...])
        for i in range(T // FT):
            ma_ref[kh, i] = ma_all[:, i * FT:(i + 1) * FT]
    add_near = [_dot_tn(nsel_t[kh], e_near_neg) + causal_near for kh in KH]
    near_rows = pl.ds(WINDOW + p0 - QT, 2 * QT)
    bias_near = [jnp.concatenate(
        [tabw_ref[2 * kh + g, :, WINDOW - QT:WINDOW + QT] - rb_ref[N_BUCKETS - 1, 2 * kh + g] + add_near[kh]
         for g in range(2)], axis=0) for kh in KH]
    s_n = [_dot_nt(q2[kh], kvp_ref[near_rows, kcol(kh, 0)]) + bias_near[kh] for kh in KH]
    m0 = [s.max(-1, keepdims=True) for s in s_n]
    e_n = [jnp.exp(s - m) for s, m in zip(s_n, m0)]
    init = tuple((m0[kh], e_n[kh].sum(-1, keepdims=True), _dot(e_n[kh].astype(BF16), kvp_ref[near_rows, kcol(kh, 128)]))
                 for kh in KH)

    def far(i, carry):
        k0 = pl.multiple_of(i * FT, FT)
        rows = pl.ds(WINDOW + k0, FT)
        lim = jnp.where(k0 + lax.broadcasted_iota(I32, (1, FT), 1) < p0 - QT, 0.0, NEG)
        s = [_dot_nt(q2[kh], kvp_ref[rows, kcol(kh, 0)]) + two(ma_ref[kh, i] + lim) for kh in KH]
        m_new = [jnp.maximum(carry[kh][0], s[kh].max(-1, keepdims=True)) for kh in KH]
        alpha = [jnp.exp(carry[kh][0] - m_new[kh]) for kh in KH]
        e = [jnp.exp(s[kh] - m_new[kh]) for kh in KH]
        return tuple((m_new[kh], alpha[kh] * carry[kh][1] + e[kh].sum(-1, keepdims=True),
                      alpha[kh] * carry[kh][2] + _dot(e[kh].astype(BF16), kvp_ref[rows, kcol(kh, 128)])) for kh in KH)

    n_far = (jnp.maximum(p0 - QT, 0) + FT - 1) // FT
    fin = lax.fori_loop(0, n_far, far, init)
    o_sel = [fin[kh][2] / jnp.maximum(fin[kh][1], 1e-30) for kh in KH]
    win_rows = pl.ds(p0, WBAND)
    s_w = [_dot_nt(q2[kh], kvp_ref[win_rows, kcol(kh, 256)])
           + jnp.concatenate([tabw_ref[2 * kh] + add_win, tabw_ref[2 * kh + 1] + add_win], axis=0) for kh in KH]
    e_w = [jnp.exp(s - s.max(-1, keepdims=True)) for s in s_w]
    o_win = [_dot(e_w[kh].astype(BF16), kvp_ref[win_rows, kcol(kh, 384)])
             / jnp.maximum(e_w[kh].sum(-1, keepdims=True), 1e-30) for kh in KH]
    for h in range(ATT_HEADS):
        kh, g = divmod(h, 2)
        rows = slice(QT * g, QT * (g + 1))
        out = (gates[:, h:h + 1] * o_cmp[kh][rows] + gates[:, 4 + h:5 + h] * o_sel[kh][rows]
               + gates[:, 8 + h:9 + h] * o_win[kh][rows])
        y_ref[0, :, 64 * h:64 * h + 64] = out.astype(BF16)


def _nsa_prompt(h3, rel_bias, wcat, w2bd, pt, tabw, tabc, covert, expneg):
    B, T, _ = h3.shape
    ns = T // CMP_STRIDE
    n_sel = T // SEL_BLOCK
    ft = min(512, T)
    return pl.pallas_call(
        functools.partial(_nsa_prompt_body, T=T),
        grid=(B, T // QT),
        in_specs=[_smem_spec(),
                  pl.BlockSpec((1, QT, 256), lambda b, i: (b, i, C_Q // 256)),
                  pl.BlockSpec((1, QT, LANES), lambda b, i: (b, i, C_SM // LANES)),
                  pl.BlockSpec((1, T, 768), lambda b, i: (b, 0, C_KV // 768)),
                  _const_spec((2, CMP_STRIDE // 2, 2 * LANES, 2 * LANES)), _const_spec((2, LANES, LANES)),
                  _const_spec((2, 1, LANES)), _const_spec((ATT_HEADS, QT, WBAND)),
                  pl.BlockSpec((ATT_HEADS, QT, ns), lambda b, i: (0, i, 0)),
                  _const_spec((n_sel, ns)), _const_spec((n_sel, T))],
        out_specs=pl.BlockSpec((1, QT, 256), lambda b, i: (b, i, 0)),
        out_shape=jax.ShapeDtypeStruct((B, T, 256), BF16),
        scratch_shapes=[pltpu.VMEM((WINDOW + T, 512), BF16), pltpu.VMEM((ns, LANES), BF16),
                        pltpu.VMEM((ns, LANES), BF16), pltpu.VMEM((2, T, LANES), F32),
                        pltpu.VMEM((ATT_KV_HEADS, T // ft, QT, ft), F32)],
        compiler_params=_cparams("arbitrary", "arbitrary"),
        name="nsa_prompt",
    )(rel_bias, h3, h3, h3, wcat, w2bd, pt, tabw, tabc, covert, expneg)


def _nsa_s1_body(pt_ref, cache_ref, wcat_ref, o_ref, buf_ref, row_ref, sem, *, layer, CH):
    s = pl.program_id(0)
    nsteps = pl.num_programs(0)
    slot = s % 2

    def page_copy(step, p, sl):
        phys = pt_ref[step * CH + p]
        return pltpu.make_async_copy(cache_ref.at[layer, phys, pl.ds(0, 2)], buf_ref.at[sl, p], sem.at[sl])

    def issue(step, sl):
        def one(p, carry):
            page_copy(step, p, sl).start()
            return carry
        lax.fori_loop(0, CH, one, 0)

    @pl.when(s == 0)
    def _():
        issue(0, 0)

    @pl.when(s + 1 < nsteps)
    def _():
        issue(s + 1, 1 - slot)

    def wait_one(p, carry):
        page_copy(s, p, slot).wait()
        return carry
    lax.fori_loop(0, CH, wait_one, 0)

    gp = 16 if CH % 16 == 0 else CH
    gsub = gp * LANES // CMP_STRIDE
    for g in range(CH // gp):
        for p in range(g * gp, (g + 1) * gp):
            for c in range(2):
                row_ref[c, LANES * p:LANES * (p + 1), :] = buf_ref[slot, p, c].reshape(2 * HEAD_DIM, LANES).T
        for c in range(2):
            hcat = _compress_pre(lambda r: row_ref[c, pl.ds(g * gp * LANES + r, gsub, stride=CMP_STRIDE), :], wcat_ref, c)
            o_ref[0, g * gsub:(g + 1) * gsub, 2 * LANES * c:2 * LANES * (c + 1)] = hcat


def _nsa_s1(page_flat, cache_t, wcat, layer, B, NP):
    CH = min(64, NP)
    nsub = CH * LANES // CMP_STRIDE
    per_b = NP // CH
    grid_spec = pltpu.PrefetchScalarGridSpec(
        num_scalar_prefetch=1,
        grid=(B * per_b,),
        in_specs=[pl.BlockSpec(memory_space=pl.ANY),
                  pl.BlockSpec((2, CMP_STRIDE // 2, 2 * LANES, 2 * LANES), lambda s, pt: (0, 0, 0, 0),
                               pipeline_mode=pl.Buffered(1))],
        out_specs=pl.BlockSpec((1, nsub, 4 * LANES), lambda s, pt: (s // per_b, s % per_b, 0)),
        scratch_shapes=[pltpu.VMEM((2, CH, 2, 2, HEAD_DIM, LANES), F32), pltpu.VMEM((2, CH * LANES, LANES), F32),
                        pltpu.SemaphoreType.DMA((2,))],
    )
    return pl.pallas_call(
        functools.partial(_nsa_s1_body, layer=layer, CH=CH),
        grid_spec=grid_spec,
        out_shape=jax.ShapeDtypeStruct((B, NP * LANES // CMP_STRIDE, 4 * LANES), F32),
        compiler_params=_cparams("arbitrary"),
        name="nsa_sample_compress",
    )(page_flat, cache_t, wcat)


def _row_bias(bk, rb_ref, kh, g_of_row):
    return _bias_lookup(bk, lambda k: jnp.where(g_of_row == 1, rb_ref[k, 2 * kh + 1], rb_ref[k, 2 * kh]))


def _nsa_s2_body(rb_ref, hc_ref, q_ref, w2_ref, pt_ref, cover_ref, pair_ref, ocmp_ref, u_ref, *, P, t_valid, NSELP):
    nsub = hc_ref.shape[1]
    kc = _compress_finish(hc_ref[0, :, 0:2 * LANES], pt_ref[0], w2_ref[0]).astype(BF16)
    vc = _compress_finish(hc_ref[0, :, 2 * LANES:4 * LANES], pt_ref[1], w2_ref[1]).astype(BF16)
    n_sel = P // SEL_BLOCK + 1
    row = lax.broadcasted_iota(I32, (16, 1), 0)
    t16 = row % 8
    g16 = row // 8
    n_i = lax.broadcasted_iota(I32, (1, nsub), 1)
    d_cmp = (P + t16) - (n_i * CMP_STRIDE + CMP_BLOCK - 1)
    ok_cmp = d_cmp >= 0
    bk_cmp = _t5_bucket(d_cmp)
    t8 = lax.broadcasted_iota(I32, (8, 1), 0)
    j_i = lax.broadcasted_iota(I32, (1, NSELP), 1)
    qpos = P + t8
    cur = qpos // SEL_BLOCK
    forced = (j_i == 0) | (j_i == cur) | (j_i == cur - 1)
    avail = (j_i * SEL_BLOCK <= qpos) & (j_i < n_sel)
    wts = jnp.where(t8 < t_valid, jnp.left_shift(1, t8 + 4 * (j_i % 2)), 0).astype(F32)
    u_rows = []
    for kh in range(ATT_KV_HEADS):
        q16 = jnp.concatenate([q_ref[0, :, 128 * kh:128 * kh + 64], q_ref[0, :, 128 * kh + 64:128 * kh + 128]],
                              axis=0).astype(BF16)
        s = _dot_nt(q16, kc[:, 64 * kh:64 * kh + 64]) * ATT_SCALE + _row_bias(bk_cmp, rb_ref, kh, g16)
        (p,) = _masked_softmax_parts([(s, ok_cmp)])
        o = _dot(p.astype(BF16), vc[:, 64 * kh:64 * kh + 64])
        ocmp_ref[0, :, 128 * kh:128 * kh + 64] = o[0:8]
        ocmp_ref[0, :, 128 * kh + 64:128 * kh + 128] = o[8:16]
        p3 = _split3(p[0:8] + p[8:16])
        imp = _dot(p3[0], cover_ref[...]) + _dot(p3[1], cover_ref[...]) + _dot(p3[2], cover_ref[...])
        score = jnp.where(avail, imp + jnp.where(forced, FORCE, 0.0), NEG)
        sel = _topn_select(score, n_sel)
        colsum = jnp.sum(sel * wts, axis=0, keepdims=True)
        u_rows.append(_dot(jnp.broadcast_to(colsum, (8, NSELP)).astype(BF16), pair_ref[...])[0:1])
    u_ref[0] = jnp.concatenate(u_rows + [jnp.zeros((6, 2 * LANES), F32)], axis=0).astype(I32)


def _nsa_s2(hs3, hc, rel_bias, w2bd, pt, cover, pair, P, t_valid):
    B, T, _ = hs3.shape
    nsub = hc.shape[1]
    nselp = cover.shape[1]
    return pl.pallas_call(
        functools.partial(_nsa_s2_body, P=P, t_valid=t_valid, NSELP=nselp),
        grid=(B,),
        in_specs=[_smem_spec(),
                  pl.BlockSpec((1, nsub, 4 * LANES), lambda b: (b, 0, 0)),
                  pl.BlockSpec((1, T, 256), lambda b: (b, 0, C_Q // 256)),
                  _const_spec((2, LANES, LANES)), _const_spec((2, 1, LANES)),
                  _const_spec((nsub, nselp)), _const_spec((nselp, 2 * LANES))],
        out_specs=[pl.BlockSpec((1, 8, 256), lambda b: (b, 0, 0)), pl.BlockSpec((1, 8, 2 * LANES), lambda b: (b, 0, 0))],
        out_shape=[jax.ShapeDtypeStruct((B, 8, 256), F32), jax.ShapeDtypeStruct((B, 8, 2 * LANES), I32)],
        compiler_params=_cparams("arbitrary"),
        name="nsa_sample_cmp_select",
    )(rel_bias, hc, hs3, w2bd, pt, cover, pair)


MAX_SEL_PAGES = 64


def _nsa_s3_body(pt_ref, u_ref, rb_ref, q_ref, sm_ref, kvn_ref, ocmp_ref, cache_ref, win_ref, y_ref,
                 kcat_ref, vcat_ref, rec_ref, slot_u, sem, *, layer, P, NP, t_valid):
    b = pl.program_id(0)
    wb = win_ref.shape[-1]
    ncat = MAX_SEL_PAGES * LANES
    row = lax.broadcasted_iota(I32, (16, 1), 0)
    t16 = row % 8
    g16 = row // 8
    valid_row = t16 < t_valid
    gates = _sigmoid(sm_ref[0])

    @pl.when(b == 0)
    def _():
        kcat_ref[...] = jnp.zeros(kcat_ref.shape, F32)
        vcat_ref[...] = jnp.zeros(vcat_ref.shape, F32)

    def clear(i, carry):
        slot_u[i] = 0
        return carry
    lax.fori_loop(0, 2 * MAX_SEL_PAGES, clear, 0)

    rec_copy = pltpu.make_async_copy(cache_ref.at[layer, pt_ref[b * NP + NP - 1], pl.ds(2, 2)], rec_ref, sem.at[1])
    rec_copy.start()

    def kv_copies(kh, phys, i):
        dst = pl.ds(pl.multiple_of(i * LANES, LANES), LANES)
        return (pltpu.make_async_copy(cache_ref.at[layer, phys, 2, kh], kcat_ref.at[kh, :, dst], sem.at[0]),
                pltpu.make_async_copy(cache_ref.at[layer, phys, 3, kh], vcat_ref.at[kh, :, dst], sem.at[0]))

    def scan(p, cnts):
        new = []
        for kh in range(ATT_KV_HEADS):
            cnt = cnts[kh]
            u = u_ref[(b * 2 + kh) * 2 * LANES + p]

            @pl.when(u != 0)
            def _(kh=kh, cnt=cnt, u=u):
                ck, cv = kv_copies(kh, pt_ref[b * NP + p], cnt)
                ck.start()
                cv.start()
                slot_u[kh * MAX_SEL_PAGES + cnt] = u
            new.append(cnt + jnp.where(u != 0, 1, 0))
        return tuple(new)
    counts = lax.fori_loop(0, NP - 1, scan, (0, 0))

    for kh in range(ATT_KV_HEADS):
        def wait_pair(i, carry, kh=kh):
            ck, cv = kv_copies(kh, 0, i)
            ck.wait()
            cv.wait()
            return carry
        lax.fori_loop(0, counts[kh], wait_pair, 0)
    rec_copy.wait()

    lane_c = lax.broadcasted_iota(I32, (1, ncat), 1)
    shift_c = t16 + 4 * ((lane_c % LANES) // SEL_BLOCK)
    lane = lax.broadcasted_iota(I32, (1, LANES), 1)
    shift = t16 + 4 * (lane // SEL_BLOCK)
    for kh in range(ATT_KV_HEADS):
        q16f = jnp.concatenate([q_ref[0, :, 128 * kh:128 * kh + 64], q_ref[0, :, 128 * kh + 64:128 * kh + 128]], axis=0)
        q16 = q16f.astype(BF16)
        c_far = jnp.where(g16 == 1, rb_ref[N_BUCKETS - 1, 2 * kh + 1], rb_ref[N_BUCKETS - 1, 2 * kh])

        u_vec = jnp.concatenate([jnp.full((1, LANES), slot_u[kh * MAX_SEL_PAGES + i], I32)
                                 for i in range(MAX_SEL_PAGES)], axis=1)
        ok_c = (jnp.bitwise_and(jnp.right_shift(u_vec, shift_c), 1) == 1) & valid_row
        s_c = jnp.where(ok_c, _dot(q16, kcat_ref[kh].astype(BF16)) * ATT_SCALE + c_far, NEG)
        u_rec = u_ref[(b * 2 + kh) * 2 * LANES + NP - 1]
        d_r = (P + t16) - ((NP - 1) * LANES + lane)
        ok_r = (jnp.bitwise_and(jnp.right_shift(u_rec, shift), 1) == 1) & valid_row
        s_r = _dot(q16, rec_ref[0, kh].astype(BF16)) * ATT_SCALE + _row_bias(_t5_bucket(d_r), rb_ref, kh, g16)
        s_r = jnp.where(ok_r, s_r, NEG)
        u_new = u_ref[(b * 2 + kh) * 2 * LANES + NP]
        sel_new = jnp.bitwise_and(jnp.right_shift(u_new, t16), 1) == 1
        m_fin = jnp.maximum(s_c.max(-1, keepdims=True), s_r.max(-1, keepdims=True))
        cols = []
        for c in range(t_valid):
            kn = kvn_ref[0, c:c + 1, 256 + 64 * kh:256 + 64 * kh + 64]
            sc = jnp.sum(q16f * kn, axis=-1, keepdims=True) * ATT_SCALE + _row_bias(_t5_bucket(t16 - c), rb_ref, kh, g16)
            ok = sel_new & (t16 >= c) & valid_row
            sc = jnp.where(ok, sc, NEG)
            cols.append((sc, ok))
            m_fin = jnp.maximum(m_fin, sc)
        e_c = jnp.where(ok_c, jnp.exp(s_c - m_fin), 0.0)
        e_r = jnp.where(ok_r, jnp.exp(s_r - m_fin), 0.0)
        l_fin = e_c.sum(-1, keepdims=True) + e_r.sum(-1, keepdims=True)
        acc = (_dot_nt(e_c.astype(BF16), vcat_ref[kh].astype(BF16))
               + _dot_nt(e_r.astype(BF16), rec_ref[1, kh].astype(BF16)))
        for c, (sc, ok) in enumerate(cols):
            e = jnp.where(ok, jnp.exp(sc - m_fin), 0.0)
            l_fin = l_fin + e
            acc = acc + e * kvn_ref[0, c:c + 1, 384 + 64 * kh:384 + 64 * kh + 64]
        o_sel = acc / jnp.maximum(l_fin, 1e-30)

        col_w = lax.broadcasted_iota(I32, (1, wb), 1)
        d_w = wb + t16 - col_w
        ok_w = (d_w < WINDOW) & valid_row
        s_w = _dot(q16, win_ref[0, 0, kh].astype(BF16)) * ATT_SCALE + _row_bias(_t5_bucket(d_w), rb_ref, kh, g16)
        s_w = jnp.where(ok_w, s_w, NEG)
        mw = s_w.max(-1, keepdims=True)
        cols = []
        for c in range(t_valid):
            kn = kvn_ref[0, c:c + 1, 512 + 64 * kh:512 + 64 * kh + 64]
            sc = jnp.sum(q16f * kn, axis=-1, keepdims=True) * ATT_SCALE + _row_bias(_t5_bucket(t16 - c), rb_ref, kh, g16)
            ok = (t16 >= c) & valid_row
            sc = jnp.where(ok, sc, NEG)
            cols.append((sc, ok))
            mw = jnp.maximum(mw, sc)
        e_w = jnp.where(ok_w, jnp.exp(s_w - mw), 0.0)
        l_w = e_w.sum(-1, keepdims=True)
        acc_w = _dot_nt(e_w.astype(BF16), win_ref[0, 1, kh].astype(BF16))
        for c, (sc, ok) in enumerate(cols):
            e = jnp.where(ok, jnp.exp(sc - mw), 0.0)
            l_w = l_w + e
            acc_w = acc_w + e * kvn_ref[0, c:c + 1, 640 + 64 * kh:640 + 64 * kh + 64]
        o_win = acc_w / jnp.maximum(l_w, 1e-30)

        for g in range(2):
            h = 2 * kh + g
            out = (gates[:, h:h + 1] * ocmp_ref[0, :, 64 * h:64 * h + 64]
                   + gates[:, 4 + h:5 + h] * o_sel[8 * g:8 * g + 8]
                   + gates[:, 8 + h:9 + h] * o_win[8 * g:8 * g + 8])
            y_ref[0, :, 64 * h:64 * h + 64] = out.astype(BF16)


def _nsa_s3(page_flat, u_flat, rel_bias, hs3, ocmp, cache_t, win_t, layer, P, NP, t_valid):
    B, T, _ = hs3.shape
    wb = win_t.shape[-1]
    grid_spec = pltpu.PrefetchScalarGridSpec(
        num_scalar_prefetch=2,
        grid=(B,),
        in_specs=[_smem_spec(),
                  pl.BlockSpec((1, T, 256), lambda b, *_: (b, 0, C_Q // 256)),
                  pl.BlockSpec((1, T, LANES), lambda b, *_: (b, 0, C_SM // LANES)),
                  pl.BlockSpec((1, T, 768), lambda b, *_: (b, 0, C_KV // 768)),
                  pl.BlockSpec((1, 8, 256), lambda b, *_: (b, 0, 0)),
                  pl.BlockSpec(memory_space=pl.ANY),
                  pl.BlockSpec((1, 2, 2, HEAD_DIM, wb), lambda b, *_: (b, 0, 0, 0, 0))],
        out_specs=pl.BlockSpec((1, T, 256), lambda b, *_: (b, 0, 0)),
        scratch_shapes=[pltpu.VMEM((2, HEAD_DIM, MAX_SEL_PAGES * LANES), F32),
                        pltpu.VMEM((2, HEAD_DIM, MAX_SEL_PAGES * LANES), F32),
                        pltpu.VMEM((2, 2, HEAD_DIM, LANES), F32),
                        pltpu.SMEM((2 * MAX_SEL_PAGES,), I32),
                        pltpu.SemaphoreType.DMA((2,))],
    )
    return pl.pallas_call(
        functools.partial(_nsa_s3_body, layer=layer, P=P, NP=NP, t_valid=t_valid),
        grid_spec=grid_spec,
        out_shape=jax.ShapeDtypeStruct((B, T, 256), BF16),
        compiler_params=_cparams("arbitrary"),
        name="nsa_sample_select_window",
    )(page_flat, u_flat, rel_bias, hs3, hs3, hs3, ocmp, cache_t, win_t)


def _per_layer_specs(depth, block, index_of_step):
    def spec(k):
        def index_map(l, i):
            idx = index_of_step(i)
            return tuple(jnp.where(l == k, v, 0) if n < 2 else v for n, v in enumerate(idx))
        return pl.BlockSpec(block, index_map)
    return [spec(k) for k in range(depth)]


def _kv_pages_body(*refs):
    o_ref = refs[-1]
    for k, h_ref in enumerate(refs[:-1]):
        @pl.when(pl.program_id(0) == k)
        def _(h_ref=h_ref):
            for j in range(o_ref.shape[1]):
                for c in range(4):
                    x = h_ref[0, LANES * j:LANES * (j + 1), LANES * c:LANES * (c + 1)]
                    o_ref[0, j, c] = x.T.reshape(ATT_KV_HEADS, HEAD_DIM, LANES)


def _kv_pages_t(hs):
    depth = len(hs)
    B, T, _ = hs[0].shape
    rows = min(4 * LANES, T)
    per_b = T // rows
    return pl.pallas_call(
        _kv_pages_body,
        grid=(depth, B * per_b),
        in_specs=_per_layer_specs(depth, (1, rows, 512), lambda i: (i // per_b, i % per_b, 0)),
        out_specs=pl.BlockSpec((1, rows // LANES, 4, ATT_KV_HEADS, HEAD_DIM, LANES), lambda l, i: (l, i, 0, 0, 0, 0)),
        out_shape=jax.ShapeDtypeStruct((depth, B * T // LANES, 4, ATT_KV_HEADS, HEAD_DIM, LANES), F32),
        compiler_params=_cparams("arbitrary", "arbitrary"),
        name="kv_pages_token_minor",
    )(*hs)


def _win_rows_body(*refs):
    o_ref = refs[-1]
    for k, h_ref in enumerate(refs[:-1]):
        @pl.when(pl.program_id(0) == k)
        def _(h_ref=h_ref):
            for c in range(2):
                x = h_ref[0, :, LANES * c:LANES * (c + 1)]
                o_ref[0, 0, c] = x.T.reshape(ATT_KV_HEADS, HEAD_DIM, x.shape[0])


def _win_rows_t(hs, wrows):
    depth = len(hs)
    B, T, _ = hs[0].shape
    last = T // wrows - 1
    return pl.pallas_call(
        _win_rows_body,
        grid=(depth, B),
        in_specs=_per_layer_specs(depth, (1, wrows, 256), lambda b: (b, last, (C_KV + 512) // 256)),
        out_specs=pl.BlockSpec((1, 1, 2, ATT_KV_HEADS, HEAD_DIM, wrows), lambda l, b: (l, b, 0, 0, 0, 0)),
        out_shape=jax.ShapeDtypeStruct((depth, B, 2, ATT_KV_HEADS, HEAD_DIM, wrows), F32),
        compiler_params=_cparams("arbitrary", "arbitrary"),
        name="win_rows_token_minor",
    )(*hs)


def _blockdiag2(w):
    z = jnp.zeros_like(w)
    return jnp.concatenate([jnp.concatenate([w, z], axis=-1), jnp.concatenate([z, w], axis=-1)], axis=-2)


def _prep_layer(w_in, pool_w, cmp_w1, cmp_w2, alog, dtb):
    glu, pool, q, kv, gate, gqkv, z, a, b = jnp.split(w_in, [512, 768, 1024, 1792, 1804, 2572, 2828, 2832], axis=1)
    pad = jnp.zeros((D_MODEL, IN_PAD - C_SM - 20), F32)
    w_in_p = jnp.concatenate([kv, gqkv, glu, pool, q, z, gate, a, b, pad], axis=1).astype(BF16)
    wblk = jnp.zeros((POOL_CH, POOL_CH), F32)
    for gi in range(4):
        wblk = wblk.at[64 * gi:64 * gi + 64, 64 * gi:64 * gi + 64].set(pool_w[gi])
    w1 = cmp_w1.reshape(2, CMP_BLOCK, HEAD_DIM, HEAD_DIM)
    wcat = jnp.concatenate([_blockdiag2(w1[:, :CMP_STRIDE]), _blockdiag2(w1[:, CMP_STRIDE:])], axis=-1).astype(BF16)
    wcat = wcat.reshape(2, CMP_STRIDE // 2, 2 * LANES, 2 * LANES)
    w2bd = _blockdiag2(cmp_w2).astype(BF16)
    lane_pad = lambda v: jnp.zeros((1, LANES), F32).at[0, SM_A:SM_A + GDN_HEADS].set(v)
    return w_in_p, wblk.astype(BF16), wcat, w2bd, lane_pad(alog), lane_pad(dtb)


def _cover_matrix(nsub, n_sel, cols):
    n = jnp.arange(nsub)[:, None] * CMP_STRIDE
    j = jnp.arange(cols)[None, :] * SEL_BLOCK
    cov = (n < j + SEL_BLOCK) & (n + CMP_BLOCK > j) & (jnp.arange(cols)[None, :] < n_sel) & (jnp.arange(nsub)[:, None] < nsub - 1)
    return cov.astype(BF16)


def kernel(x_prompt, x_sample, cache_nsa_kv, cache_win_kv, state_conv, state_pool, state_gdn_conv, state_gdn,
           page_table, w_in, conv_dw, conv_dw_b, conv_ln_g, conv_ln_b, conv_pw, pool_w, pool_scale,
           cmp_pe, cmp_w1, cmp_w2, gdn_conv_w, gdn_a_log, gdn_dt_bias, gdn_norm_g,
           w_out, ln1_g, ln1_b, w_up, w_down, ln2_g, ln2_b, rel_bias):
    depth = w_in.shape[0]
    BP, T, _ = x_prompt.shape
    BS, TS, _ = x_sample.shape
    NP = page_table.shape[1]
    page = cache_nsa_kv.shape[2]
    P = NP * page
    TSP = 8
    assert page == LANES and TS <= TSP and TS < CMP_STRIDE and T % QT == 0 and P % SEL_BLOCK == 0

    cache_t = jnp.transpose(cache_nsa_kv, (0, 1, 3, 4, 5, 2))
    win_t = jnp.transpose(cache_win_kv, (0, 1, 3, 4, 5, 2))
    page_flat = page_table.reshape(-1)

    tabw, tabc = _bias_tables(rel_bias, T)
    pterm = _peterm(cmp_pe.reshape(depth * 2, 1, CMP_BLOCK * HEAD_DIM), cmp_w1.reshape(depth * 2, CMP_BLOCK * HEAD_DIM, HEAD_DIM))
    pterm = pterm[:, 0:1, :].reshape(depth, 2, 1, HEAD_DIM)
    pterm = jnp.concatenate([pterm, pterm], axis=-1)

    ns_p, nsel_p = T // CMP_STRIDE, T // SEL_BLOCK
    covert_p = _cover_matrix(ns_p, nsel_p, nsel_p).T
    expneg_p = jnp.where(jnp.arange(nsel_p)[:, None] == (jnp.arange(T)[None, :] // SEL_BLOCK), NEG, 0.0).astype(BF16)
    ns_s = P // CMP_STRIDE
    nsel_s = P // SEL_BLOCK + 1
    nselp_s = -(-nsel_s // LANES) * LANES
    cover_s = _cover_matrix(ns_s + 1, nsel_s, nselp_s)[:ns_s]
    jj = jnp.arange(nselp_s)[:, None]
    pp = jnp.arange(2 * LANES)[None, :]
    pair = (((jj // 2 == pp) & (jj < 2 * NP)) | ((jj == 2 * NP) & (pp == NP))).astype(BF16)

    yp = x_prompt.reshape(BP * T, D_MODEL)
    ys = jnp.pad(x_sample, ((0, 0), (0, TSP - TS), (0, 0))).reshape(BS * TSP, D_MODEL)
    zeros = lambda *s: jnp.zeros(s, F32)
    outs_p = [[] for _ in range(6)]
    outs_s = [[] for _ in range(6)]
    hps = []
    preps = [_prep_layer(w_in[l], pool_w[l], cmp_w1[l], cmp_w2[l], gdn_a_log[l], gdn_dt_bias[l]) for l in range(depth)]
    hcs = [_nsa_s1(page_flat, cache_t, preps[l][2], l, BS, NP) for l in range(depth)]
    for l in range(depth):
        w_in_p, wblk, wcat, w2bd, alog_l, dtb_l = preps[l]
        lw = dict(w_out=w_out[l].astype(BF16), ln1_g=ln1_g[l][None], ln1_b=ln1_b[l][None],
                  w_up=w_up[l].astype(BF16), w_down=w_down[l].astype(BF16), ln2_g=ln2_g[l][None], ln2_b=ln2_b[l][None])
        conv_w = (conv_dw[l], conv_dw_b[l][None], conv_ln_g[l][None], conv_ln_b[l][None], conv_pw[l].astype(BF16))
        gdn_w = (gdn_conv_w[l], alog_l, dtb_l, gdn_norm_g[l][None])

        hp = _proj_in(yp, w_in_p, 512).reshape(BP, T, IN_PAD)
        m_conv, conv_new = _conv_mixer(hp, zeros(BP, CONV_WIDTH - 1, CONV_CH), *conv_w, t_valid=T)
        m_pool, pool_new = _pool_mixer(hp, zeros(BP, POOL_BUF, POOL_CH), wblk, pool_scale[l][None], t_valid=T, offset=0)
        m_att = _nsa_prompt(hp, rel_bias, wcat, w2bd, pterm[l], tabw, tabc, covert_p, expneg_p)
        m_gdn, gbuf_new, s_new = _gdn_mixer(hp, zeros(BP, GDN_CONV - 1, GDN_QKV), zeros(BP, GDN_HEADS, GDN_DK, GDN_DV),
                                            *gdn_w, t_valid=T)
        mixes = [m.reshape(BP * T, GROUP_WIDTH) for m in (m_conv, m_pool, m_att, m_gdn)]
        yp = _out_ffn(yp, mixes, lw, 512)
        hps.append(hp)
        for lst, arr in zip(outs_p[2:], (conv_new, pool_new, gbuf_new, s_new)):
            lst.append(arr)

        hs = _proj_in(ys, w_in_p, BS * TSP).reshape(BS, TSP, IN_PAD)
        m_conv, conv_new = _conv_mixer(hs, state_conv[l], *conv_w, t_valid=TS)
        m_pool, pool_new = _pool_mixer(hs, state_pool[l], wblk, pool_scale[l][None], t_valid=TS, offset=P)
        ocmp, u = _nsa_s2(hs, hcs[l], rel_bias, w2bd, pterm[l], cover_s, pair, P, TS)
        m_att = _nsa_s3(page_flat, u[:, :2, :].reshape(-1), rel_bias, hs, ocmp, cache_t, win_t[l], l, P, NP, TS)
        m_gdn, gbuf_new, s_new = _gdn_mixer(hs, state_gdn_conv[l], state_gdn[l], *gdn_w, t_valid=TS)
        mixes = [m.reshape(BS * TSP, GROUP_WIDTH) for m in (m_conv, m_pool, m_att, m_gdn)]
        ys = _out_ffn(ys, mixes, lw, BS * TSP)
        kv_new = hs[:, :TS, C_KV:C_KV + 768].reshape(BS, TS, 6, ATT_KV_HEADS, HEAD_DIM)
        outs_s[0].append(kv_new[:, :, :4])
        kw_all = jnp.concatenate([cache_win_kv[l], kv_new[:, :, 4:]], axis=1)
        outs_s[1].append(kw_all[:, -min(WINDOW, kw_all.shape[1]):])
        for lst, arr in zip(outs_s[2:], (conv_new, pool_new, gbuf_new, s_new)):
            lst.append(arr)

    p_nsa = jnp.transpose(_kv_pages_t(hps), (0, 1, 5, 2, 3, 4))
    p_win = jnp.transpose(_win_rows_t(hps, min(WINDOW, T)), (0, 1, 5, 2, 3, 4))
    p_rest = [p_win] + [jnp.stack(a) for a in outs_p[2:]]
    s_all = [jnp.stack(a) for a in outs_s]
    y_s = ys.reshape(BS, TSP, D_MODEL)[:, :TS]
    return (yp.reshape(BP, T, D_MODEL), y_s, p_nsa, *p_rest, *s_all)
```

```python
import functools
import math

import jax
import jax.numpy as jnp
from jax import lax
from jax.experimental import pallas as pl
from jax.experimental.pallas import tpu as pltpu

F32 = jnp.float32
BF16 = jnp.bfloat16
I32 = jnp.int32

D_MODEL = 1024
GROUP_WIDTH = 256
CONV_CH = 256
CONV_WIDTH = 31
POOL_CH = 256
POOL_WINDOWS = (2, 4, 8, 16)
POOL_BUF = 15
ATT_HEADS = 4
ATT_KV_HEADS = 2
HEAD_DIM = 64
ATT_SCALE = HEAD_DIM ** -0.5
CMP_STRIDE = 16
CMP_BLOCK = 32
SEL_BLOCK = 64
SEL_TOPN = 16
WINDOW = 512
N_BUCKETS = 32
GDN_HEADS = 4
GDN_DK = 64
GDN_DV = 64
GDN_QKV = 768
GDN_CONV = 4
GDN_CHUNK = 64
D_FF = 4096
DEPTH = 2
DN_ALPHA = (2 * DEPTH) ** 0.25
LN_EPS = 1e-5
NEG = -1e30
FORCE = 1e4

LANES = 128
VMEM_LIMIT_BYTES = 56 * 1024 * 1024

C_KV, C_GQKV, C_GLU, C_POOL, C_Q, C_Z, C_SM = 0, 768, 1536, 2048, 2304, 2560, 2816
IN_PAD = 2944
SM_GATE, SM_A, SM_B = 0, 12, 16


def _cparams(*sem):
    return pltpu.CompilerParams(dimension_semantics=sem, vmem_limit_bytes=VMEM_LIMIT_BYTES)


def _const_spec(shape):
    nd = len(shape)
    return pl.BlockSpec(shape, lambda *_: (0,) * nd, pipeline_mode=pl.Buffered(1))


def _smem_spec():
    return pl.BlockSpec(memory_space=pltpu.SMEM)


def _sigmoid(x):
    return jax.nn.sigmoid(x)


def _silu(x):
    return x * jax.nn.sigmoid(x)


def _layer_norm(y, g, b):
    mu = jnp.mean(y, axis=-1, keepdims=True)
    yc = y - mu
    var = jnp.mean(yc * yc, axis=-1, keepdims=True)
    return yc * lax.rsqrt(var + LN_EPS) * g + b


def _dot(a, b):
    return jnp.dot(a, b, preferred_element_type=F32)


def _dot_nt(a, b):
    return lax.dot_general(a, b, (((1,), (1,)), ((), ())), preferred_element_type=F32)


def _dot_tn(a, b):
    return lax.dot_general(a, b, (((0,), (0,)), ((), ())), preferred_element_type=F32)


def _split3(x):
    x1 = x.astype(BF16)
    r = x - x1.astype(F32)
    x2 = r.astype(BF16)
    x3 = (r - x2.astype(F32)).astype(BF16)
    return x1, x2, x3


def _dot_hl(a, b):
    ah = a.astype(BF16)
    al = (a - ah.astype(F32)).astype(BF16)
    bh = b.astype(BF16)
    bl = (b - bh.astype(F32)).astype(BF16)
    lhs = jnp.concatenate([ah, al, ah], axis=1)
    rhs = jnp.concatenate([bh, bh, bl], axis=0)
    return _dot(lhs, rhs)


def _t5_bucket(d):
    d = jnp.maximum(d, 0)
    logd = jnp.log(jnp.maximum(d, 1).astype(F32) / 16.0) / math.log(8.0)
    large = jnp.minimum(16 + (logd * 16.0).astype(I32), N_BUCKETS - 1)
    return jnp.where(d < 16, d, large)


def _bias_lookup(bk, value_of_bucket):
    out = jnp.zeros(bk.shape, F32)
    for k in range(N_BUCKETS):
        out = jnp.where(bk == k, value_of_bucket(k), out)
    return out


def _masked_softmax_parts(parts):
    ss = [jnp.where(ok, s, NEG) for s, ok in parts]
    mx = ss[0].max(-1, keepdims=True)
    for s in ss[1:]:
        mx = jnp.maximum(mx, s.max(-1, keepdims=True))
    es = [jnp.where(ok, jnp.exp(s - mx), 0.0) for s, (_, ok) in zip(ss, parts)]
    tot = es[0].sum(-1, keepdims=True)
    for e in es[1:]:
        tot = tot + e.sum(-1, keepdims=True)
    inv = 1.0 / jnp.maximum(tot, 1e-30)
    return [e * inv for e in es]


def _proj_in_body(x_ref, w_ref, o_ref):
    xb = x_ref[...].astype(BF16)
    for a in range(0, IN_PAD, 512):
        b = min(a + 512, IN_PAD)
        o_ref[:, a:b] = _dot(xb, w_ref[:, a:b])


def _proj_in(x, w, tm):
    n = x.shape[0]
    return pl.pallas_call(
        _proj_in_body,
        grid=(n // tm,),
        in_specs=[pl.BlockSpec((tm, D_MODEL), lambda i: (i, 0)), _const_spec((D_MODEL, IN_PAD))],
        out_specs=pl.BlockSpec((tm, IN_PAD), lambda i: (i, 0)),
        out_shape=jax.ShapeDtypeStruct((n, IN_PAD), F32),
        compiler_params=_cparams("arbitrary"),
        name="proj_in",
    )(x, w)


FF_CHUNK = 1024


def _out_ffn_body(x_ref, m0_ref, m1_ref, m2_ref, m3_ref, wo_ref, g1_ref, b1_ref, wu_ref, wd_ref, g2_ref, b2_ref, o_ref):
    acc = _dot(m0_ref[...], wo_ref[0:256, :])
    acc += _dot(m1_ref[...], wo_ref[256:512, :])
    acc += _dot(m2_ref[...], wo_ref[512:768, :])
    acc += _dot(m3_ref[...], wo_ref[768:1024, :])
    x1 = _layer_norm(DN_ALPHA * x_ref[...] + acc, g1_ref[...], b1_ref[...])
    xb = x1.astype(BF16)
    acc = jnp.zeros(x1.shape, F32)
    for c in range(0, D_FF, FF_CHUNK):
        h = _dot(xb, wu_ref[:, c:c + FF_CHUNK])
        a = jnp.square(jnp.maximum(h, 0.0)).astype(BF16)
        acc += _dot(a, wd_ref[c:c + FF_CHUNK, :])
    o_ref[...] = _layer_norm(DN_ALPHA * x1 + acc, g2_ref[...], b2_ref[...])


def _out_ffn(x, mixes, lw, tm):
    n = x.shape[0]
    row = lambda i: (i, 0)
    vec = _const_spec((1, D_MODEL))
    return pl.pallas_call(
        _out_ffn_body,
        grid=(n // tm,),
        in_specs=[pl.BlockSpec((tm, D_MODEL), row)] + [pl.BlockSpec((tm, GROUP_WIDTH), row)] * 4
        + [_const_spec((D_MODEL, D_MODEL)), vec, vec, _const_spec((D_MODEL, D_FF)), _const_spec((D_FF, D_MODEL)), vec, vec],
        out_specs=pl.BlockSpec((tm, D_MODEL), row),
        out_shape=jax.ShapeDtypeStruct((n, D_MODEL), F32),
        compiler_params=_cparams("arbitrary"),
        name="proj_out_ffn",
    )(x, *mixes, lw["w_out"], lw["ln1_g"], lw["ln1_b"], lw["w_up"], lw["w_down"], lw["ln2_g"], lw["ln2_b"])


CONV_PAD = 32


def _conv_body(h_ref, buf_ref, dw_ref, dwb_ref, g_ref, b_ref, pw_ref, y_ref, new_ref, full_ref, *, T, t_valid):
    hh = h_ref[0]
    full_ref[0:8, :] = jnp.zeros((8, CONV_CH), F32)
    full_ref[2:CONV_PAD, :] = buf_ref[0]
    full_ref[CONV_PAD:CONV_PAD + T, :] = hh[:, :CONV_CH] * _sigmoid(hh[:, CONV_CH:])
    new_ref[0] = full_ref[t_valid + 2:t_valid + CONV_PAD, :]
    rc = min(T, 128)

    def chunk(c, carry):
        base = pl.multiple_of(c * rc, rc)
        win = full_ref[pl.ds(base, rc + CONV_PAD), :]
        acc = jnp.zeros((rc, CONV_CH), F32) + dwb_ref[...]
        for r in range(8):
            shifted = win[r:r + (rc + CONV_PAD - r) // 8 * 8, :]
            for k in range(CONV_WIDTH):
                if (2 + k) % 8 == r:
                    a = (2 + k) // 8 * 8
                    acc = acc + dw_ref[k:k + 1, :] * shifted[a:a + rc, :]
        y = _silu(_layer_norm(acc, g_ref[...], b_ref[...]))
        y_ref[0, pl.ds(base, rc), :] = _dot(y.astype(BF16), pw_ref[...]).astype(BF16)
        return carry

    lax.fori_loop(0, T // rc, chunk, 0)


def _conv_mixer(h3, buf, dw, dwb, g, b, pw, t_valid):
    B, T, _ = h3.shape
    return pl.pallas_call(
        functools.partial(_conv_body, T=T, t_valid=t_valid),
        grid=(B,),
        in_specs=[pl.BlockSpec((1, T, 2 * CONV_CH), lambda i: (i, 0, C_GLU // (2 * CONV_CH))),
                  pl.BlockSpec((1, CONV_WIDTH - 1, CONV_CH), lambda i: (i, 0, 0)),
                  _const_spec((CONV_WIDTH, CONV_CH)), _const_spec((1, CONV_CH)), _const_spec((1, CONV_CH)),
                  _const_spec((1, CONV_CH)), _const_spec((CONV_CH, CONV_CH))],
        out_specs=[pl.BlockSpec((1, T, CONV_CH), lambda i: (i, 0, 0)),
                   pl.BlockSpec((1, CONV_WIDTH - 1, CONV_CH), lambda i: (i, 0, 0))],
        out_shape=[jax.ShapeDtypeStruct((B, T, CONV_CH), BF16),
                   jax.ShapeDtypeStruct((B, CONV_WIDTH - 1, CONV_CH), F32)],
        scratch_shapes=[pltpu.VMEM((T + CONV_PAD, CONV_CH), F32)],
        compiler_params=_cparams("arbitrary"),
        name="conv_mixer",
    )(h3, buf, dw, dwb, g, b, pw)


POOL_PAD = 16


def _pool_body(h_ref, buf_ref, w_ref, sc_ref, y_ref, new_ref, full_ref, *, T, t_valid, offset):
    full_ref[0:8, :] = jnp.zeros((8, POOL_CH), F32)
    full_ref[1:POOL_PAD, :] = buf_ref[0]
    full_ref[POOL_PAD:POOL_PAD + T, :] = h_ref[0]
    new_ref[0] = full_ref[t_valid + 1:t_valid + POOL_PAD, :]
    rc = min(T, 128)
    lane = lax.broadcasted_iota(I32, (1, POOL_CH), 1)
    group = lane // (POOL_CH // len(POOL_WINDOWS))
    wl = jnp.where(group == 0, 2, jnp.where(group == 1, 4, jnp.where(group == 2, 8, 16)))

    def chunk(c, carry):
        base = pl.multiple_of(c * rc, rc)
        win = full_ref[pl.ds(base, rc + POOL_PAD), :]
        x0 = win[POOL_PAD:POOL_PAD + rc, :]
        sums = {}
        acc = x0
        for i in range(1, 16):
            acc = acc + win[POOL_PAD - i:POOL_PAD - i + rc, :]
            if i + 1 in POOL_WINDOWS:
                sums[i + 1] = acc
        sel = jnp.where(group == 0, sums[2], jnp.where(group == 1, sums[4], jnp.where(group == 2, sums[8], sums[16])))
        pos = offset + base + lax.broadcasted_iota(I32, (rc, 1), 0)
        cnt = jnp.minimum(pos + 1, wl).astype(F32)
        d = sel / cnt - x0
        y_ref[0, pl.ds(base, rc), :] = (_dot(d.astype(BF16), w_ref[...]) * sc_ref[...]).astype(BF16)
        return carry

    lax.fori_loop(0, T // rc, chunk, 0)


def _pool_mixer(h3, buf, wblk, scale, t_valid, offset):
    B, T, _ = h3.shape
    return pl.pallas_call(
        functools.partial(_pool_body, T=T, t_valid=t_valid, offset=offset),
        grid=(B,),
        in_specs=[pl.BlockSpec((1, T, POOL_CH), lambda i: (i, 0, C_POOL // POOL_CH)),
                  pl.BlockSpec((1, POOL_BUF, POOL_CH), lambda i: (i, 0, 0)),
                  _const_spec((POOL_CH, POOL_CH)), _const_spec((1, POOL_CH))],
        out_specs=[pl.BlockSpec((1, T, POOL_CH), lambda i: (i, 0, 0)),
                   pl.BlockSpec((1, POOL_BUF, POOL_CH), lambda i: (i, 0, 0))],
        out_shape=[jax.ShapeDtypeStruct((B, T, POOL_CH), BF16),
                   jax.ShapeDtypeStruct((B, POOL_BUF, POOL_CH), F32)],
        scratch_shapes=[pltpu.VMEM((T + POOL_PAD, POOL_CH), F32)],
        compiler_params=_cparams("arbitrary"),
        name="pool_mixer",
    )(h3, buf, wblk, scale)


GDN_PAD = 8
CK = GDN_CHUNK


def _gdn_body(qkv_ref, z_ref, sm_ref, buf_ref, s0_ref, cw_ref, alog_ref, dtb_ref, ng_ref,
              y_ref, newbuf_ref, sout_ref, full_ref, c_ref, g_ref, bt_ref, gi_ref, bi_ref, u_ref, w_ref, a_ref,
              qg_ref, kdt_ref, s_ref, *, T, Tp, t_valid):
    full_ref[0:8, :] = jnp.zeros((8, GDN_QKV), F32)
    full_ref[5:GDN_PAD, :] = buf_ref[0]
    full_ref[GDN_PAD:GDN_PAD + T, :] = qkv_ref[0]
    newbuf_ref[0] = full_ref[t_valid + 5:t_valid + GDN_PAD, :]
    if Tp > t_valid:
        c_ref[...] = jnp.zeros((Tp, GDN_QKV), F32)
        g_ref[...] = jnp.zeros((Tp, LANES), F32)
        bt_ref[...] = jnp.zeros((Tp, LANES), F32)

    rc = min(t_valid, 128)

    def conv_chunk(c, carry):
        base = pl.multiple_of(c * rc, rc)
        win = full_ref[pl.ds(base, rc + GDN_PAD), :] if rc % 8 == 0 else full_ref[0:rc + GDN_PAD, :]
        acc = jnp.zeros((rc, GDN_QKV), F32)
        for k in range(GDN_CONV):
            acc = acc + cw_ref[k:k + 1, :] * win[5 + k:5 + k + rc, :]
        sm = sm_ref[0, pl.ds(base, rc), :] if rc % 8 == 0 else sm_ref[0, 0:rc, :]
        x = sm + dtb_ref[...]
        softplus = jnp.maximum(x, 0.0) + jnp.log1p(jnp.exp(-jnp.abs(x)))
        gv = -jnp.exp(alog_ref[...]) * softplus
        bv = _sigmoid(sm)
        if rc % 8 == 0:
            c_ref[pl.ds(base, rc), :] = _silu(acc)
            g_ref[pl.ds(base, rc), :] = gv
            bt_ref[pl.ds(base, rc), :] = bv
        else:
            c_ref[0:rc, :] = _silu(acc)
            g_ref[0:rc, :] = gv
            bt_ref[0:rc, :] = bv
        return carry

    lax.fori_loop(0, t_valid // rc, conv_chunk, 0)

    HW = GDN_HEADS * GDN_DK
    lane = lax.broadcasted_iota(I32, (1, HW), 1)
    hmask = [jnp.where(lane // GDN_DK == h, 1.0, 0.0).astype(BF16) for h in range(GDN_HEADS)]
    row = lax.broadcasted_iota(I32, (CK, 1), 0)
    jl = lane % CK
    incl = row >= jl
    strict = row > jl
    eye_all = jnp.where(row == jl, 1.0, 0.0)
    er = lax.broadcasted_iota(I32, (LANES, HW), 0)
    ec = lax.broadcasted_iota(I32, (LANES, HW), 1) // GDN_DK
    exp_a = jnp.where(er == ec + SM_A, 1.0, 0.0).astype(BF16)
    exp_b = jnp.where(er == ec + SM_B, 1.0, 0.0).astype(BF16)
    br = lax.broadcasted_iota(I32, (HW, HW), 0) // GDN_DK
    bc = lax.broadcasted_iota(I32, (HW, HW), 1) // GDN_DK
    same_head = br == bc
    bones = jnp.where(same_head, 1.0, 0.0).astype(BF16)

    def blockdiag(x):
        return jnp.concatenate([x * m for m in hmask], axis=0)

    def expand3(x, e):
        x1, x2, x3 = _split3(x)
        return _dot(jnp.concatenate([x1, x2, x3], axis=1), jnp.concatenate([e, e, e], axis=0))

    def bd_dot_hl(a, b):
        ah = a.astype(BF16)
        al = (a - ah.astype(F32)).astype(BF16)
        bh = b.astype(BF16)
        bl = (b - bh.astype(F32)).astype(BF16)
        bdh = blockdiag(bh)
        return _dot(jnp.concatenate([ah, al, ah], axis=1), jnp.concatenate([bdh, bdh, blockdiag(bl)], axis=0))

    rb = min(Tp, 4 * CK)
    row_in_chunk = lax.broadcasted_iota(I32, (rb, 1), 0) % CK

    def prep(c, carry):
        r0 = pl.multiple_of(c * rb, rb)
        g = g_ref[pl.ds(r0, rb), :]
        for s in (1, 2, 4, 8, 16, 32):
            g = g + jnp.where(row_in_chunk >= s, jnp.roll(g, s, axis=0), 0.0)
        gi_ref[pl.ds(r0, rb), :] = expand3(g, exp_a)
        bi_ref[pl.ds(r0, rb), :] = expand3(bt_ref[pl.ds(r0, rb), :], exp_b)
        for part, scale in ((0, GDN_DK ** -0.5), (1, 1.0)):
            x = c_ref[pl.ds(r0, rb), HW * part:HW * (part + 1)]
            ssq = expand3(x * x, bones)
            c_ref[pl.ds(r0, rb), HW * part:HW * (part + 1)] = x * lax.rsqrt(ssq + 1e-6) * scale
        return carry

    lax.fori_loop(0, Tp // rb, prep, 0)

    n_chunks = Tp // CK
    group = 4 if n_chunks % 4 == 0 else 1

    def solve(it, carry):
        r0s = [pl.multiple_of((it * group + k) * CK, CK) for k in range(group)]
        gi = [gi_ref[pl.ds(r0, CK), :] for r0 in r0s]
        kn = [c_ref[pl.ds(r0, CK), HW:2 * HW] for r0 in r0s]
        decay = [jnp.exp(jnp.where(incl, g - jnp.sum(eye_all * g, axis=0, keepdims=True), NEG)) for g in gi]
        kb = [k * bi_ref[pl.ds(r0, CK), :] for k, r0 in zip(kn, r0s)]
        kst = [blockdiag(k.astype(BF16)) for k in kn]
        pw = [-jnp.where(strict, _dot_nt(b.astype(BF16), s) * d, 0.0) for b, s, d in zip(kb, kst, decay)]
        tinv = [eye_all + p for p in pw]
        pw = [bd_dot_hl(p, p) for p in pw]
        for _ in range(4):
            both = [bd_dot_hl(jnp.concatenate([t, p], axis=0), p) for t, p in zip(tinv, pw)]
            tinv = [t + b[:CK] for t, b in zip(tinv, both)]
            pw = [b[CK:] for b in both]
        tinv = [t + bd_dot_hl(t, p) for t, p in zip(tinv, pw)]
        for k, r0 in enumerate(r0s):
            vb = c_ref[pl.ds(r0, CK), 2 * HW:3 * HW] * bi_ref[pl.ds(r0, CK), :]
            u_ref[pl.ds(r0, CK), :] = bd_dot_hl(tinv[k], vb)
        for k, r0 in enumerate(r0s):
            w_ref[pl.ds(r0, CK), :] = bd_dot_hl(tinv[k], kb[k] * jnp.exp(gi[k]))
        for k, r0 in enumerate(r0s):
            qn = c_ref[pl.ds(r0, CK), 0:HW]
            a_ref[pl.ds(r0, CK), :] = (_dot_nt(qn.astype(BF16), kst[k]) * decay[k]).astype(BF16)
            qg_ref[pl.ds(r0, CK), :] = (qn * jnp.exp(gi[k])).astype(BF16)
            kdt_ref[it * group + k] = (kn[k] * jnp.exp(gi[k][CK - 1:CK, :] - gi[k])).T.astype(BF16)
        return carry

    lax.fori_loop(0, n_chunks // group, solve, 0)

    s_ref[...] = jnp.zeros((HW, HW), F32)
    for h in range(GDN_HEADS):
        s_ref[GDN_DK * h:GDN_DK * (h + 1), GDN_DV * h:GDN_DV * (h + 1)] = s0_ref[0, h]
    rows_out = min(CK, T)
    ng_all = jnp.concatenate([ng_ref[...]] * GDN_HEADS, axis=1)

    def recur(c, carry):
        r0 = pl.multiple_of(c * CK, CK)
        s_all = s_ref[...]
        s_b = s_all.astype(BF16)
        vnew = u_ref[pl.ds(r0, CK), :] - _dot(w_ref[pl.ds(r0, CK), :].astype(BF16), s_b)
        vnb = vnew.astype(BF16)
        glast = gi_ref[pl.ds(r0 + CK - 1, 1), :]
        s_ref[...] = s_all * jnp.exp(glast) + jnp.where(same_head, _dot(kdt_ref[c], vnb), 0.0)
        u_ref[pl.ds(r0, CK), :] = _dot(qg_ref[pl.ds(r0, CK), :], s_b) + _dot(a_ref[pl.ds(r0, CK), :], blockdiag(vnb))
        return carry

    lax.fori_loop(0, Tp // CK, recur, 0)
    for h in range(GDN_HEADS):
        sout_ref[0, h] = s_ref[GDN_DK * h:GDN_DK * (h + 1), GDN_DV * h:GDN_DV * (h + 1)]

    def finish(c, carry):
        r0 = pl.multiple_of(c * rb, rb)
        o = u_ref[pl.ds(r0, rb), :]
        on = o * lax.rsqrt(expand3(o * o, bones) * (1.0 / GDN_DV) + LN_EPS) * ng_all
        if T >= CK:
            y_ref[0, pl.ds(r0, rb), :] = (on * _silu(z_ref[0, pl.ds(r0, rb), :])).astype(BF16)
        else:
            y_ref[0] = (on[0:rows_out] * _silu(z_ref[0])).astype(BF16)
        return carry

    lax.fori_loop(0, Tp // rb, finish, 0)


def _gdn_mixer(h3, buf, s0, cw, alog_l, dtb_l, ng, t_valid):
    B, T, _ = h3.shape
    Tp = -(-T // CK) * CK
    return pl.pallas_call(
        functools.partial(_gdn_body, T=T, Tp=Tp, t_valid=t_valid),
        grid=(B,),
        in_specs=[pl.BlockSpec((1, T, GDN_QKV), lambda i: (i, 0, C_GQKV // GDN_QKV)),
                  pl.BlockSpec((1, T, 256), lambda i: (i, 0, C_Z // 256)),
                  pl.BlockSpec((1, T, LANES), lambda i: (i, 0, C_SM // LANES)),
                  pl.BlockSpec((1, GDN_CONV - 1, GDN_QKV), lambda i: (i, 0, 0)),
                  pl.BlockSpec((1, GDN_HEADS, GDN_DK, GDN_DV), lambda i: (i, 0, 0, 0)),
                  _const_spec((GDN_CONV, GDN_QKV)), _const_spec((1, LANES)), _const_spec((1, LANES)),
                  _const_spec((1, GDN_DV))],
        out_specs=[pl.BlockSpec((1, T, 256), lambda i: (i, 0, 0)),
                   pl.BlockSpec((1, GDN_CONV - 1, GDN_QKV), lambda i: (i, 0, 0)),
                   pl.BlockSpec((1, GDN_HEADS, GDN_DK, GDN_DV), lambda i: (i, 0, 0, 0))],
        out_shape=[jax.ShapeDtypeStruct((B, T, 256), BF16),
                   jax.ShapeDtypeStruct((B, GDN_CONV - 1, GDN_QKV), F32),
                   jax.ShapeDtypeStruct((B, GDN_HEADS, GDN_DK, GDN_DV), F32)],
        scratch_shapes=[pltpu.VMEM((T + GDN_PAD, GDN_QKV), F32), pltpu.VMEM((Tp, GDN_QKV), F32),
                        pltpu.VMEM((Tp, LANES), F32), pltpu.VMEM((Tp, LANES), F32),
                        pltpu.VMEM((Tp, 256), F32), pltpu.VMEM((Tp, 256), F32),
                        pltpu.VMEM((Tp, 256), F32), pltpu.VMEM((Tp, 256), F32), pltpu.VMEM((Tp, 256), BF16),
                        pltpu.VMEM((Tp, 256), BF16), pltpu.VMEM((Tp // CK, GDN_HEADS * GDN_DK, CK), BF16),
                        pltpu.VMEM((GDN_HEADS * GDN_DK, GDN_HEADS * GDN_DV), F32)],
        compiler_params=_cparams("arbitrary"),
        name="gdn_mixer",
    )(h3, h3, h3, buf, s0, cw, alog_l, dtb_l, ng)


def _peterm_body(pe_ref, w1_ref, o_ref):
    pe = jnp.broadcast_to(pe_ref[0], (8, CMP_BLOCK * HEAD_DIM)).astype(BF16)
    o_ref[0] = _dot(pe, w1_ref[0].astype(BF16))


def _peterm(pe_flat, w1):
    n = pe_flat.shape[0]
    return pl.pallas_call(
        _peterm_body,
        grid=(n,),
        in_specs=[pl.BlockSpec((1, 1, CMP_BLOCK * HEAD_DIM), lambda i: (i, 0, 0)),
                  pl.BlockSpec((1, CMP_BLOCK * HEAD_DIM, HEAD_DIM), lambda i: (i, 0, 0))],
        out_specs=pl.BlockSpec((1, 8, HEAD_DIM), lambda i: (i, 0, 0)),
        out_shape=jax.ShapeDtypeStruct((n, 8, HEAD_DIM), F32),
        compiler_params=_cparams("arbitrary"),
        name="cmp_pe_term",
    )(pe_flat, w1)


QT = 128
WBAND = WINDOW + QT


def _tabw_body(rb_ref, o_ref):
    i = lax.broadcasted_iota(I32, (QT, WBAND), 0)
    j = lax.broadcasted_iota(I32, (QT, WBAND), 1)
    bk = _t5_bucket(WINDOW + i - j)
    for h in range(ATT_HEADS):
        o_ref[h] = _bias_lookup(bk, lambda k: rb_ref[k, h])


def _tabc_body(rb_ref, o_ref):
    p0 = pl.program_id(0) * QT
    ns = o_ref.shape[-1]
    t = p0 + lax.broadcasted_iota(I32, (QT, ns), 0)
    n = lax.broadcasted_iota(I32, (QT, ns), 1)
    bk = _t5_bucket(t - (n * CMP_STRIDE + CMP_BLOCK - 1))
    for h in range(ATT_HEADS):
        o_ref[h] = _bias_lookup(bk, lambda k: rb_ref[k, h])


def _bias_tables(rel_bias, T):
    ns = T // CMP_STRIDE
    tabw = pl.pallas_call(
        _tabw_body, in_specs=[_smem_spec()],
        out_shape=jax.ShapeDtypeStruct((ATT_HEADS, QT, WBAND), F32), name="bias_window_table")(rel_bias)
    tabc = pl.pallas_call(
        _tabc_body, grid=(T // QT,), in_specs=[_smem_spec()],
        out_specs=pl.BlockSpec((ATT_HEADS, QT, ns), lambda i: (0, i, 0)),
        out_shape=jax.ShapeDtypeStruct((ATT_HEADS, T, ns), F32),
        compiler_params=_cparams("arbitrary"), name="bias_cmp_table")(rel_bias)
    return tabw, tabc


def _compress_pre(load_rows, wcat_ref, c):
    acc = None
    for r in range(0, CMP_STRIDE, 2):
        lhs = jnp.concatenate([load_rows(r), load_rows(r + 1)], axis=1).astype(BF16)
        part = _dot(lhs, wcat_ref[c, r // 2])
        acc = part if acc is None else acc + part
    return acc


def _compress_finish(hcat, pt, w2):
    pre = hcat[:, :LANES] + jnp.roll(hcat[:, LANES:], -1, axis=0) + pt
    return _dot(_silu(pre).astype(BF16), w2)


def _topn_select(score, n_cols):
    j = lax.broadcasted_iota(I32, score.shape, 1)
    rank = jnp.zeros(score.shape, F32)
    for jp in range(n_cols):
        col = score[:, jp:jp + 1]
        ahead = (col > score) | ((col == score) & (jp < j))
        rank = rank + jnp.where(ahead, 1.0, 0.0)
    return jnp.where((rank < SEL_TOPN) & (score > 0.5 * NEG), 1.0, 0.0)


def _topn_select_rows(score_t, n_rows):
    j = lax.broadcasted_iota(I32, score_t.shape, 0)
    rank = jnp.zeros(score_t.shape, F32)
    for jp in range(n_rows):
        r = score_t[jp:jp + 1, :]
        ahead = (r > score_t) | ((r == score_t) & (jp < j))
        rank = rank + jnp.where(ahead, 1.0, 0.0)
    return jnp.where((rank < SEL_TOPN) & (score_t > 0.5 * NEG), 1.0, 0.0)


def _nsa_prompt_body(rb_ref, q_ref, sm_ref, kv_ref, wcat_ref, w2_ref, pt_ref, tabw_ref, tabc_ref, covert_ref,
                     y_ref, kvp_ref, kc_ref, vc_ref, cmp_ref, nsel_ref, *, T, FT):
    ns = T // CMP_STRIDE
    n_sel = T // SEL_BLOCK
    qt = pl.program_id(1)
    p0 = pl.multiple_of(qt * QT, QT)

    @pl.when(qt == 0)
    def _():
        kvp_ref[0:WINDOW, :] = jnp.zeros((WINDOW, 512), BF16)
        kvp_ref[WINDOW:WINDOW + T, :] = kv_ref[0, :, 256:768].astype(BF16)
        for c, dst in ((0, kc_ref), (1, vc_ref)):
            cmp_ref[c] = kv_ref[0, :, LANES * c:LANES * (c + 1)]
            hcat = _compress_pre(lambda r: cmp_ref[c, pl.ds(r, ns, stride=CMP_STRIDE), :], wcat_ref, c)
            dst[...] = _compress_finish(hcat, pt_ref[c], w2_ref[c]).astype(BF16)

    gates = _sigmoid(sm_ref[0])
    t = p0 + lax.broadcasted_iota(I32, (QT, 1), 0)
    n_i = lax.broadcasted_iota(I32, (1, ns), 1)
    ok_cmp = (t - (n_i * CMP_STRIDE + CMP_BLOCK - 1) >= 0) & (n_i < ns - 1)
    ok_cmp2 = jnp.concatenate([ok_cmp, ok_cmp], axis=0)
    t_l = p0 + lax.broadcasted_iota(I32, (1, QT), 1)
    j_s = lax.broadcasted_iota(I32, (n_sel, 1), 0)
    cur = t_l // SEL_BLOCK
    forced = (j_s == 0) | (j_s == cur) | (j_s == cur - 1)
    avail = j_s * SEL_BLOCK <= t_l
    m_near = p0 - QT + lax.broadcasted_iota(I32, (1, 2 * QT), 1)
    near_blk = (p0 - QT + lax.broadcasted_iota(I32, (n_sel, 2 * QT), 1)) // SEL_BLOCK
    e_near_neg = jnp.where(near_blk == j_s, NEG, 0.0).astype(BF16)
    causal_near = jnp.where((m_near >= 0) & (m_near <= t), 0.0, NEG)
    m_win = p0 - WINDOW + lax.broadcasted_iota(I32, (1, WBAND), 1)
    d_win = t - m_win
    add_win = jnp.where((m_win >= 0) & (d_win >= 0) & (d_win < WINDOW), 0.0, NEG)
    two = lambda x: jnp.concatenate([x, x], axis=0)

    KH = range(ATT_KV_HEADS)
    kcol = lambda kh, base: slice(base + 64 * kh, base + 64 * kh + 64)
    q2 = [(jnp.concatenate([q_ref[0, :, kcol(2 * kh, 0)], q_ref[0, :, kcol(2 * kh + 1, 0)]], axis=0)
           * ATT_SCALE).astype(BF16) for kh in KH]
    s_c = [_dot_nt(q2[kh], kc_ref[:, kcol(kh, 0)]) + jnp.concatenate([tabc_ref[2 * kh], tabc_ref[2 * kh + 1]], axis=0)
           for kh in KH]
    p_c = [_masked_softmax_parts([(s, ok_cmp2)])[0] for s in s_c]
    o_cmp = [_dot(p_c[kh].astype(BF16), vc_ref[:, kcol(kh, 0)]) for kh in KH]
    nsel_t = []
    for kh in KH:
        p3 = _split3(p_c[kh][0:QT] + p_c[kh][QT:2 * QT])
        imp_t = (_dot_nt(covert_ref[...], p3[0]) + _dot_nt(covert_ref[...], p3[1])
                 + _dot_nt(covert_ref[...], p3[2]))
        score_t = jnp.where(avail, imp_t + jnp.where(forced, FORCE, 0.0), NEG)
        nsel = 1.0 - _topn_select_rows(score_t, n_sel)
        nsel_ref[kh] = nsel
        nsel_t.append(nsel.astype(BF16))
    add_near = [_dot_tn(nsel_t[kh], e_near_neg) + causal_near for kh in KH]
    nb = FT // SEL_BLOCK
    e_tile = jnp.where(lax.broadcasted_iota(I32, (nb, FT), 1) // SEL_BLOCK == lax.broadcasted_iota(I32, (nb, FT), 0),
                       NEG, 0.0)
    near_rows = pl.ds(WINDOW + p0 - QT, 2 * QT)
    bias_near = [jnp.concatenate(
        [tabw_ref[2 * kh + g, :, WINDOW - QT:WINDOW + QT] - rb_ref[N_BUCKETS - 1, 2 * kh + g] + add_near[kh]
         for g in range(2)], axis=0) for kh in KH]
    s_n = [_dot_nt(q2[kh], kvp_ref[near_rows, kcol(kh, 0)]) + bias_near[kh] for kh in KH]
    m0 = [s.max(-1, keepdims=True) for s in s_n]
    e_n = [jnp.exp(s - m) for s, m in zip(s_n, m0)]
    init = tuple((m0[kh], e_n[kh].sum(-1, keepdims=True), _dot(e_n[kh].astype(BF16), kvp_ref[near_rows, kcol(kh, 128)]))
                 for kh in KH)

    def far(i, carry):
        k0 = pl.multiple_of(i * FT, FT)
        rows = pl.ds(WINDOW + k0, FT)
        lim = jnp.where(k0 + lax.broadcasted_iota(I32, (1, FT), 1) < p0 - QT, 0.0, NEG)
        blocks = pl.ds(pl.multiple_of(i * nb, nb), nb)
        s = [_dot_nt(q2[kh], kvp_ref[rows, kcol(kh, 0)]) + two(_dot_tn(nsel_ref[kh, blocks, :], e_tile) + lim)
             for kh in KH]
        m_new = [jnp.maximum(carry[kh][0], s[kh].max(-1, keepdims=True)) for kh in KH]
        alpha = [jnp.exp(carry[kh][0] - m_new[kh]) for kh in KH]
        e = [jnp.exp(s[kh] - m_new[kh]) for kh in KH]
        return tuple((m_new[kh], alpha[kh] * carry[kh][1] + e[kh].sum(-1, keepdims=True),
                      alpha[kh] * carry[kh][2] + _dot(e[kh].astype(BF16), kvp_ref[rows, kcol(kh, 128)])) for kh in KH)

    n_far = (jnp.maximum(p0 - QT, 0) + FT - 1) // FT
    fin = lax.fori_loop(0, n_far, far, init)
    o_sel = [fin[kh][2] / jnp.maximum(fin[kh][1], 1e-30) for kh in KH]
    win_rows = pl.ds(p0, WBAND)
    s_w = [_dot_nt(q2[kh], kvp_ref[win_rows, kcol(kh, 256)])
           + jnp.concatenate([tabw_ref[2 * kh] + add_win, tabw_ref[2 * kh + 1] + add_win], axis=0) for kh in KH]
    e_w = [jnp.exp(s - s.max(-1, keepdims=True)) for s in s_w]
    o_win = [_dot(e_w[kh].astype(BF16), kvp_ref[win_rows, kcol(kh, 384)])
             / jnp.maximum(e_w[kh].sum(-1, keepdims=True), 1e-30) for kh in KH]
    for h in range(ATT_HEADS):
        kh, g = divmod(h, 2)
        rows = slice(QT * g, QT * (g + 1))
        out = (gates[:, h:h + 1] * o_cmp[kh][rows] + gates[:, 4 + h:5 + h] * o_sel[kh][rows]
               + gates[:, 8 + h:9 + h] * o_win[kh][rows])
        y_ref[0, :, 64 * h:64 * h + 64] = out.astype(BF16)


def _nsa_prompt(h3, rel_bias, wcat, w2bd, pt, tabw, tabc, covert):
    B, T, _ = h3.shape
    ns = T // CMP_STRIDE
    n_sel = T // SEL_BLOCK
    ft = min(512, T)
    return pl.pallas_call(
        functools.partial(_nsa_prompt_body, T=T, FT=ft),
        grid=(B, T // QT),
        in_specs=[_smem_spec(),
                  pl.BlockSpec((1, QT, 256), lambda b, i: (b, i, C_Q // 256)),
                  pl.BlockSpec((1, QT, LANES), lambda b, i: (b, i, C_SM // LANES)),
                  pl.BlockSpec((1, T, 768), lambda b, i: (b, 0, C_KV // 768)),
                  _const_spec((2, CMP_STRIDE // 2, 2 * LANES, 2 * LANES)), _const_spec((2, LANES, LANES)),
                  _const_spec((2, 1, LANES)), _const_spec((ATT_HEADS, QT, WBAND)),
                  pl.BlockSpec((ATT_HEADS, QT, ns), lambda b, i: (0, i, 0)),
                  _const_spec((n_sel, ns))],
        out_specs=pl.BlockSpec((1, QT, 256), lambda b, i: (b, i, 0)),
        out_shape=jax.ShapeDtypeStruct((B, T, 256), BF16),
        scratch_shapes=[pltpu.VMEM((WINDOW + T, 512), BF16), pltpu.VMEM((ns, LANES), BF16),
                        pltpu.VMEM((ns, LANES), BF16), pltpu.VMEM((2, T, LANES), F32),
                        pltpu.VMEM((ATT_KV_HEADS, n_sel, QT), F32)],
        compiler_params=_cparams("arbitrary", "arbitrary"),
        name="nsa_prompt",
    )(rel_bias, h3, h3, h3, wcat, w2bd, pt, tabw, tabc, covert)


def _nsa_s1_body(pt_ref, cache_ref, wcat_ref, o_ref, buf_ref, row_ref, sem, *, layer, CH):
    s = pl.program_id(0)
    nsteps = pl.num_programs(0)
    slot = s % 2

    def page_copy(step, p, sl):
        phys = pt_ref[step * CH + p]
        return pltpu.make_async_copy(cache_ref.at[layer, phys, pl.ds(0, 2)], buf_ref.at[sl, p], sem.at[sl])

    def issue(step, sl):
        def one(p, carry):
            page_copy(step, p, sl).start()
            return carry
        lax.fori_loop(0, CH, one, 0)

    @pl.when(s == 0)
    def _():
        issue(0, 0)

    @pl.when(s + 1 < nsteps)
    def _():
        issue(s + 1, 1 - slot)

    def wait_one(p, carry):
        page_copy(s, p, slot).wait()
        return carry
    lax.fori_loop(0, CH, wait_one, 0)

    gp = 16 if CH % 16 == 0 else CH
    gsub = gp * LANES // CMP_STRIDE
    for g in range(CH // gp):
        for p in range(g * gp, (g + 1) * gp):
            for c in range(2):
                row_ref[c, LANES * p:LANES * (p + 1), :] = buf_ref[slot, p, c].reshape(2 * HEAD_DIM, LANES).T
        for c in range(2):
            hcat = _compress_pre(lambda r: row_ref[c, pl.ds(g * gp * LANES + r, gsub, stride=CMP_STRIDE), :], wcat_ref, c)
            o_ref[0, g * gsub:(g + 1) * gsub, 2 * LANES * c:2 * LANES * (c + 1)] = hcat


def _nsa_s1(page_flat, cache_t, wcat, layer, B, NP):
    CH = min(64, NP)
    nsub = CH * LANES // CMP_STRIDE
    per_b = NP // CH
    grid_spec = pltpu.PrefetchScalarGridSpec(
        num_scalar_prefetch=1,
        grid=(B * per_b,),
        in_specs=[pl.BlockSpec(memory_space=pl.ANY),
                  pl.BlockSpec((2, CMP_STRIDE // 2, 2 * LANES, 2 * LANES), lambda s, pt: (0, 0, 0, 0),
                               pipeline_mode=pl.Buffered(1))],
        out_specs=pl.BlockSpec((1, nsub, 4 * LANES), lambda s, pt: (s // per_b, s % per_b, 0)),
        scratch_shapes=[pltpu.VMEM((2, CH, 2, 2, HEAD_DIM, LANES), F32), pltpu.VMEM((2, CH * LANES, LANES), F32),
                        pltpu.SemaphoreType.DMA((2,))],
    )
    return pl.pallas_call(
        functools.partial(_nsa_s1_body, layer=layer, CH=CH),
        grid_spec=grid_spec,
        out_shape=jax.ShapeDtypeStruct((B, NP * LANES // CMP_STRIDE, 4 * LANES), F32),
        compiler_params=_cparams("arbitrary"),
        name="nsa_sample_compress",
    )(page_flat, cache_t, wcat)


def _row_bias(bk, rb_ref, kh, g_of_row):
    return _bias_lookup(bk, lambda k: jnp.where(g_of_row == 1, rb_ref[k, 2 * kh + 1], rb_ref[k, 2 * kh]))


MAX_SEL_PAGES = 64
SLOT_INFO = 5


def _nsa_s2_body(rb_ref, hc_ref, q_ref, w2_ref, pt_ref, cover_ref, pair_ref, ut_ref, ocmp_ref, slots_ref,
                 *, P, t_valid, NSELP):
    nsub = hc_ref.shape[1]
    kc = _compress_finish(hc_ref[0, :, 0:2 * LANES], pt_ref[0], w2_ref[0]).astype(BF16)
    vc = _compress_finish(hc_ref[0, :, 2 * LANES:4 * LANES], pt_ref[1], w2_ref[1]).astype(BF16)
    n_sel = P // SEL_BLOCK + 1
    row = lax.broadcasted_iota(I32, (16, 1), 0)
    t16 = row % 8
    g16 = row // 8
    n_i = lax.broadcasted_iota(I32, (1, nsub), 1)
    d_cmp = (P + t16) - (n_i * CMP_STRIDE + CMP_BLOCK - 1)
    ok_cmp = d_cmp >= 0
    bk_cmp = _t5_bucket(d_cmp)
    t8 = lax.broadcasted_iota(I32, (8, 1), 0)
    j_i = lax.broadcasted_iota(I32, (1, NSELP), 1)
    qpos = P + t8
    cur = qpos // SEL_BLOCK
    forced = (j_i == 0) | (j_i == cur) | (j_i == cur - 1)
    avail = (j_i * SEL_BLOCK <= qpos) & (j_i < n_sel)
    wts = jnp.where(t8 < t_valid, jnp.left_shift(1, t8 + 4 * (j_i % 2)), 0).astype(F32)
    NP = P // LANES
    page_l = lax.broadcasted_iota(I32, (1, 2 * LANES), 1)
    slot_s = lax.broadcasted_iota(I32, (MAX_SEL_PAGES, 1), 0).astype(F32)
    col = lax.broadcasted_iota(I32, (1, LANES), 1)
    pick = lambda x, lane_no: jnp.sum(jnp.where(page_l == lane_no, x, 0.0), axis=1, keepdims=True)
    for kh in range(ATT_KV_HEADS):
        q16 = jnp.concatenate([q_ref[0, :, 128 * kh:128 * kh + 64], q_ref[0, :, 128 * kh + 64:128 * kh + 128]],
                              axis=0).astype(BF16)
        s = _dot_nt(q16, kc[:, 64 * kh:64 * kh + 64]) * ATT_SCALE + _row_bias(bk_cmp, rb_ref, kh, g16)
        (p,) = _masked_softmax_parts([(s, ok_cmp)])
        o = _dot(p.astype(BF16), vc[:, 64 * kh:64 * kh + 64])
        ocmp_ref[0, :, 128 * kh:128 * kh + 64] = o[0:8]
        ocmp_ref[0, :, 128 * kh + 64:128 * kh + 128] = o[8:16]
        p3 = _split3(p[0:8] + p[8:16])
        imp = _dot(p3[0], cover_ref[...]) + _dot(p3[1], cover_ref[...]) + _dot(p3[2], cover_ref[...])
        score = jnp.where(avail, imp + jnp.where(forced, FORCE, 0.0), NEG)
        sel = _topn_select(score, n_sel)
        colsum = jnp.sum(sel * wts, axis=0, keepdims=True)
        urow = _dot(jnp.broadcast_to(colsum, (8, NSELP)).astype(BF16), pair_ref[...])[0:1]
        nz = jnp.where((urow > 0.5) & (page_l < NP - 1), 1.0, 0.0)
        before = _dot(jnp.broadcast_to(nz, (8, 2 * LANES)).astype(BF16), ut_ref[...])[0:1]
        in_slot = (nz > 0.5) & (before == slot_s)
        page_of = jnp.sum(jnp.where(in_slot, page_l.astype(F32), 0.0), axis=1, keepdims=True)
        bits_of = jnp.sum(jnp.where(in_slot, urow, 0.0), axis=1, keepdims=True)
        count = jnp.sum(nz, axis=1, keepdims=True)
        info = jnp.where(col == 0, page_of, jnp.where(col == 1, bits_of, jnp.where(
            col == 2, count, jnp.where(col == 3, pick(urow, NP - 1), jnp.where(col == 4, pick(urow, NP), 0.0)))))
        slots_ref[0, MAX_SEL_PAGES * kh:MAX_SEL_PAGES * (kh + 1), :] = info.astype(I32)


def _nsa_s2(hs3, hc, rel_bias, w2bd, pt, cover, pair, P, t_valid):
    B, T, _ = hs3.shape
    nsub = hc.shape[1]
    nselp = cover.shape[1]
    ut = (jnp.arange(2 * LANES)[:, None] < jnp.arange(2 * LANES)[None, :]).astype(BF16)
    return pl.pallas_call(
        functools.partial(_nsa_s2_body, P=P, t_valid=t_valid, NSELP=nselp),
        grid=(B,),
        in_specs=[_smem_spec(),
                  pl.BlockSpec((1, nsub, 4 * LANES), lambda b: (b, 0, 0)),
                  pl.BlockSpec((1, T, 256), lambda b: (b, 0, C_Q // 256)),
                  _const_spec((2, LANES, LANES)), _const_spec((2, 1, LANES)),
                  _const_spec((nsub, nselp)), _const_spec((nselp, 2 * LANES)), _const_spec((2 * LANES, 2 * LANES))],
        out_specs=[pl.BlockSpec((1, 8, 256), lambda b: (b, 0, 0)),
                   pl.BlockSpec((1, ATT_KV_HEADS * MAX_SEL_PAGES, LANES), lambda b: (b, 0, 0))],
        out_shape=[jax.ShapeDtypeStruct((B, 8, 256), F32),
                   jax.ShapeDtypeStruct((B, ATT_KV_HEADS * MAX_SEL_PAGES, LANES), I32)],
        compiler_params=_cparams("arbitrary"),
        name="nsa_sample_cmp_select",
    )(rel_bias, hc, hs3, w2bd, pt, cover, pair, ut)


def _nsa_s3_body(pt_ref, sl_ref, rb_ref, q_ref, sm_ref, kvn_ref, ocmp_ref, cache_ref, win_ref, y_ref,
                 kcat_ref, vcat_ref, rec_ref, sem, *, layer, P, NP, t_valid):
    b = pl.program_id(0)

    def slot_info(kh, i, field):
        return sl_ref[((b * ATT_KV_HEADS + kh) * MAX_SEL_PAGES + i) * SLOT_INFO + field]

    wb = win_ref.shape[-1]
    ncat = MAX_SEL_PAGES * LANES
    row = lax.broadcasted_iota(I32, (16, 1), 0)
    t16 = row % 8
    g16 = row // 8
    valid_row = t16 < t_valid
    gates = _sigmoid(sm_ref[0])

    @pl.when(b == 0)
    def _():
        kcat_ref[...] = jnp.zeros(kcat_ref.shape, F32)
        vcat_ref[...] = jnp.zeros(vcat_ref.shape, F32)

    rec_copy = pltpu.make_async_copy(cache_ref.at[layer, pt_ref[b * NP + NP - 1], pl.ds(2, 2)], rec_ref, sem.at[1])
    rec_copy.start()

    def kv_copies(kh, phys, i):
        dst = pl.ds(pl.multiple_of(i * LANES, LANES), LANES)
        return (pltpu.make_async_copy(cache_ref.at[layer, phys, 2, kh], kcat_ref.at[kh, :, dst], sem.at[0]),
                pltpu.make_async_copy(cache_ref.at[layer, phys, 3, kh], vcat_ref.at[kh, :, dst], sem.at[0]))

    counts = [slot_info(kh, 0, 2) for kh in range(ATT_KV_HEADS)]
    for kh in range(ATT_KV_HEADS):
        def issue(i, carry, kh=kh):
            ck, cv = kv_copies(kh, pt_ref[b * NP + slot_info(kh, i, 0)], i)
            ck.start()
            cv.start()
            return carry
        lax.fori_loop(0, counts[kh], issue, 0)

    q16fs = [jnp.concatenate([q_ref[0, :, 128 * kh:128 * kh + 64], q_ref[0, :, 128 * kh + 64:128 * kh + 128]], axis=0)
             for kh in range(ATT_KV_HEADS)]

    o_wins = []
    for kh in range(ATT_KV_HEADS):
        q16f = q16fs[kh]
        q16 = q16f.astype(BF16)
        col_w = lax.broadcasted_iota(I32, (1, wb), 1)
        d_w = wb + t16 - col_w
        ok_w = (d_w < WINDOW) & valid_row
        s_w = _dot(q16, win_ref[0, 0, kh].astype(BF16)) * ATT_SCALE + _row_bias(_t5_bucket(d_w), rb_ref, kh, g16)
        s_w = jnp.where(ok_w, s_w, NEG)
        mw = s_w.max(-1, keepdims=True)
        cols = []
        for c in range(t_valid):
            kn = kvn_ref[0, c:c + 1, 512 + 64 * kh:512 + 64 * kh + 64]
            sc = jnp.sum(q16f * kn, axis=-1, keepdims=True) * ATT_SCALE + _row_bias(_t5_bucket(t16 - c), rb_ref, kh, g16)
            ok = (t16 >= c) & valid_row
            sc = jnp.where(ok, sc, NEG)
            cols.append((sc, ok))
            mw = jnp.maximum(mw, sc)
        e_w = jnp.where(ok_w, jnp.exp(s_w - mw), 0.0)
        l_w = e_w.sum(-1, keepdims=True)
        acc_w = _dot_nt(e_w.astype(BF16), win_ref[0, 1, kh].astype(BF16))
        for c, (sc, ok) in enumerate(cols):
            e = jnp.where(ok, jnp.exp(sc - mw), 0.0)
            l_w = l_w + e
            acc_w = acc_w + e * kvn_ref[0, c:c + 1, 640 + 64 * kh:640 + 64 * kh + 64]
        o_wins.append(acc_w / jnp.maximum(l_w, 1e-30))

    for kh in range(ATT_KV_HEADS):
        def wait_pair(i, carry, kh=kh):
            ck, cv = kv_copies(kh, 0, i)
            ck.wait()
            cv.wait()
            return carry
        lax.fori_loop(0, counts[kh], wait_pair, 0)
    rec_copy.wait()

    lane_c = lax.broadcasted_iota(I32, (1, ncat), 1)
    shift_c = t16 + 4 * ((lane_c % LANES) // SEL_BLOCK)
    lane = lax.broadcasted_iota(I32, (1, LANES), 1)
    shift = t16 + 4 * (lane // SEL_BLOCK)
    for kh in range(ATT_KV_HEADS):
        q16f = q16fs[kh]
        q16 = q16f.astype(BF16)
        c_far = jnp.where(g16 == 1, rb_ref[N_BUCKETS - 1, 2 * kh + 1], rb_ref[N_BUCKETS - 1, 2 * kh])

        u_vec = jnp.concatenate([jnp.full((1, LANES), slot_info(kh, i, 1), I32) for i in range(MAX_SEL_PAGES)], axis=1)
        ok_c = (jnp.bitwise_and(jnp.right_shift(u_vec, shift_c), 1) == 1) & valid_row
        s_c = jnp.where(ok_c, _dot(q16, kcat_ref[kh].astype(BF16)) * ATT_SCALE + c_far, NEG)
        u_rec = slot_info(kh, 0, 3)
        d_r = (P + t16) - ((NP - 1) * LANES + lane)
        ok_r = (jnp.bitwise_and(jnp.right_shift(u_rec, shift), 1) == 1) & valid_row
        s_r = _dot(q16, rec_ref[0, kh].astype(BF16)) * ATT_SCALE + _row_bias(_t5_bucket(d_r), rb_ref, kh, g16)
        s_r = jnp.where(ok_r, s_r, NEG)
        u_new = slot_info(kh, 0, 4)
        sel_new = jnp.bitwise_and(jnp.right_shift(u_new, t16), 1) == 1
        m_fin = jnp.maximum(s_c.max(-1, keepdims=True), s_r.max(-1, keepdims=True))
        cols = []
        for c in range(t_valid):
            kn = kvn_ref[0, c:c + 1, 256 + 64 * kh:256 + 64 * kh + 64]
            sc = jnp.sum(q16f * kn, axis=-1, keepdims=True) * ATT_SCALE + _row_bias(_t5_bucket(t16 - c), rb_ref, kh, g16)
            ok = sel_new & (t16 >= c) & valid_row
            sc = jnp.where(ok, sc, NEG)
            cols.append((sc, ok))
            m_fin = jnp.maximum(m_fin, sc)
        e_c = jnp.where(ok_c, jnp.exp(s_c - m_fin), 0.0)
        e_r = jnp.where(ok_r, jnp.exp(s_r - m_fin), 0.0)
        l_fin = e_c.sum(-1, keepdims=True) + e_r.sum(-1, keepdims=True)
        acc = (_dot_nt(e_c.astype(BF16), vcat_ref[kh].astype(BF16))
               + _dot_nt(e_r.astype(BF16), rec_ref[1, kh].astype(BF16)))
        for c, (sc, ok) in enumerate(cols):
            e = jnp.where(ok, jnp.exp(sc - m_fin), 0.0)
            l_fin = l_fin + e
            acc = acc + e * kvn_ref[0, c:c + 1, 384 + 64 * kh:384 + 64 * kh + 64]
        o_sel = acc / jnp.maximum(l_fin, 1e-30)
        o_win = o_wins[kh]

        for g in range(2):
            h = 2 * kh + g
            out = (gates[:, h:h + 1] * ocmp_ref[0, :, 64 * h:64 * h + 64]
                   + gates[:, 4 + h:5 + h] * o_sel[8 * g:8 * g + 8]
                   + gates[:, 8 + h:9 + h] * o_win[8 * g:8 * g + 8])
            y_ref[0, :, 64 * h:64 * h + 64] = out.astype(BF16)


def _nsa_s3(page_flat, slots_flat, rel_bias, hs3, ocmp, cache_t, win_t, layer, P, NP, t_valid):
    B, T, _ = hs3.shape
    wb = win_t.shape[-1]
    grid_spec = pltpu.PrefetchScalarGridSpec(
        num_scalar_prefetch=2,
        grid=(B,),
        in_specs=[_smem_spec(),
                  pl.BlockSpec((1, T, 256), lambda b, *_: (b, 0, C_Q // 256)),
                  pl.BlockSpec((1, T, LANES), lambda b, *_: (b, 0, C_SM // LANES)),
                  pl.BlockSpec((1, T, 768), lambda b, *_: (b, 0, C_KV // 768)),
                  pl.BlockSpec((1, 8, 256), lambda b, *_: (b, 0, 0)),
                  pl.BlockSpec(memory_space=pl.ANY),
                  pl.BlockSpec((1, 2, 2, HEAD_DIM, wb), lambda b, *_: (b, 0, 0, 0, 0))],
        out_specs=pl.BlockSpec((1, T, 256), lambda b, *_: (b, 0, 0)),
        scratch_shapes=[pltpu.VMEM((2, HEAD_DIM, MAX_SEL_PAGES * LANES), F32),
                        pltpu.VMEM((2, HEAD_DIM, MAX_SEL_PAGES * LANES), F32),
                        pltpu.VMEM((2, 2, HEAD_DIM, LANES), F32),
                        pltpu.SemaphoreType.DMA((2,))],
    )
    return pl.pallas_call(
        functools.partial(_nsa_s3_body, layer=layer, P=P, NP=NP, t_valid=t_valid),
        grid_spec=grid_spec,
        out_shape=jax.ShapeDtypeStruct((B, T, 256), BF16),
        compiler_params=_cparams("arbitrary"),
        name="nsa_sample_select_window",
    )(page_flat, slots_flat, rel_bias, hs3, hs3, hs3, ocmp, cache_t, win_t)


def _per_layer_specs(depth, block, index_of_step):
    def spec(k):
        def index_map(l, i):
            idx = index_of_step(i)
            return tuple(jnp.where(l == k, v, 0) if n < 2 else v for n, v in enumerate(idx))
        return pl.BlockSpec(block, index_map)
    return [spec(k) for k in range(depth)]


def _kv_pages_body(*refs):
    o_ref = refs[-1]
    for k, h_ref in enumerate(refs[:-1]):
        @pl.when(pl.program_id(0) == k)
        def _(h_ref=h_ref):
            for j in range(o_ref.shape[1]):
                for c in range(4):
                    x = h_ref[0, LANES * j:LANES * (j + 1), LANES * c:LANES * (c + 1)]
                    o_ref[0, j, c] = x.T.reshape(ATT_KV_HEADS, HEAD_DIM, LANES)


def _kv_pages_t(hs):
    depth = len(hs)
    B, T, _ = hs[0].shape
    rows = min(4 * LANES, T)
    per_b = T // rows
    return pl.pallas_call(
        _kv_pages_body,
        grid=(depth, B * per_b),
        in_specs=_per_layer_specs(depth, (1, rows, 512), lambda i: (i // per_b, i % per_b, 0)),
        out_specs=pl.BlockSpec((1, rows // LANES, 4, ATT_KV_HEADS, HEAD_DIM, LANES), lambda l, i: (l, i, 0, 0, 0, 0)),
        out_shape=jax.ShapeDtypeStruct((depth, B * T // LANES, 4, ATT_KV_HEADS, HEAD_DIM, LANES), F32),
        compiler_params=_cparams("arbitrary", "arbitrary"),
        name="kv_pages_token_minor",
    )(*hs)


def _win_rows_body(*refs):
    o_ref = refs[-1]
    for k, h_ref in enumerate(refs[:-1]):
        @pl.when(pl.program_id(0) == k)
        def _(h_ref=h_ref):
            for c in range(2):
                x = h_ref[0, :, LANES * c:LANES * (c + 1)]
                o_ref[0, 0, c] = x.T.reshape(ATT_KV_HEADS, HEAD_DIM, x.shape[0])


def _win_rows_t(hs, wrows):
    depth = len(hs)
    B, T, _ = hs[0].shape
    last = T // wrows - 1
    return pl.pallas_call(
        _win_rows_body,
        grid=(depth, B),
        in_specs=_per_layer_specs(depth, (1, wrows, 256), lambda b: (b, last, (C_KV + 512) // 256)),
        out_specs=pl.BlockSpec((1, 1, 2, ATT_KV_HEADS, HEAD_DIM, wrows), lambda l, b: (l, b, 0, 0, 0, 0)),
        out_shape=jax.ShapeDtypeStruct((depth, B, 2, ATT_KV_HEADS, HEAD_DIM, wrows), F32),
        compiler_params=_cparams("arbitrary", "arbitrary"),
        name="win_rows_token_minor",
    )(*hs)


def _blockdiag2(w):
    z = jnp.zeros_like(w)
    return jnp.concatenate([jnp.concatenate([w, z], axis=-1), jnp.concatenate([z, w], axis=-1)], axis=-2)


def _prep_layer(w_in, pool_w, cmp_w1, cmp_w2, alog, dtb):
    glu, pool, q, kv, gate, gqkv, z, a, b = jnp.split(w_in, [512, 768, 1024, 1792, 1804, 2572, 2828, 2832], axis=1)
    pad = jnp.zeros((D_MODEL, IN_PAD - C_SM - 20), F32)
    w_in_p = jnp.concatenate([kv, gqkv, glu, pool, q, z, gate, a, b, pad], axis=1).astype(BF16)
    wblk = jnp.zeros((POOL_CH, POOL_CH), F32)
    for gi in range(4):
        wblk = wblk.at[64 * gi:64 * gi + 64, 64 * gi:64 * gi + 64].set(pool_w[gi])
    w1 = cmp_w1.reshape(2, CMP_BLOCK, HEAD_DIM, HEAD_DIM)
    wcat = jnp.concatenate([_blockdiag2(w1[:, :CMP_STRIDE]), _blockdiag2(w1[:, CMP_STRIDE:])], axis=-1).astype(BF16)
    wcat = wcat.reshape(2, CMP_STRIDE // 2, 2 * LANES, 2 * LANES)
    w2bd = _blockdiag2(cmp_w2).astype(BF16)
    lane_pad = lambda v: jnp.zeros((1, LANES), F32).at[0, SM_A:SM_A + GDN_HEADS].set(v)
    return w_in_p, wblk.astype(BF16), wcat, w2bd, lane_pad(alog), lane_pad(dtb)


def _cover_matrix(nsub, n_sel, cols):
    n = jnp.arange(nsub)[:, None] * CMP_STRIDE
    j = jnp.arange(cols)[None, :] * SEL_BLOCK
    cov = (n < j + SEL_BLOCK) & (n + CMP_BLOCK > j) & (jnp.arange(cols)[None, :] < n_sel) & (jnp.arange(nsub)[:, None] < nsub - 1)
    return cov.astype(BF16)


def kernel(x_prompt, x_sample, cache_nsa_kv, cache_win_kv, state_conv, state_pool, state_gdn_conv, state_gdn,
           page_table, w_in, conv_dw, conv_dw_b, conv_ln_g, conv_ln_b, conv_pw, pool_w, pool_scale,
           cmp_pe, cmp_w1, cmp_w2, gdn_conv_w, gdn_a_log, gdn_dt_bias, gdn_norm_g,
           w_out, ln1_g, ln1_b, w_up, w_down, ln2_g, ln2_b, rel_bias):
    depth = w_in.shape[0]
    BP, T, _ = x_prompt.shape
    BS, TS, _ = x_sample.shape
    NP = page_table.shape[1]
    page = cache_nsa_kv.shape[2]
    P = NP * page
    TSP = 8
    assert page == LANES and TS <= TSP and TS < CMP_STRIDE and T % QT == 0 and P % SEL_BLOCK == 0

    cache_t = jnp.transpose(cache_nsa_kv, (0, 1, 3, 4, 5, 2))
    win_t = jnp.transpose(cache_win_kv, (0, 1, 3, 4, 5, 2))
    page_flat = page_table.reshape(-1)

    tabw, tabc = _bias_tables(rel_bias, T)
    pterm = _peterm(cmp_pe.reshape(depth * 2, 1, CMP_BLOCK * HEAD_DIM), cmp_w1.reshape(depth * 2, CMP_BLOCK * HEAD_DIM, HEAD_DIM))
    pterm = pterm[:, 0:1, :].reshape(depth, 2, 1, HEAD_DIM)
    pterm = jnp.concatenate([pterm, pterm], axis=-1)

    ns_p, nsel_p = T // CMP_STRIDE, T // SEL_BLOCK
    covert_p = _cover_matrix(ns_p, nsel_p, nsel_p).T
    ns_s = P // CMP_STRIDE
    nsel_s = P // SEL_BLOCK + 1
    nselp_s = -(-nsel_s // LANES) * LANES
    cover_s = _cover_matrix(ns_s + 1, nsel_s, nselp_s)[:ns_s]
    jj = jnp.arange(nselp_s)[:, None]
    pp = jnp.arange(2 * LANES)[None, :]
    pair = (((jj // 2 == pp) & (jj < 2 * NP)) | ((jj == 2 * NP) & (pp == NP))).astype(BF16)

    yp = x_prompt.reshape(BP * T, D_MODEL)
    ys = jnp.pad(x_sample, ((0, 0), (0, TSP - TS), (0, 0))).reshape(BS * TSP, D_MODEL)
    zeros = lambda *s: jnp.zeros(s, F32)
    outs_p = [[] for _ in range(6)]
    outs_s = [[] for _ in range(6)]
    hps = []
    preps = [_prep_layer(w_in[l], pool_w[l], cmp_w1[l], cmp_w2[l], gdn_a_log[l], gdn_dt_bias[l]) for l in range(depth)]
    hcs = [_nsa_s1(page_flat, cache_t, preps[l][2], l, BS, NP) for l in range(depth)]
    for l in range(depth):
        w_in_p, wblk, wcat, w2bd, alog_l, dtb_l = preps[l]
        lw = dict(w_out=w_out[l].astype(BF16), ln1_g=ln1_g[l][None], ln1_b=ln1_b[l][None],
                  w_up=w_up[l].astype(BF16), w_down=w_down[l].astype(BF16), ln2_g=ln2_g[l][None], ln2_b=ln2_b[l][None])
        conv_w = (conv_dw[l], conv_dw_b[l][None], conv_ln_g[l][None], conv_ln_b[l][None], conv_pw[l].astype(BF16))
        gdn_w = (gdn_conv_w[l], alog_l, dtb_l, gdn_norm_g[l][None])

        hp = _proj_in(yp, w_in_p, 512).reshape(BP, T, IN_PAD)
        m_conv, conv_new = _conv_mixer(hp, zeros(BP, CONV_WIDTH - 1, CONV_CH), *conv_w, t_valid=T)
        m_pool, pool_new = _pool_mixer(hp, zeros(BP, POOL_BUF, POOL_CH), wblk, pool_scale[l][None], t_valid=T, offset=0)
        m_att = _nsa_prompt(hp, rel_bias, wcat, w2bd, pterm[l], tabw, tabc, covert_p)
        m_gdn, gbuf_new, s_new = _gdn_mixer(hp, zeros(BP, GDN_CONV - 1, GDN_QKV), zeros(BP, GDN_HEADS, GDN_DK, GDN_DV),
                                            *gdn_w, t_valid=T)
        mixes = [m.reshape(BP * T, GROUP_WIDTH) for m in (m_conv, m_pool, m_att, m_gdn)]
        yp = _out_ffn(yp, mixes, lw, 512)
        hps.append(hp)
        for lst, arr in zip(outs_p[2:], (conv_new, pool_new, gbuf_new, s_new)):
            lst.append(arr)

        hs = _proj_in(ys, w_in_p, BS * TSP).reshape(BS, TSP, IN_PAD)
        m_conv, conv_new = _conv_mixer(hs, state_conv[l], *conv_w, t_valid=TS)
        m_pool, pool_new = _pool_mixer(hs, state_pool[l], wblk, pool_scale[l][None], t_valid=TS, offset=P)
        ocmp, slots = _nsa_s2(hs, hcs[l], rel_bias, w2bd, pterm[l], cover_s, pair, P, TS)
        m_att = _nsa_s3(page_flat, slots[:, :, :SLOT_INFO].reshape(-1), rel_bias, hs, ocmp, cache_t, win_t[l], l, P, NP, TS)
        m_gdn, gbuf_new, s_new = _gdn_mixer(hs, state_gdn_conv[l], state_gdn[l], *gdn_w, t_valid=TS)
        mixes = [m.reshape(BS * TSP, GROUP_WIDTH) for m in (m_conv, m_pool, m_att, m_gdn)]
        ys = _out_ffn(ys, mixes, lw, BS * TSP)
        kv_new = hs[:, :TS, C_KV:C_KV + 768].reshape(BS, TS, 6, ATT_KV_HEADS, HEAD_DIM)
        outs_s[0].append(kv_new[:, :, :4])
        kw_all = jnp.concatenate([cache_win_kv[l], kv_new[:, :, 4:]], axis=1)
        outs_s[1].append(kw_all[:, -min(WINDOW, kw_all.shape[1]):])
        for lst, arr in zip(outs_s[2:], (conv_new, pool_new, gbuf_new, s_new)):
            lst.append(arr)

    p_nsa = jnp.transpose(_kv_pages_t(hps), (0, 1, 5, 2, 3, 4))
    p_win = jnp.transpose(_win_rows_t(hps, min(WINDOW, T)), (0, 1, 5, 2, 3, 4))
    p_rest = [p_win] + [jnp.stack(a) for a in outs_p[2:]]
    s_all = [jnp.stack(a) for a in outs_s]
    y_s = ys.reshape(BS, TSP, D_MODEL)[:, :TS]
    return (yp.reshape(BP, T, D_MODEL), y_s, p_nsa, *p_rest, *s_all)
```

```python
import functools
import math

import jax
import jax.numpy as jnp
from jax import lax
from jax.experimental import pallas as pl
from jax.experimental.pallas import tpu as pltpu

F32 = jnp.float32
BF16 = jnp.bfloat16
I32 = jnp.int32

D_MODEL = 1024
GROUP_WIDTH = 256
CONV_CH = 256
CONV_WIDTH = 31
POOL_CH = 256
POOL_WINDOWS = (2, 4, 8, 16)
POOL_BUF = 15
ATT_HEADS = 4
ATT_KV_HEADS = 2
HEAD_DIM = 64
ATT_SCALE = HEAD_DIM ** -0.5
CMP_STRIDE = 16
CMP_BLOCK = 32
SEL_BLOCK = 64
SEL_TOPN = 16
WINDOW = 512
N_BUCKETS = 32
GDN_HEADS = 4
GDN_DK = 64
GDN_DV = 64
GDN_QKV = 768
GDN_CONV = 4
GDN_CHUNK = 64
D_FF = 4096
DEPTH = 2
DN_ALPHA = (2 * DEPTH) ** 0.25
LN_EPS = 1e-5
NEG = -1e30
FORCE = 1e4

LANES = 128
VMEM_LIMIT_BYTES = 56 * 1024 * 1024

C_KV, C_GQKV, C_GLU, C_POOL, C_Q, C_Z, C_SM = 0, 768, 1536, 2048, 2304, 2560, 2816
IN_PAD = 2944
SM_GATE, SM_A, SM_B = 0, 12, 16


def _cparams(*sem):
    return pltpu.CompilerParams(dimension_semantics=sem, vmem_limit_bytes=VMEM_LIMIT_BYTES)


def _const_spec(shape):
    nd = len(shape)
    return pl.BlockSpec(shape, lambda *_: (0,) * nd, pipeline_mode=pl.Buffered(1))


def _smem_spec():
    return pl.BlockSpec(memory_space=pltpu.SMEM)


def _sigmoid(x):
    return 0.5 * jnp.tanh(0.5 * x) + 0.5


def _silu(x):
    h = 0.5 * x
    return h * jnp.tanh(h) + h


def _layer_norm(y, g, b):
    mu = jnp.mean(y, axis=-1, keepdims=True)
    yc = y - mu
    var = jnp.mean(yc * yc, axis=-1, keepdims=True)
    return yc * lax.rsqrt(var + LN_EPS) * g + b


def _dot(a, b):
    return jnp.dot(a, b, preferred_element_type=F32)


def _dot_nt(a, b):
    return lax.dot_general(a, b, (((1,), (1,)), ((), ())), preferred_element_type=F32)


def _dot_tn(a, b):
    return lax.dot_general(a, b, (((0,), (0,)), ((), ())), preferred_element_type=F32)


def _split3(x):
    x1 = x.astype(BF16)
    r = x - x1.astype(F32)
    x2 = r.astype(BF16)
    x3 = (r - x2.astype(F32)).astype(BF16)
    return x1, x2, x3


def _dot_hl(a, b):
    ah = a.astype(BF16)
    al = (a - ah.astype(F32)).astype(BF16)
    bh = b.astype(BF16)
    bl = (b - bh.astype(F32)).astype(BF16)
    lhs = jnp.concatenate([ah, al, ah], axis=1)
    rhs = jnp.concatenate([bh, bh, bl], axis=0)
    return _dot(lhs, rhs)


def _t5_bucket(d):
    d = jnp.maximum(d, 0)
    logd = jnp.log(jnp.maximum(d, 1).astype(F32) / 16.0) / math.log(8.0)
    large = jnp.minimum(16 + (logd * 16.0).astype(I32), N_BUCKETS - 1)
    return jnp.where(d < 16, d, large)


def _bias_lookup(bk, value_of_bucket):
    out = jnp.zeros(bk.shape, F32)
    for k in range(N_BUCKETS):
        out = jnp.where(bk == k, value_of_bucket(k), out)
    return out


def _masked_softmax_parts(parts):
    ss = [jnp.where(ok, s, NEG) for s, ok in parts]
    mx = ss[0].max(-1, keepdims=True)
    for s in ss[1:]:
        mx = jnp.maximum(mx, s.max(-1, keepdims=True))
    es = [jnp.where(ok, jnp.exp(s - mx), 0.0) for s, (_, ok) in zip(ss, parts)]
    tot = es[0].sum(-1, keepdims=True)
    for e in es[1:]:
        tot = tot + e.sum(-1, keepdims=True)
    inv = 1.0 / jnp.maximum(tot, 1e-30)
    return [e * inv for e in es]


def _proj_in_body(x_ref, w_ref, o_ref):
    xb = x_ref[...].astype(BF16)
    for a in range(0, IN_PAD, 512):
        b = min(a + 512, IN_PAD)
        o_ref[:, a:b] = _dot(xb, w_ref[:, a:b])


def _proj_in(x, w, tm):
    n = x.shape[0]
    return pl.pallas_call(
        _proj_in_body,
        grid=(n // tm,),
        in_specs=[pl.BlockSpec((tm, D_MODEL), lambda i: (i, 0)), _const_spec((D_MODEL, IN_PAD))],
        out_specs=pl.BlockSpec((tm, IN_PAD), lambda i: (i, 0)),
        out_shape=jax.ShapeDtypeStruct((n, IN_PAD), F32),
        compiler_params=_cparams("arbitrary"),
        name="proj_in",
    )(x, w)


FF_CHUNK = 1024


def _out_ffn_body(x_ref, m0_ref, m1_ref, m2_ref, m3_ref, wo_ref, g1_ref, b1_ref, wu_ref, wd_ref, g2_ref, b2_ref, o_ref):
    acc = _dot(m0_ref[...], wo_ref[0:256, :])
    acc += _dot(m1_ref[...], wo_ref[256:512, :])
    acc += _dot(m2_ref[...], wo_ref[512:768, :])
    acc += _dot(m3_ref[...], wo_ref[768:1024, :])
    x1 = _layer_norm(DN_ALPHA * x_ref[...] + acc, g1_ref[...], b1_ref[...])
    xb = x1.astype(BF16)
    acc = jnp.zeros(x1.shape, F32)
    for c in range(0, D_FF, FF_CHUNK):
        h = _dot(xb, wu_ref[:, c:c + FF_CHUNK])
        a = jnp.square(jnp.maximum(h, 0.0)).astype(BF16)
        acc += _dot(a, wd_ref[c:c + FF_CHUNK, :])
    o_ref[...] = _layer_norm(DN_ALPHA * x1 + acc, g2_ref[...], b2_ref[...])


def _out_ffn(x, mixes, lw, tm):
    n = x.shape[0]
    row = lambda i: (i, 0)
    vec = _const_spec((1, D_MODEL))
    return pl.pallas_call(
        _out_ffn_body,
        grid=(n // tm,),
        in_specs=[pl.BlockSpec((tm, D_MODEL), row)] + [pl.BlockSpec((tm, GROUP_WIDTH), row)] * 4
        + [_const_spec((D_MODEL, D_MODEL)), vec, vec, _const_spec((D_MODEL, D_FF)), _const_spec((D_FF, D_MODEL)), vec, vec],
        out_specs=pl.BlockSpec((tm, D_MODEL), row),
        out_shape=jax.ShapeDtypeStruct((n, D_MODEL), F32),
        compiler_params=_cparams("arbitrary"),
        name="proj_out_ffn",
    )(x, *mixes, lw["w_out"], lw["ln1_g"], lw["ln1_b"], lw["w_up"], lw["w_down"], lw["ln2_g"], lw["ln2_b"])


CONV_PAD = 32


def _conv_body(h_ref, buf_ref, dw_ref, dwb_ref, g_ref, b_ref, pw_ref, y_ref, new_ref, full_ref, *, T, t_valid):
    hh = h_ref[0]
    full_ref[0:8, :] = jnp.zeros((8, CONV_CH), F32)
    full_ref[2:CONV_PAD, :] = buf_ref[0]
    full_ref[CONV_PAD:CONV_PAD + T, :] = hh[:, :CONV_CH] * _sigmoid(hh[:, CONV_CH:])
    new_ref[0] = full_ref[t_valid + 2:t_valid + CONV_PAD, :]
    rc = min(T, 128)

    def chunk(c, carry):
        base = pl.multiple_of(c * rc, rc)
        win = full_ref[pl.ds(base, rc + CONV_PAD), :]
        acc = jnp.zeros((rc, CONV_CH), F32) + dwb_ref[...]
        for r in range(8):
            shifted = win[r:r + (rc + CONV_PAD - r) // 8 * 8, :]
            for k in range(CONV_WIDTH):
                if (2 + k) % 8 == r:
                    a = (2 + k) // 8 * 8
                    acc = acc + dw_ref[k:k + 1, :] * shifted[a:a + rc, :]
        y = _silu(_layer_norm(acc, g_ref[...], b_ref[...]))
        y_ref[0, pl.ds(base, rc), :] = _dot(y.astype(BF16), pw_ref[...]).astype(BF16)
        return carry

    lax.fori_loop(0, T // rc, chunk, 0)


def _conv_mixer(h3, buf, dw, dwb, g, b, pw, t_valid):
    B, T, _ = h3.shape
    return pl.pallas_call(
        functools.partial(_conv_body, T=T, t_valid=t_valid),
        grid=(B,),
        in_specs=[pl.BlockSpec((1, T, 2 * CONV_CH), lambda i: (i, 0, C_GLU // (2 * CONV_CH))),
                  pl.BlockSpec((1, CONV_WIDTH - 1, CONV_CH), lambda i: (i, 0, 0)),
                  _const_spec((CONV_WIDTH, CONV_CH)), _const_spec((1, CONV_CH)), _const_spec((1, CONV_CH)),
                  _const_spec((1, CONV_CH)), _const_spec((CONV_CH, CONV_CH))],
        out_specs=[pl.BlockSpec((1, T, CONV_CH), lambda i: (i, 0, 0)),
                   pl.BlockSpec((1, CONV_WIDTH - 1, CONV_CH), lambda i: (i, 0, 0))],
        out_shape=[jax.ShapeDtypeStruct((B, T, CONV_CH), BF16),
                   jax.ShapeDtypeStruct((B, CONV_WIDTH - 1, CONV_CH), F32)],
        scratch_shapes=[pltpu.VMEM((T + CONV_PAD, CONV_CH), F32)],
        compiler_params=_cparams("arbitrary"),
        name="conv_mixer",
    )(h3, buf, dw, dwb, g, b, pw)


POOL_PAD = 16


def _pool_body(h_ref, buf_ref, w_ref, sc_ref, y_ref, new_ref, full_ref, *, T, t_valid, offset):
    full_ref[0:8, :] = jnp.zeros((8, POOL_CH), F32)
    full_ref[1:POOL_PAD, :] = buf_ref[0]
    full_ref[POOL_PAD:POOL_PAD + T, :] = h_ref[0]
    new_ref[0] = full_ref[t_valid + 1:t_valid + POOL_PAD, :]
    rc = min(T, 128)
    lane = lax.broadcasted_iota(I32, (1, POOL_CH), 1)
    group = lane // (POOL_CH // len(POOL_WINDOWS))
    wl = jnp.where(group == 0, 2, jnp.where(group == 1, 4, jnp.where(group == 2, 8, 16)))

    def chunk(c, carry):
        base = pl.multiple_of(c * rc, rc)
        win = full_ref[pl.ds(base, rc + POOL_PAD), :]
        x0 = win[POOL_PAD:POOL_PAD + rc, :]
        sums = {}
        acc = x0
        for i in range(1, 16):
            acc = acc + win[POOL_PAD - i:POOL_PAD - i + rc, :]
            if i + 1 in POOL_WINDOWS:
                sums[i + 1] = acc
        sel = jnp.where(group == 0, sums[2], jnp.where(group == 1, sums[4], jnp.where(group == 2, sums[8], sums[16])))
        pos = offset + base + lax.broadcasted_iota(I32, (rc, 1), 0)
        cnt = jnp.minimum(pos + 1, wl).astype(F32)
        d = sel / cnt - x0
        y_ref[0, pl.ds(base, rc), :] = (_dot(d.astype(BF16), w_ref[...]) * sc_ref[...]).astype(BF16)
        return carry

    lax.fori_loop(0, T // rc, chunk, 0)


def _pool_mixer(h3, buf, wblk, scale, t_valid, offset):
    B, T, _ = h3.shape
    return pl.pallas_call(
        functools.partial(_pool_body, T=T, t_valid=t_valid, offset=offset),
        grid=(B,),
        in_specs=[pl.BlockSpec((1, T, POOL_CH), lambda i: (i, 0, C_POOL // POOL_CH)),
                  pl.BlockSpec((1, POOL_BUF, POOL_CH), lambda i: (i, 0, 0)),
                  _const_spec((POOL_CH, POOL_CH)), _const_spec((1, POOL_CH))],
        out_specs=[pl.BlockSpec((1, T, POOL_CH), lambda i: (i, 0, 0)),
                   pl.BlockSpec((1, POOL_BUF, POOL_CH), lambda i: (i, 0, 0))],
        out_shape=[jax.ShapeDtypeStruct((B, T, POOL_CH), BF16),
                   jax.ShapeDtypeStruct((B, POOL_BUF, POOL_CH), F32)],
        scratch_shapes=[pltpu.VMEM((T + POOL_PAD, POOL_CH), F32)],
        compiler_params=_cparams("arbitrary"),
        name="pool_mixer",
    )(h3, buf, wblk, scale)


GDN_PAD = 8
CK = GDN_CHUNK


def _gdn_body(qkv_ref, z_ref, sm_ref, buf_ref, s0_ref, cw_ref, alog_ref, dtb_ref, ng_ref,
              y_ref, newbuf_ref, sout_ref, full_ref, c_ref, g_ref, bt_ref, gi_ref, bi_ref, u_ref, w_ref, a_ref,
              qg_ref, kdt_ref, s_ref, *, T, Tp, t_valid):
    full_ref[0:8, :] = jnp.zeros((8, GDN_QKV), F32)
    full_ref[5:GDN_PAD, :] = buf_ref[0]
    full_ref[GDN_PAD:GDN_PAD + T, :] = qkv_ref[0]
    newbuf_ref[0] = full_ref[t_valid + 5:t_valid + GDN_PAD, :]
    if Tp > t_valid:
        c_ref[...] = jnp.zeros((Tp, GDN_QKV), F32)
        g_ref[...] = jnp.zeros((Tp, LANES), F32)
        bt_ref[...] = jnp.zeros((Tp, LANES), F32)

    rc = min(t_valid, 128)

    def conv_chunk(c, carry):
        base = pl.multiple_of(c * rc, rc)
        win = full_ref[pl.ds(base, rc + GDN_PAD), :] if rc % 8 == 0 else full_ref[0:rc + GDN_PAD, :]
        acc = jnp.zeros((rc, GDN_QKV), F32)
        for k in range(GDN_CONV):
            acc = acc + cw_ref[k:k + 1, :] * win[5 + k:5 + k + rc, :]
        sm = sm_ref[0, pl.ds(base, rc), :] if rc % 8 == 0 else sm_ref[0, 0:rc, :]
        x = sm + dtb_ref[...]
        softplus = jnp.maximum(x, 0.0) + jnp.log1p(jnp.exp(-jnp.abs(x)))
        gv = -jnp.exp(alog_ref[...]) * softplus
        bv = _sigmoid(sm)
        if rc % 8 == 0:
            c_ref[pl.ds(base, rc), :] = _silu(acc)
            g_ref[pl.ds(base, rc), :] = gv
            bt_ref[pl.ds(base, rc), :] = bv
        else:
            c_ref[0:rc, :] = _silu(acc)
            g_ref[0:rc, :] = gv
            bt_ref[0:rc, :] = bv
        return carry

    lax.fori_loop(0, t_valid // rc, conv_chunk, 0)

    HW = GDN_HEADS * GDN_DK
    lane = lax.broadcasted_iota(I32, (1, HW), 1)
    hmask = [jnp.where(lane // GDN_DK == h, 1.0, 0.0).astype(BF16) for h in range(GDN_HEADS)]
    row = lax.broadcasted_iota(I32, (CK, 1), 0)
    jl = lane % CK
    incl = row >= jl
    strict = row > jl
    eye_all = jnp.where(row == jl, 1.0, 0.0)
    er = lax.broadcasted_iota(I32, (LANES, HW), 0)
    ec = lax.broadcasted_iota(I32, (LANES, HW), 1) // GDN_DK
    exp_a = jnp.where(er == ec + SM_A, 1.0, 0.0).astype(BF16)
    exp_b = jnp.where(er == ec + SM_B, 1.0, 0.0).astype(BF16)
    br = lax.broadcasted_iota(I32, (HW, HW), 0) // GDN_DK
    bc = lax.broadcasted_iota(I32, (HW, HW), 1) // GDN_DK
    same_head = br == bc
    bones = jnp.where(same_head, 1.0, 0.0).astype(BF16)

    def blockdiag(x):
        return jnp.concatenate([x * m for m in hmask], axis=0)

    def expand3(x, e):
        x1, x2, x3 = _split3(x)
        return _dot(jnp.concatenate([x1, x2, x3], axis=1), jnp.concatenate([e, e, e], axis=0))

    def bd_dot_hl(a, b):
        ah = a.astype(BF16)
        al = (a - ah.astype(F32)).astype(BF16)
        bh = b.astype(BF16)
        bl = (b - bh.astype(F32)).astype(BF16)
        bdh = blockdiag(bh)
        return _dot(jnp.concatenate([ah, al, ah], axis=1), jnp.concatenate([bdh, bdh, blockdiag(bl)], axis=0))

    rb = min(Tp, 4 * CK)
    row_in_chunk = lax.broadcasted_iota(I32, (rb, 1), 0) % CK

    def prep(c, carry):
        r0 = pl.multiple_of(c * rb, rb)
        g = g_ref[pl.ds(r0, rb), :]
        for s in (1, 2, 4, 8, 16, 32):
            g = g + jnp.where(row_in_chunk >= s, jnp.roll(g, s, axis=0), 0.0)
        gi_ref[pl.ds(r0, rb), :] = expand3(g, exp_a)
        bi_ref[pl.ds(r0, rb), :] = expand3(bt_ref[pl.ds(r0, rb), :], exp_b)
        for part, scale in ((0, GDN_DK ** -0.5), (1, 1.0)):
            x = c_ref[pl.ds(r0, rb), HW * part:HW * (part + 1)]
            ssq = expand3(x * x, bones)
            c_ref[pl.ds(r0, rb), HW * part:HW * (part + 1)] = x * lax.rsqrt(ssq + 1e-6) * scale
        return carry

    lax.fori_loop(0, Tp // rb, prep, 0)

    n_chunks = Tp // CK
    group = 8 if n_chunks % 8 == 0 else 1

    def solve(it, carry):
        r0s = [pl.multiple_of((it * group + k) * CK, CK) for k in range(group)]
        gi = [gi_ref[pl.ds(r0, CK), :] for r0 in r0s]
        kn = [c_ref[pl.ds(r0, CK), HW:2 * HW] for r0 in r0s]
        decay = [jnp.exp(jnp.where(incl, g - jnp.sum(eye_all * g, axis=0, keepdims=True), NEG)) for g in gi]
        kb = [k * bi_ref[pl.ds(r0, CK), :] for k, r0 in zip(kn, r0s)]
        kst = [blockdiag(k.astype(BF16)) for k in kn]
        pw = [-jnp.where(strict, _dot_nt(b.astype(BF16), s) * d, 0.0) for b, s, d in zip(kb, kst, decay)]
        tinv = [eye_all + p for p in pw]
        pw = [bd_dot_hl(p, p) for p in pw]
        for _ in range(4):
            both = [bd_dot_hl(jnp.concatenate([t, p], axis=0), p) for t, p in zip(tinv, pw)]
            tinv = [t + b[:CK] for t, b in zip(tinv, both)]
            pw = [b[CK:] for b in both]
        tinv = [t + bd_dot_hl(t, p) for t, p in zip(tinv, pw)]
        for k, r0 in enumerate(r0s):
            vb = c_ref[pl.ds(r0, CK), 2 * HW:3 * HW] * bi_ref[pl.ds(r0, CK), :]
            u_ref[pl.ds(r0, CK), :] = bd_dot_hl(tinv[k], vb)
        for k, r0 in enumerate(r0s):
            w_ref[pl.ds(r0, CK), :] = bd_dot_hl(tinv[k], kb[k] * jnp.exp(gi[k]))
        for k, r0 in enumerate(r0s):
            qn = c_ref[pl.ds(r0, CK), 0:HW]
            a_ref[pl.ds(r0, CK), :] = (_dot_nt(qn.astype(BF16), kst[k]) * decay[k]).astype(BF16)
            qg_ref[pl.ds(r0, CK), :] = (qn * jnp.exp(gi[k])).astype(BF16)
            kdt_ref[it * group + k] = (kn[k] * jnp.exp(gi[k][CK - 1:CK, :] - gi[k])).T.astype(BF16)
        return carry

    lax.fori_loop(0, n_chunks // group, solve, 0)

    s_ref[...] = jnp.zeros((HW, HW), F32)
    for h in range(GDN_HEADS):
        s_ref[GDN_DK * h:GDN_DK * (h + 1), GDN_DV * h:GDN_DV * (h + 1)] = s0_ref[0, h]
    rows_out = min(CK, T)
    ng_all = jnp.concatenate([ng_ref[...]] * GDN_HEADS, axis=1)

    def recur(c, carry):
        r0 = pl.multiple_of(c * CK, CK)
        s_all = s_ref[...]
        s_b = s_all.astype(BF16)
        vnew = u_ref[pl.ds(r0, CK), :] - _dot(w_ref[pl.ds(r0, CK), :].astype(BF16), s_b)
        vnb = vnew.astype(BF16)
        glast = gi_ref[pl.ds(r0 + CK - 1, 1), :]
        s_ref[...] = s_all * jnp.exp(glast) + jnp.where(same_head, _dot(kdt_ref[c], vnb), 0.0)
        u_ref[pl.ds(r0, CK), :] = _dot(qg_ref[pl.ds(r0, CK), :], s_b) + _dot(a_ref[pl.ds(r0, CK), :], blockdiag(vnb))
        return carry

    lax.fori_loop(0, Tp // CK, recur, 0)
    for h in range(GDN_HEADS):
        sout_ref[0, h] = s_ref[GDN_DK * h:GDN_DK * (h + 1), GDN_DV * h:GDN_DV * (h + 1)]

    def finish(c, carry):
        r0 = pl.multiple_of(c * rb, rb)
        o = u_ref[pl.ds(r0, rb), :]
        on = o * lax.rsqrt(expand3(o * o, bones) * (1.0 / GDN_DV) + LN_EPS) * ng_all
        if T >= CK:
            y_ref[0, pl.ds(r0, rb), :] = (on * _silu(z_ref[0, pl.ds(r0, rb), :])).astype(BF16)
        else:
            y_ref[0] = (on[0:rows_out] * _silu(z_ref[0])).astype(BF16)
        return carry

    lax.fori_loop(0, Tp // rb, finish, 0)


def _gdn_mixer(h3, buf, s0, cw, alog_l, dtb_l, ng, t_valid):
    B, T, _ = h3.shape
    Tp = -(-T // CK) * CK
    return pl.pallas_call(
        functools.partial(_gdn_body, T=T, Tp=Tp, t_valid=t_valid),
        grid=(B,),
        in_specs=[pl.BlockSpec((1, T, GDN_QKV), lambda i: (i, 0, C_GQKV // GDN_QKV)),
                  pl.BlockSpec((1, T, 256), lambda i: (i, 0, C_Z // 256)),
                  pl.BlockSpec((1, T, LANES), lambda i: (i, 0, C_SM // LANES)),
                  pl.BlockSpec((1, GDN_CONV - 1, GDN_QKV), lambda i: (i, 0, 0)),
                  pl.BlockSpec((1, GDN_HEADS, GDN_DK, GDN_DV), lambda i: (i, 0, 0, 0)),
                  _const_spec((GDN_CONV, GDN_QKV)), _const_spec((1, LANES)), _const_spec((1, LANES)),
                  _const_spec((1, GDN_DV))],
        out_specs=[pl.BlockSpec((1, T, 256), lambda i: (i, 0, 0)),
                   pl.BlockSpec((1, GDN_CONV - 1, GDN_QKV), lambda i: (i, 0, 0)),
                   pl.BlockSpec((1, GDN_HEADS, GDN_DK, GDN_DV), lambda i: (i, 0, 0, 0))],
        out_shape=[jax.ShapeDtypeStruct((B, T, 256), BF16),
                   jax.ShapeDtypeStruct((B, GDN_CONV - 1, GDN_QKV), F32),
                   jax.ShapeDtypeStruct((B, GDN_HEADS, GDN_DK, GDN_DV), F32)],
        scratch_shapes=[pltpu.VMEM((T + GDN_PAD, GDN_QKV), F32), pltpu.VMEM((Tp, GDN_QKV), F32),
                        pltpu.VMEM((Tp, LANES), F32), pltpu.VMEM((Tp, LANES), F32),
                        pltpu.VMEM((Tp, 256), F32), pltpu.VMEM((Tp, 256), F32),
                        pltpu.VMEM((Tp, 256), F32), pltpu.VMEM((Tp, 256), F32), pltpu.VMEM((Tp, 256), BF16),
                        pltpu.VMEM((Tp, 256), BF16), pltpu.VMEM((Tp // CK, GDN_HEADS * GDN_DK, CK), BF16),
                        pltpu.VMEM((GDN_HEADS * GDN_DK, GDN_HEADS * GDN_DV), F32)],
        compiler_params=_cparams("arbitrary"),
        name="gdn_mixer",
    )(h3, h3, h3, buf, s0, cw, alog_l, dtb_l, ng)


def _peterm_body(pe_ref, w1_ref, o_ref):
    pe = jnp.broadcast_to(pe_ref[0], (8, CMP_BLOCK * HEAD_DIM)).astype(BF16)
    o_ref[0] = _dot(pe, w1_ref[0].astype(BF16))


def _peterm(pe_flat, w1):
    n = pe_flat.shape[0]
    return pl.pallas_call(
        _peterm_body,
        grid=(n,),
        in_specs=[pl.BlockSpec((1, 1, CMP_BLOCK * HEAD_DIM), lambda i: (i, 0, 0)),
                  pl.BlockSpec((1, CMP_BLOCK * HEAD_DIM, HEAD_DIM), lambda i: (i, 0, 0))],
        out_specs=pl.BlockSpec((1, 8, HEAD_DIM), lambda i: (i, 0, 0)),
        out_shape=jax.ShapeDtypeStruct((n, 8, HEAD_DIM), F32),
        compiler_params=_cparams("arbitrary"),
        name="cmp_pe_term",
    )(pe_flat, w1)


QT = 128
WBAND = WINDOW + QT


def _tabw_body(rb_ref, o_ref):
    i = lax.broadcasted_iota(I32, (QT, WBAND), 0)
    j = lax.broadcasted_iota(I32, (QT, WBAND), 1)
    bk = _t5_bucket(WINDOW + i - j)
    for h in range(ATT_HEADS):
        o_ref[h] = _bias_lookup(bk, lambda k: rb_ref[k, h])


def _tabc_body(rb_ref, o_ref):
    p0 = pl.program_id(0) * QT
    ns = o_ref.shape[-1]
    t = p0 + lax.broadcasted_iota(I32, (QT, ns), 0)
    n = lax.broadcasted_iota(I32, (QT, ns), 1)
    bk = _t5_bucket(t - (n * CMP_STRIDE + CMP_BLOCK - 1))
    for h in range(ATT_HEADS):
        o_ref[h] = _bias_lookup(bk, lambda k: rb_ref[k, h])


def _bias_tables(rel_bias, T):
    ns = T // CMP_STRIDE
    tabw = pl.pallas_call(
        _tabw_body, in_specs=[_smem_spec()],
        out_shape=jax.ShapeDtypeStruct((ATT_HEADS, QT, WBAND), F32), name="bias_window_table")(rel_bias)
    tabc = pl.pallas_call(
        _tabc_body, grid=(T // QT,), in_specs=[_smem_spec()],
        out_specs=pl.BlockSpec((ATT_HEADS, QT, ns), lambda i: (0, i, 0)),
        out_shape=jax.ShapeDtypeStruct((ATT_HEADS, T, ns), F32),
        compiler_params=_cparams("arbitrary"), name="bias_cmp_table")(rel_bias)
    return tabw, tabc


def _compress_pre(load_rows, wcat_ref, c):
    acc = None
    for r in range(0, CMP_STRIDE, 2):
        lhs = jnp.concatenate([load_rows(r), load_rows(r + 1)], axis=1).astype(BF16)
        part = _dot(lhs, wcat_ref[c, r // 2])
        acc = part if acc is None else acc + part
    return acc


def _compress_finish(hcat, pt, w2):
    pre = hcat[:, :LANES] + jnp.roll(hcat[:, LANES:], -1, axis=0) + pt
    return _dot(_silu(pre).astype(BF16), w2)


def _topn_select(score, n_cols):
    j = lax.broadcasted_iota(I32, score.shape, 1)
    rank = jnp.zeros(score.shape, F32)
    for jp in range(n_cols):
        col = score[:, jp:jp + 1]
        ahead = (col > score) | ((col == score) & (jp < j))
        rank = rank + jnp.where(ahead, 1.0, 0.0)
    return jnp.where((rank < SEL_TOPN) & (score > 0.5 * NEG), 1.0, 0.0)


def _topn_select_rows(score_t, n_rows):
    j = lax.broadcasted_iota(I32, score_t.shape, 0)
    rank = jnp.zeros(score_t.shape, F32)
    for jp in range(n_rows):
        r = score_t[jp:jp + 1, :]
        ahead = (r > score_t) | ((r == score_t) & (jp < j))
        rank = rank + jnp.where(ahead, 1.0, 0.0)
    return jnp.where((rank < SEL_TOPN) & (score_t > 0.5 * NEG), 1.0, 0.0)


def _nsa_prompt_body(rb_ref, q_ref, sm_ref, kv_ref, wcat_ref, w2_ref, pt_ref, tabw_ref, tabc_ref, covert_ref,
                     y_ref, kvp_ref, kc_ref, vc_ref, cmp_ref, nsel_ref, *, T, FT):
    ns = T // CMP_STRIDE
    n_sel = T // SEL_BLOCK
    qt = pl.program_id(1)
    p0 = pl.multiple_of(qt * QT, QT)

    @pl.when(qt == 0)
    def _():
        kvp_ref[0:WINDOW, :] = jnp.zeros((WINDOW, 512), BF16)
        kvp_ref[WINDOW:WINDOW + T, :] = kv_ref[0, :, 256:768].astype(BF16)
        for c, dst in ((0, kc_ref), (1, vc_ref)):
            cmp_ref[c] = kv_ref[0, :, LANES * c:LANES * (c + 1)]
            hcat = _compress_pre(lambda r: cmp_ref[c, pl.ds(r, ns, stride=CMP_STRIDE), :], wcat_ref, c)
            dst[...] = _compress_finish(hcat, pt_ref[c], w2_ref[c]).astype(BF16)

    gates = _sigmoid(sm_ref[0])
    t = p0 + lax.broadcasted_iota(I32, (QT, 1), 0)
    n_i = lax.broadcasted_iota(I32, (1, ns), 1)
    ok_cmp = (t - (n_i * CMP_STRIDE + CMP_BLOCK - 1) >= 0) & (n_i < ns - 1)
    ok_cmp2 = jnp.concatenate([ok_cmp, ok_cmp], axis=0)
    t_l = p0 + lax.broadcasted_iota(I32, (1, QT), 1)
    j_s = lax.broadcasted_iota(I32, (n_sel, 1), 0)
    cur = t_l // SEL_BLOCK
    forced = (j_s == 0) | (j_s == cur) | (j_s == cur - 1)
    avail = j_s * SEL_BLOCK <= t_l
    m_near = p0 - QT + lax.broadcasted_iota(I32, (1, 2 * QT), 1)
    near_blk = (p0 - QT + lax.broadcasted_iota(I32, (n_sel, 2 * QT), 1)) // SEL_BLOCK
    e_near_neg = jnp.where(near_blk == j_s, NEG, 0.0).astype(BF16)
    causal_near = jnp.where((m_near >= 0) & (m_near <= t), 0.0, NEG)
    m_win = p0 - WINDOW + lax.broadcasted_iota(I32, (1, WBAND), 1)
    d_win = t - m_win
    add_win = jnp.where((m_win >= 0) & (d_win >= 0) & (d_win < WINDOW), 0.0, NEG)
    two = lambda x: jnp.concatenate([x, x], axis=0)

    KH = range(ATT_KV_HEADS)
    kcol = lambda kh, base: slice(base + 64 * kh, base + 64 * kh + 64)
    q2 = [(jnp.concatenate([q_ref[0, :, kcol(2 * kh, 0)], q_ref[0, :, kcol(2 * kh + 1, 0)]], axis=0)
           * ATT_SCALE).astype(BF16) for kh in KH]
    s_c = [_dot_nt(q2[kh], kc_ref[:, kcol(kh, 0)]) + jnp.concatenate([tabc_ref[2 * kh], tabc_ref[2 * kh + 1]], axis=0)
           for kh in KH]
    p_c = [_masked_softmax_parts([(s, ok_cmp2)])[0] for s in s_c]
    o_cmp = [_dot(p_c[kh].astype(BF16), vc_ref[:, kcol(kh, 0)]) for kh in KH]
    nsel_t = []
    for kh in KH:
        p3 = _split3(p_c[kh][0:QT] + p_c[kh][QT:2 * QT])
        imp_t = (_dot_nt(covert_ref[...], p3[0]) + _dot_nt(covert_ref[...], p3[1])
                 + _dot_nt(covert_ref[...], p3[2]))
        score_t = jnp.where(avail, imp_t + jnp.where(forced, FORCE, 0.0), NEG)
        nsel = 1.0 - _topn_select_rows(score_t, n_sel)
        nsel_ref[kh] = nsel
        nsel_t.append(nsel.astype(BF16))
    add_near = [_dot_tn(nsel_t[kh], e_near_neg) + causal_near for kh in KH]
    nb = FT // SEL_BLOCK
    e_tile = jnp.where(lax.broadcasted_iota(I32, (nb, FT), 1) // SEL_BLOCK == lax.broadcasted_iota(I32, (nb, FT), 0),
                       NEG, 0.0)
    near_rows = pl.ds(WINDOW + p0 - QT, 2 * QT)
    bias_near = [jnp.concatenate(
        [tabw_ref[2 * kh + g, :, WINDOW - QT:WINDOW + QT] - rb_ref[N_BUCKETS - 1, 2 * kh + g] + add_near[kh]
         for g in range(2)], axis=0) for kh in KH]
    s_n = [_dot_nt(q2[kh], kvp_ref[near_rows, kcol(kh, 0)]) + bias_near[kh] for kh in KH]
    m0 = [s.max(-1, keepdims=True) for s in s_n]
    e_n = [jnp.exp(s - m) for s, m in zip(s_n, m0)]
    init = tuple((m0[kh], e_n[kh].sum(-1, keepdims=True), _dot(e_n[kh].astype(BF16), kvp_ref[near_rows, kcol(kh, 128)]))
                 for kh in KH)

    def far(i, carry):
        k0 = pl.multiple_of(i * FT, FT)
        rows = pl.ds(WINDOW + k0, FT)
        lim = jnp.where(k0 + lax.broadcasted_iota(I32, (1, FT), 1) < p0 - QT, 0.0, NEG)
        blocks = pl.ds(pl.multiple_of(i * nb, nb), nb)
        s = [_dot_nt(q2[kh], kvp_ref[rows, kcol(kh, 0)]) + two(_dot_tn(nsel_ref[kh, blocks, :], e_tile) + lim)
             for kh in KH]
        m_new = [jnp.maximum(carry[kh][0], s[kh].max(-1, keepdims=True)) for kh in KH]
        alpha = [jnp.exp(carry[kh][0] - m_new[kh]) for kh in KH]
        e = [jnp.exp(s[kh] - m_new[kh]) for kh in KH]
        return tuple((m_new[kh], alpha[kh] * carry[kh][1] + e[kh].sum(-1, keepdims=True),
                      alpha[kh] * carry[kh][2] + _dot(e[kh].astype(BF16), kvp_ref[rows, kcol(kh, 128)])) for kh in KH)

    n_far = (jnp.maximum(p0 - QT, 0) + FT - 1) // FT
    fin = lax.fori_loop(0, n_far, far, init)
    o_sel = [fin[kh][2] / jnp.maximum(fin[kh][1], 1e-30) for kh in KH]
    win_rows = pl.ds(p0, WBAND)
    s_w = [_dot_nt(q2[kh], kvp_ref[win_rows, kcol(kh, 256)])
           + jnp.concatenate([tabw_ref[2 * kh] + add_win, tabw_ref[2 * kh + 1] + add_win], axis=0) for kh in KH]
    e_w = [jnp.exp(s - s.max(-1, keepdims=True)) for s in s_w]
    o_win = [_dot(e_w[kh].astype(BF16), kvp_ref[win_rows, kcol(kh, 384)])
             / jnp.maximum(e_w[kh].sum(-1, keepdims=True), 1e-30) for kh in KH]
    for h in range(ATT_HEADS):
        kh, g = divmod(h, 2)
        rows = slice(QT * g, QT * (g + 1))
        out = (gates[:, h:h + 1] * o_cmp[kh][rows] + gates[:, 4 + h:5 + h] * o_sel[kh][rows]
               + gates[:, 8 + h:9 + h] * o_win[kh][rows])
        y_ref[0, :, 64 * h:64 * h + 64] = out.astype(BF16)


def _nsa_prompt(h3, rel_bias, wcat, w2bd, pt, tabw, tabc, covert):
    B, T, _ = h3.shape
    ns = T // CMP_STRIDE
    n_sel = T // SEL_BLOCK
    ft = min(512, T)
    return pl.pallas_call(
        functools.partial(_nsa_prompt_body, T=T, FT=ft),
        grid=(B, T // QT),
        in_specs=[_smem_spec(),
                  pl.BlockSpec((1, QT, 256), lambda b, i: (b, i, C_Q // 256)),
                  pl.BlockSpec((1, QT, LANES), lambda b, i: (b, i, C_SM // LANES)),
                  pl.BlockSpec((1, T, 768), lambda b, i: (b, 0, C_KV // 768)),
                  _const_spec((2, CMP_STRIDE // 2, 2 * LANES, 2 * LANES)), _const_spec((2, LANES, LANES)),
                  _const_spec((2, 1, LANES)), _const_spec((ATT_HEADS, QT, WBAND)),
                  pl.BlockSpec((ATT_HEADS, QT, ns), lambda b, i: (0, i, 0)),
                  _const_spec((n_sel, ns))],
        out_specs=pl.BlockSpec((1, QT, 256), lambda b, i: (b, i, 0)),
        out_shape=jax.ShapeDtypeStruct((B, T, 256), BF16),
        scratch_shapes=[pltpu.VMEM((WINDOW + T, 512), BF16), pltpu.VMEM((ns, LANES), BF16),
                        pltpu.VMEM((ns, LANES), BF16), pltpu.VMEM((2, T, LANES), F32),
                        pltpu.VMEM((ATT_KV_HEADS, n_sel, QT), F32)],
        compiler_params=_cparams("arbitrary", "arbitrary"),
        name="nsa_prompt",
    )(rel_bias, h3, h3, h3, wcat, w2bd, pt, tabw, tabc, covert)


def _nsa_s1_body(pt_ref, cache_ref, wcat_ref, o_ref, buf_ref, row_ref, sem, *, layer, CH):
    s = pl.program_id(0)
    nsteps = pl.num_programs(0)
    slot = s % 2

    def page_copy(step, p, sl):
        phys = pt_ref[step * CH + p]
        return pltpu.make_async_copy(cache_ref.at[layer, phys, pl.ds(0, 2)], buf_ref.at[sl, p], sem.at[sl])

    def issue(step, sl):
        def one(p, carry):
            page_copy(step, p, sl).start()
            return carry
        lax.fori_loop(0, CH, one, 0)

    @pl.when(s == 0)
    def _():
        issue(0, 0)

    @pl.when(s + 1 < nsteps)
    def _():
        issue(s + 1, 1 - slot)

    def wait_one(p, carry):
        page_copy(s, p, slot).wait()
        return carry
    lax.fori_loop(0, CH, wait_one, 0)

    gp = 16 if CH % 16 == 0 else CH
    gsub = gp * LANES // CMP_STRIDE
    for g in range(CH // gp):
        for p in range(g * gp, (g + 1) * gp):
            for c in range(2):
                row_ref[c, LANES * p:LANES * (p + 1), :] = buf_ref[slot, p, c].reshape(2 * HEAD_DIM, LANES).T
        for c in range(2):
            hcat = _compress_pre(lambda r: row_ref[c, pl.ds(g * gp * LANES + r, gsub, stride=CMP_STRIDE), :], wcat_ref, c)
            o_ref[0, g * gsub:(g + 1) * gsub, 2 * LANES * c:2 * LANES * (c + 1)] = hcat


def _nsa_s1(page_flat, cache_t, wcat, layer, B, NP):
    CH = min(64, NP)
    nsub = CH * LANES // CMP_STRIDE
    per_b = NP // CH
    grid_spec = pltpu.PrefetchScalarGridSpec(
        num_scalar_prefetch=1,
        grid=(B * per_b,),
        in_specs=[pl.BlockSpec(memory_space=pl.ANY),
                  pl.BlockSpec((2, CMP_STRIDE // 2, 2 * LANES, 2 * LANES), lambda s, pt: (0, 0, 0, 0),
                               pipeline_mode=pl.Buffered(1))],
        out_specs=pl.BlockSpec((1, nsub, 4 * LANES), lambda s, pt: (s // per_b, s % per_b, 0)),
        scratch_shapes=[pltpu.VMEM((2, CH, 2, 2, HEAD_DIM, LANES), F32), pltpu.VMEM((2, CH * LANES, LANES), F32),
                        pltpu.SemaphoreType.DMA((2,))],
    )
    return pl.pallas_call(
        functools.partial(_nsa_s1_body, layer=layer, CH=CH),
        grid_spec=grid_spec,
        out_shape=jax.ShapeDtypeStruct((B, NP * LANES // CMP_STRIDE, 4 * LANES), F32),
        compiler_params=_cparams("arbitrary"),
        name="nsa_sample_compress",
    )(page_flat, cache_t, wcat)


def _row_bias(bk, rb_ref, kh, g_of_row):
    return _bias_lookup(bk, lambda k: jnp.where(g_of_row == 1, rb_ref[k, 2 * kh + 1], rb_ref[k, 2 * kh]))


MAX_SEL_PAGES = 64
SLOT_INFO = 5


def _nsa_s2_body(rb_ref, hc_ref, q_ref, w2_ref, pt_ref, cover_ref, pair_ref, ut_ref, ocmp_ref, slots_ref,
                 *, P, t_valid, NSELP):
    nsub = hc_ref.shape[1]
    kc = _compress_finish(hc_ref[0, :, 0:2 * LANES], pt_ref[0], w2_ref[0]).astype(BF16)
    vc = _compress_finish(hc_ref[0, :, 2 * LANES:4 * LANES], pt_ref[1], w2_ref[1]).astype(BF16)
    n_sel = P // SEL_BLOCK + 1
    row = lax.broadcasted_iota(I32, (16, 1), 0)
    t16 = row % 8
    g16 = row // 8
    n_i = lax.broadcasted_iota(I32, (1, nsub), 1)
    d_cmp = (P + t16) - (n_i * CMP_STRIDE + CMP_BLOCK - 1)
    ok_cmp = d_cmp >= 0
    bk_cmp = _t5_bucket(d_cmp)
    t8 = lax.broadcasted_iota(I32, (8, 1), 0)
    j_i = lax.broadcasted_iota(I32, (1, NSELP), 1)
    qpos = P + t8
    cur = qpos // SEL_BLOCK
    forced = (j_i == 0) | (j_i == cur) | (j_i == cur - 1)
    avail = (j_i * SEL_BLOCK <= qpos) & (j_i < n_sel)
    wts = jnp.where(t8 < t_valid, jnp.left_shift(1, t8 + 4 * (j_i % 2)), 0).astype(F32)
    NP = P // LANES
    page_l = lax.broadcasted_iota(I32, (1, 2 * LANES), 1)
    slot_s = lax.broadcasted_iota(I32, (MAX_SEL_PAGES, 1), 0).astype(F32)
    col = lax.broadcasted_iota(I32, (1, LANES), 1)
    pick = lambda x, lane_no: jnp.sum(jnp.where(page_l == lane_no, x, 0.0), axis=1, keepdims=True)
    scores = []
    for kh in range(ATT_KV_HEADS):
        q16 = jnp.concatenate([q_ref[0, :, 128 * kh:128 * kh + 64], q_ref[0, :, 128 * kh + 64:128 * kh + 128]],
                              axis=0).astype(BF16)
        s = _dot_nt(q16, kc[:, 64 * kh:64 * kh + 64]) * ATT_SCALE + _row_bias(bk_cmp, rb_ref, kh, g16)
        (p,) = _masked_softmax_parts([(s, ok_cmp)])
        o = _dot(p.astype(BF16), vc[:, 64 * kh:64 * kh + 64])
        ocmp_ref[0, :, 128 * kh:128 * kh + 64] = o[0:8]
        ocmp_ref[0, :, 128 * kh + 64:128 * kh + 128] = o[8:16]
        p3 = _split3(p[0:8] + p[8:16])
        imp = _dot(p3[0], cover_ref[...]) + _dot(p3[1], cover_ref[...]) + _dot(p3[2], cover_ref[...])
        scores.append(jnp.where(avail, imp + jnp.where(forced, FORCE, 0.0), NEG))
    sel_all = _topn_select(jnp.concatenate(scores, axis=0), n_sel)
    for kh in range(ATT_KV_HEADS):
        sel = sel_all[8 * kh:8 * kh + 8]
        colsum = jnp.sum(sel * wts, axis=0, keepdims=True)
        urow = _dot(jnp.broadcast_to(colsum, (8, NSELP)).astype(BF16), pair_ref[...])[0:1]
        nz = jnp.where((urow > 0.5) & (page_l < NP - 1), 1.0, 0.0)
        before = _dot(jnp.broadcast_to(nz, (8, 2 * LANES)).astype(BF16), ut_ref[...])[0:1]
        in_slot = (nz > 0.5) & (before == slot_s)
        page_of = jnp.sum(jnp.where(in_slot, page_l.astype(F32), 0.0), axis=1, keepdims=True)
        bits_of = jnp.sum(jnp.where(in_slot, urow, 0.0), axis=1, keepdims=True)
        count = jnp.sum(nz, axis=1, keepdims=True)
        info = jnp.where(col == 0, page_of, jnp.where(col == 1, bits_of, jnp.where(
            col == 2, count, jnp.where(col == 3, pick(urow, NP - 1), jnp.where(col == 4, pick(urow, NP), 0.0)))))
        slots_ref[0, MAX_SEL_PAGES * kh:MAX_SEL_PAGES * (kh + 1), :] = info.astype(I32)


def _nsa_s2(hs3, hc, rel_bias, w2bd, pt, cover, pair, P, t_valid):
    B, T, _ = hs3.shape
    nsub = hc.shape[1]
    nselp = cover.shape[1]
    ut = (jnp.arange(2 * LANES)[:, None] < jnp.arange(2 * LANES)[None, :]).astype(BF16)
    return pl.pallas_call(
        functools.partial(_nsa_s2_body, P=P, t_valid=t_valid, NSELP=nselp),
        grid=(B,),
        in_specs=[_smem_spec(),
                  pl.BlockSpec((1, nsub, 4 * LANES), lambda b: (b, 0, 0)),
                  pl.BlockSpec((1, T, 256), lambda b: (b, 0, C_Q // 256)),
                  _const_spec((2, LANES, LANES)), _const_spec((2, 1, LANES)),
                  _const_spec((nsub, nselp)), _const_spec((nselp, 2 * LANES)), _const_spec((2 * LANES, 2 * LANES))],
        out_specs=[pl.BlockSpec((1, 8, 256), lambda b: (b, 0, 0)),
                   pl.BlockSpec((1, ATT_KV_HEADS * MAX_SEL_PAGES, LANES), lambda b: (b, 0, 0))],
        out_shape=[jax.ShapeDtypeStruct((B, 8, 256), F32),
                   jax.ShapeDtypeStruct((B, ATT_KV_HEADS * MAX_SEL_PAGES, LANES), I32)],
        compiler_params=_cparams("arbitrary"),
        name="nsa_sample_cmp_select",
    )(rel_bias, hc, hs3, w2bd, pt, cover, pair, ut)


def _nsa_s3_body(pt_ref, sl_ref, rb_ref, q_ref, sm_ref, kvn_ref, ocmp_ref, cache_ref, win_ref, y_ref,
                 kcat_ref, vcat_ref, rec_ref, sem, *, layer, P, NP, t_valid):
    b = pl.program_id(0)

    def slot_info(kh, i, field):
        return sl_ref[((b * ATT_KV_HEADS + kh) * MAX_SEL_PAGES + i) * SLOT_INFO + field]

    wb = win_ref.shape[-1]
    ncat = MAX_SEL_PAGES * LANES
    row = lax.broadcasted_iota(I32, (16, 1), 0)
    t16 = row % 8
    g16 = row // 8
    valid_row = t16 < t_valid
    gates = _sigmoid(sm_ref[0])

    @pl.when(b == 0)
    def _():
        kcat_ref[...] = jnp.zeros(kcat_ref.shape, F32)
        vcat_ref[...] = jnp.zeros(vcat_ref.shape, F32)

    rec_copy = pltpu.make_async_copy(cache_ref.at[layer, pt_ref[b * NP + NP - 1], pl.ds(2, 2)], rec_ref, sem.at[1])
    rec_copy.start()

    def kv_copies(kh, phys, i):
        dst = pl.ds(pl.multiple_of(i * LANES, LANES), LANES)
        return (pltpu.make_async_copy(cache_ref.at[layer, phys, 2, kh], kcat_ref.at[kh, :, dst], sem.at[0]),
                pltpu.make_async_copy(cache_ref.at[layer, phys, 3, kh], vcat_ref.at[kh, :, dst], sem.at[0]))

    counts = [slot_info(kh, 0, 2) for kh in range(ATT_KV_HEADS)]
    for kh in range(ATT_KV_HEADS):
        def issue(i, carry, kh=kh):
            ck, cv = kv_copies(kh, pt_ref[b * NP + slot_info(kh, i, 0)], i)
            ck.start()
            cv.start()
            return carry
        lax.fori_loop(0, counts[kh], issue, 0)

    q16fs = [jnp.concatenate([q_ref[0, :, 128 * kh:128 * kh + 64], q_ref[0, :, 128 * kh + 64:128 * kh + 128]], axis=0)
             for kh in range(ATT_KV_HEADS)]

    o_wins = []
    for kh in range(ATT_KV_HEADS):
        q16f = q16fs[kh]
        q16 = q16f.astype(BF16)
        col_w = lax.broadcasted_iota(I32, (1, wb), 1)
        d_w = wb + t16 - col_w
        ok_w = (d_w < WINDOW) & valid_row
        s_w = _dot(q16, win_ref[0, 0, kh].astype(BF16)) * ATT_SCALE + _row_bias(_t5_bucket(d_w), rb_ref, kh, g16)
        s_w = jnp.where(ok_w, s_w, NEG)
        mw = s_w.max(-1, keepdims=True)
        cols = []
        for c in range(t_valid):
            kn = kvn_ref[0, c:c + 1, 512 + 64 * kh:512 + 64 * kh + 64]
            sc = jnp.sum(q16f * kn, axis=-1, keepdims=True) * ATT_SCALE + _row_bias(_t5_bucket(t16 - c), rb_ref, kh, g16)
            ok = (t16 >= c) & valid_row
            sc = jnp.where(ok, sc, NEG)
            cols.append((sc, ok))
            mw = jnp.maximum(mw, sc)
        e_w = jnp.where(ok_w, jnp.exp(s_w - mw), 0.0)
        l_w = e_w.sum(-1, keepdims=True)
        acc_w = _dot_nt(e_w.astype(BF16), win_ref[0, 1, kh].astype(BF16))
        for c, (sc, ok) in enumerate(cols):
            e = jnp.where(ok, jnp.exp(sc - mw), 0.0)
            l_w = l_w + e
            acc_w = acc_w + e * kvn_ref[0, c:c + 1, 640 + 64 * kh:640 + 64 * kh + 64]
        o_wins.append(acc_w / jnp.maximum(l_w, 1e-30))

    for kh in range(ATT_KV_HEADS):
        def wait_pair(i, carry, kh=kh):
            ck, cv = kv_copies(kh, 0, i)
            ck.wait()
            cv.wait()
            return carry
        lax.fori_loop(0, counts[kh], wait_pair, 0)
    rec_copy.wait()

    lane_c = lax.broadcasted_iota(I32, (1, ncat), 1)
    shift_c = t16 + 4 * ((lane_c % LANES) // SEL_BLOCK)
    lane = lax.broadcasted_iota(I32, (1, LANES), 1)
    shift = t16 + 4 * (lane // SEL_BLOCK)
    for kh in range(ATT_KV_HEADS):
        q16f = q16fs[kh]
        q16 = q16f.astype(BF16)
        c_far = jnp.where(g16 == 1, rb_ref[N_BUCKETS - 1, 2 * kh + 1], rb_ref[N_BUCKETS - 1, 2 * kh])

        u_vec = jnp.concatenate([jnp.full((1, LANES), slot_info(kh, i, 1), I32) for i in range(MAX_SEL_PAGES)], axis=1)
        ok_c = (jnp.bitwise_and(jnp.right_shift(u_vec, shift_c), 1) == 1) & valid_row
        s_c = jnp.where(ok_c, _dot(q16, kcat_ref[kh].astype(BF16)) * ATT_SCALE + c_far, NEG)
        u_rec = slot_info(kh, 0, 3)
        d_r = (P + t16) - ((NP - 1) * LANES + lane)
        ok_r = (jnp.bitwise_and(jnp.right_shift(u_rec, shift), 1) == 1) & valid_row
        s_r = _dot(q16, rec_ref[0, kh].astype(BF16)) * ATT_SCALE + _row_bias(_t5_bucket(d_r), rb_ref, kh, g16)
        s_r = jnp.where(ok_r, s_r, NEG)
        u_new = slot_info(kh, 0, 4)
        sel_new = jnp.bitwise_and(jnp.right_shift(u_new, t16), 1) == 1
        m_fin = jnp.maximum(s_c.max(-1, keepdims=True), s_r.max(-1, keepdims=True))
        cols = []
        for c in range(t_valid):
            kn = kvn_ref[0, c:c + 1, 256 + 64 * kh:256 + 64 * kh + 64]
            sc = jnp.sum(q16f * kn, axis=-1, keepdims=True) * ATT_SCALE + _row_bias(_t5_bucket(t16 - c), rb_ref, kh, g16)
            ok = sel_new & (t16 >= c) & valid_row
            sc = jnp.where(ok, sc, NEG)
            cols.append((sc, ok))
            m_fin = jnp.maximum(m_fin, sc)
        e_c = jnp.where(ok_c, jnp.exp(s_c - m_fin), 0.0)
        e_r = jnp.where(ok_r, jnp.exp(s_r - m_fin), 0.0)
        l_fin = e_c.sum(-1, keepdims=True) + e_r.sum(-1, keepdims=True)
        acc = (_dot_nt(e_c.astype(BF16), vcat_ref[kh].astype(BF16))
               + _dot_nt(e_r.astype(BF16), rec_ref[1, kh].astype(BF16)))
        for c, (sc, ok) in enumerate(cols):
            e = jnp.where(ok, jnp.exp(sc - m_fin), 0.0)
            l_fin = l_fin + e
            acc = acc + e * kvn_ref[0, c:c + 1, 384 + 64 * kh:384 + 64 * kh + 64]
        o_sel = acc / jnp.maximum(l_fin, 1e-30)
        o_win = o_wins[kh]

        for g in range(2):
            h = 2 * kh + g
            out = (gates[:, h:h + 1] * ocmp_ref[0, :, 64 * h:64 * h + 64]
                   + gates[:, 4 + h:5 + h] * o_sel[8 * g:8 * g + 8]
                   + gates[:, 8 + h:9 + h] * o_win[8 * g:8 * g + 8])
            y_ref[0, :, 64 * h:64 * h + 64] = out.astype(BF16)


def _nsa_s3(page_flat, slots_flat, rel_bias, hs3, ocmp, cache_t, win_t, layer, P, NP, t_valid):
    B, T, _ = hs3.shape
    wb = win_t.shape[-1]
    grid_spec = pltpu.PrefetchScalarGridSpec(
        num_scalar_prefetch=2,
        grid=(B,),
        in_specs=[_smem_spec(),
                  pl.BlockSpec((1, T, 256), lambda b, *_: (b, 0, C_Q // 256)),
                  pl.BlockSpec((1, T, LANES), lambda b, *_: (b, 0, C_SM // LANES)),
                  pl.BlockSpec((1, T, 768), lambda b, *_: (b, 0, C_KV // 768)),
                  pl.BlockSpec((1, 8, 256), lambda b, *_: (b, 0, 0)),
                  pl.BlockSpec(memory_space=pl.ANY),
                  pl.BlockSpec((1, 2, 2, HEAD_DIM, wb), lambda b, *_: (b, 0, 0, 0, 0))],
        out_specs=pl.BlockSpec((1, T, 256), lambda b, *_: (b, 0, 0)),
        scratch_shapes=[pltpu.VMEM((2, HEAD_DIM, MAX_SEL_PAGES * LANES), F32),
                        pltpu.VMEM((2, HEAD_DIM, MAX_SEL_PAGES * LANES), F32),
                        pltpu.VMEM((2, 2, HEAD_DIM, LANES), F32),
                        pltpu.SemaphoreType.DMA((2,))],
    )
    return pl.pallas_call(
        functools.partial(_nsa_s3_body, layer=layer, P=P, NP=NP, t_valid=t_valid),
        grid_spec=grid_spec,
        out_shape=jax.ShapeDtypeStruct((B, T, 256), BF16),
        compiler_params=_cparams("arbitrary"),
        name="nsa_sample_select_window",
    )(page_flat, slots_flat, rel_bias, hs3, hs3, hs3, ocmp, cache_t, win_t)


def _per_layer_specs(depth, block, index_of_step):
    def spec(k):
        def index_map(l, i):
            idx = index_of_step(i)
            return tuple(jnp.where(l == k, v, 0) if n < 2 else v for n, v in enumerate(idx))
        return pl.BlockSpec(block, index_map)
    return [spec(k) for k in range(depth)]


def _kv_pages_body(*refs):
    o_ref = refs[-1]
    for k, h_ref in enumerate(refs[:-1]):
        @pl.when(pl.program_id(0) == k)
        def _(h_ref=h_ref):
            for j in range(o_ref.shape[1]):
                for c in range(4):
                    x = h_ref[0, LANES * j:LANES * (j + 1), LANES * c:LANES * (c + 1)]
                    o_ref[0, j, c] = x.T.reshape(ATT_KV_HEADS, HEAD_DIM, LANES)


def _kv_pages_t(hs):
    depth = len(hs)
    B, T, _ = hs[0].shape
    rows = min(4 * LANES, T)
    per_b = T // rows
    return pl.pallas_call(
        _kv_pages_body,
        grid=(depth, B * per_b),
        in_specs=_per_layer_specs(depth, (1, rows, 512), lambda i: (i // per_b, i % per_b, 0)),
        out_specs=pl.BlockSpec((1, rows // LANES, 4, ATT_KV_HEADS, HEAD_DIM, LANES), lambda l, i: (l, i, 0, 0, 0, 0)),
        out_shape=jax.ShapeDtypeStruct((depth, B * T // LANES, 4, ATT_KV_HEADS, HEAD_DIM, LANES), F32),
        compiler_params=_cparams("arbitrary", "arbitrary"),
        name="kv_pages_token_minor",
    )(*hs)


def _win_rows_body(*refs):
    o_ref = refs[-1]
    for k, h_ref in enumerate(refs[:-1]):
        @pl.when(pl.program_id(0) == k)
        def _(h_ref=h_ref):
            for c in range(2):
                x = h_ref[0, :, LANES * c:LANES * (c + 1)]
                o_ref[0, 0, c] = x.T.reshape(ATT_KV_HEADS, HEAD_DIM, x.shape[0])


def _win_rows_t(hs, wrows):
    depth = len(hs)
    B, T, _ = hs[0].shape
    last = T // wrows - 1
    return pl.pallas_call(
        _win_rows_body,
        grid=(depth, B),
        in_specs=_per_layer_specs(depth, (1, wrows, 256), lambda b: (b, last, (C_KV + 512) // 256)),
        out_specs=pl.BlockSpec((1, 1, 2, ATT_KV_HEADS, HEAD_DIM, wrows), lambda l, b: (l, b, 0, 0, 0, 0)),
        out_shape=jax.ShapeDtypeStruct((depth, B, 2, ATT_KV_HEADS, HEAD_DIM, wrows), F32),
        compiler_params=_cparams("arbitrary", "arbitrary"),
        name="win_rows_token_minor",
    )(*hs)


def _blockdiag2(w):
    z = jnp.zeros_like(w)
    return jnp.concatenate([jnp.concatenate([w, z], axis=-1), jnp.concatenate([z, w], axis=-1)], axis=-2)


def _prep_layer(w_in, pool_w, cmp_w1, cmp_w2, alog, dtb):
    glu, pool, q, kv, gate, gqkv, z, a, b = jnp.split(w_in, [512, 768, 1024, 1792, 1804, 2572, 2828, 2832], axis=1)
    pad = jnp.zeros((D_MODEL, IN_PAD - C_SM - 20), F32)
    w_in_p = jnp.concatenate([kv, gqkv, glu, pool, q, z, gate, a, b, pad], axis=1).astype(BF16)
    wblk = jnp.zeros((POOL_CH, POOL_CH), F32)
    for gi in range(4):
        wblk = wblk.at[64 * gi:64 * gi + 64, 64 * gi:64 * gi + 64].set(pool_w[gi])
    w1 = cmp_w1.reshape(2, CMP_BLOCK, HEAD_DIM, HEAD_DIM)
    wcat = jnp.concatenate([_blockdiag2(w1[:, :CMP_STRIDE]), _blockdiag2(w1[:, CMP_STRIDE:])], axis=-1).astype(BF16)
    wcat = wcat.reshape(2, CMP_STRIDE // 2, 2 * LANES, 2 * LANES)
    w2bd = _blockdiag2(cmp_w2).astype(BF16)
    lane_pad = lambda v: jnp.zeros((1, LANES), F32).at[0, SM_A:SM_A + GDN_HEADS].set(v)
    return w_in_p, wblk.astype(BF16), wcat, w2bd, lane_pad(alog), lane_pad(dtb)


def _cover_matrix(nsub, n_sel, cols):
    n = jnp.arange(nsub)[:, None] * CMP_STRIDE
    j = jnp.arange(cols)[None, :] * SEL_BLOCK
    cov = (n < j + SEL_BLOCK) & (n + CMP_BLOCK > j) & (jnp.arange(cols)[None, :] < n_sel) & (jnp.arange(nsub)[:, None] < nsub - 1)
    return cov.astype(BF16)


def kernel(x_prompt, x_sample, cache_nsa_kv, cache_win_kv, state_conv, state_pool, state_gdn_conv, state_gdn,
           page_table, w_in, conv_dw, conv_dw_b, conv_ln_g, conv_ln_b, conv_pw, pool_w, pool_scale,
           cmp_pe, cmp_w1, cmp_w2, gdn_conv_w, gdn_a_log, gdn_dt_bias, gdn_norm_g,
           w_out, ln1_g, ln1_b, w_up, w_down, ln2_g, ln2_b, rel_bias):
    depth = w_in.shape[0]
    BP, T, _ = x_prompt.shape
    BS, TS, _ = x_sample.shape
    NP = page_table.shape[1]
    page = cache_nsa_kv.shape[2]
    P = NP * page
    TSP = 8
    assert page == LANES and TS <= TSP and TS < CMP_STRIDE and T % QT == 0 and P % SEL_BLOCK == 0

    cache_t = jnp.transpose(cache_nsa_kv, (0, 1, 3, 4, 5, 2))
    win_t = jnp.transpose(cache_win_kv, (0, 1, 3, 4, 5, 2))
    page_flat = page_table.reshape(-1)

    tabw, tabc = _bias_tables(rel_bias, T)
    pterm = _peterm(cmp_pe.reshape(depth * 2, 1, CMP_BLOCK * HEAD_DIM), cmp_w1.reshape(depth * 2, CMP_BLOCK * HEAD_DIM, HEAD_DIM))
    pterm = pterm[:, 0:1, :].reshape(depth, 2, 1, HEAD_DIM)
    pterm = jnp.concatenate([pterm, pterm], axis=-1)

    ns_p, nsel_p = T // CMP_STRIDE, T // SEL_BLOCK
    covert_p = _cover_matrix(ns_p, nsel_p, nsel_p).T
    ns_s = P // CMP_STRIDE
    nsel_s = P // SEL_BLOCK + 1
    nselp_s = -(-nsel_s // LANES) * LANES
    cover_s = _cover_matrix(ns_s + 1, nsel_s, nselp_s)[:ns_s]
    jj = jnp.arange(nselp_s)[:, None]
    pp = jnp.arange(2 * LANES)[None, :]
    pair = (((jj // 2 == pp) & (jj < 2 * NP)) | ((jj == 2 * NP) & (pp == NP))).astype(BF16)

    yp = x_prompt.reshape(BP * T, D_MODEL)
    ys = jnp.pad(x_sample, ((0, 0), (0, TSP - TS), (0, 0))).reshape(BS * TSP, D_MODEL)
    zeros = lambda *s: jnp.zeros(s, F32)
    outs_p = [[] for _ in range(6)]
    outs_s = [[] for _ in range(6)]
    hps = []
    preps = [_prep_layer(w_in[l], pool_w[l], cmp_w1[l], cmp_w2[l], gdn_a_log[l], gdn_dt_bias[l]) for l in range(depth)]
    hcs = [_nsa_s1(page_flat, cache_t, preps[l][2], l, BS, NP) for l in range(depth)]
    for l in range(depth):
        w_in_p, wblk, wcat, w2bd, alog_l, dtb_l = preps[l]
        lw = dict(w_out=w_out[l].astype(BF16), ln1_g=ln1_g[l][None], ln1_b=ln1_b[l][None],
                  w_up=w_up[l].astype(BF16), w_down=w_down[l].astype(BF16), ln2_g=ln2_g[l][None], ln2_b=ln2_b[l][None])
        conv_w = (conv_dw[l], conv_dw_b[l][None], conv_ln_g[l][None], conv_ln_b[l][None], conv_pw[l].astype(BF16))
        gdn_w = (gdn_conv_w[l], alog_l, dtb_l, gdn_norm_g[l][None])

        hp = _proj_in(yp, w_in_p, 512).reshape(BP, T, IN_PAD)
        m_conv, conv_new = _conv_mixer(hp, zeros(BP, CONV_WIDTH - 1, CONV_CH), *conv_w, t_valid=T)
        m_pool, pool_new = _pool_mixer(hp, zeros(BP, POOL_BUF, POOL_CH), wblk, pool_scale[l][None], t_valid=T, offset=0)
        m_att = _nsa_prompt(hp, rel_bias, wcat, w2bd, pterm[l], tabw, tabc, covert_p)
        m_gdn, gbuf_new, s_new = _gdn_mixer(hp, zeros(BP, GDN_CONV - 1, GDN_QKV), zeros(BP, GDN_HEADS, GDN_DK, GDN_DV),
                                            *gdn_w, t_valid=T)
        mixes = [m.reshape(BP * T, GROUP_WIDTH) for m in (m_conv, m_pool, m_att, m_gdn)]
        yp = _out_ffn(yp, mixes, lw, 512)
        hps.append(hp)
        for lst, arr in zip(outs_p[2:], (conv_new, pool_new, gbuf_new, s_new)):
            lst.append(arr)

        hs = _proj_in(ys, w_in_p, BS * TSP).reshape(BS, TSP, IN_PAD)
        m_conv, conv_new = _conv_mixer(hs, state_conv[l], *conv_w, t_valid=TS)
        m_pool, pool_new = _pool_mixer(hs, state_pool[l], wblk, pool_scale[l][None], t_valid=TS, offset=P)
        ocmp, slots = _nsa_s2(hs, hcs[l], rel_bias, w2bd, pterm[l], cover_s, pair, P, TS)
        m_att = _nsa_s3(page_flat, slots[:, :, :SLOT_INFO].reshape(-1), rel_bias, hs, ocmp, cache_t, win_t[l], l, P, NP, TS)
        m_gdn, gbuf_new, s_new = _gdn_mixer(hs, state_gdn_conv[l], state_gdn[l], *gdn_w, t_valid=TS)
        mixes = [m.reshape(BS * TSP, GROUP_WIDTH) for m in (m_conv, m_pool, m_att, m_gdn)]
        ys = _out_ffn(ys, mixes, lw, BS * TSP)
        kv_new = hs[:, :TS, C_KV:C_KV + 768].reshape(BS, TS, 6, ATT_KV_HEADS, HEAD_DIM)
        outs_s[0].append(kv_new[:, :, :4])
        kw_all = jnp.concatenate([cache_win_kv[l], kv_new[:, :, 4:]], axis=1)
        outs_s[1].append(kw_all[:, -min(WINDOW, kw_all.shape[1]):])
        for lst, arr in zip(outs_s[2:], (conv_new, pool_new, gbuf_new, s_new)):
            lst.append(arr)

    p_nsa = jnp.transpose(_kv_pages_t(hps), (0, 1, 5, 2, 3, 4))
    p_win = jnp.transpose(_win_rows_t(hps, min(WINDOW, T)), (0, 1, 5, 2, 3, 4))
    p_rest = [p_win] + [jnp.stack(a) for a in outs_p[2:]]
    s_all = [jnp.stack(a) for a in outs_s]
    y_s = ys.reshape(BS, TSP, D_MODEL)[:, :TS]
    return (yp.reshape(BP, T, D_MODEL), y_s, p_nsa, *p_rest, *s_all)
```

```python
import functools
import math

import jax
import jax.numpy as jnp
from jax import lax
from jax.experimental import pallas as pl
from jax.experimental.pallas import tpu as pltpu

F32 = jnp.float32
BF16 = jnp.bfloat16
I32 = jnp.int32

D_MODEL = 1024
GROUP_WIDTH = 256
CONV_CH = 256
CONV_WIDTH = 31
POOL_CH = 256
POOL_WINDOWS = (2, 4, 8, 16)
POOL_BUF = 15
ATT_HEADS = 4
ATT_KV_HEADS = 2
HEAD_DIM = 64
ATT_SCALE = HEAD_DIM ** -0.5
CMP_STRIDE = 16
CMP_BLOCK = 32
SEL_BLOCK = 64
SEL_TOPN = 16
WINDOW = 512
N_BUCKETS = 32
GDN_HEADS = 4
GDN_DK = 64
GDN_DV = 64
GDN_QKV = 768
GDN_CONV = 4
GDN_CHUNK = 64
D_FF = 4096
DEPTH = 2
DN_ALPHA = (2 * DEPTH) ** 0.25
LN_EPS = 1e-5
NEG = -1e30
FORCE = 1e4

LANES = 128
VMEM_LIMIT_BYTES = 56 * 1024 * 1024

C_KV, C_GQKV, C_GLU, C_POOL, C_Q, C_Z, C_SM = 0, 768, 1536, 2048, 2304, 2560, 2816
IN_PAD = 2944
SM_GATE, SM_A, SM_B = 0, 12, 16


def _cparams(*sem):
    return pltpu.CompilerParams(dimension_semantics=sem, vmem_limit_bytes=VMEM_LIMIT_BYTES)


def _const_spec(shape):
    nd = len(shape)
    return pl.BlockSpec(shape, lambda *_: (0,) * nd, pipeline_mode=pl.Buffered(1))


def _smem_spec():
    return pl.BlockSpec(memory_space=pltpu.SMEM)


def _sigmoid(x):
    return 0.5 * jnp.tanh(0.5 * x) + 0.5


def _silu(x):
    h = 0.5 * x
    return h * jnp.tanh(h) + h


def _layer_norm(y, g, b):
    mu = jnp.mean(y, axis=-1, keepdims=True)
    yc = y - mu
    var = jnp.mean(yc * yc, axis=-1, keepdims=True)
    return yc * lax.rsqrt(var + LN_EPS) * g + b


def _dot(a, b):
    return jnp.dot(a, b, preferred_element_type=F32)


def _dot_nt(a, b):
    return lax.dot_general(a, b, (((1,), (1,)), ((), ())), preferred_element_type=F32)


def _dot_tn(a, b):
    return lax.dot_general(a, b, (((0,), (0,)), ((), ())), preferred_element_type=F32)


def _split3(x):
    x1 = x.astype(BF16)
    r = x - x1.astype(F32)
    x2 = r.astype(BF16)
    x3 = (r - x2.astype(F32)).astype(BF16)
    return x1, x2, x3


def _t5_bucket(d):
    d = jnp.maximum(d, 0)
    logd = jnp.log(jnp.maximum(d, 1).astype(F32) / 16.0) / math.log(8.0)
    large = jnp.minimum(16 + (logd * 16.0).astype(I32), N_BUCKETS - 1)
    return jnp.where(d < 16, d, large)


def _bias_lookup(bk, value_of_bucket):
    out = jnp.zeros(bk.shape, F32)
    for k in range(N_BUCKETS):
        out = jnp.where(bk == k, value_of_bucket(k), out)
    return out


def _masked_softmax_parts(parts):
    ss = [jnp.where(ok, s, NEG) for s, ok in parts]
    mx = ss[0].max(-1, keepdims=True)
    for s in ss[1:]:
        mx = jnp.maximum(mx, s.max(-1, keepdims=True))
    es = [jnp.where(ok, jnp.exp(s - mx), 0.0) for s, (_, ok) in zip(ss, parts)]
    tot = es[0].sum(-1, keepdims=True)
    for e in es[1:]:
        tot = tot + e.sum(-1, keepdims=True)
    inv = 1.0 / jnp.maximum(tot, 1e-30)
    return [e * inv for e in es]


def _proj_in_body(x_ref, w_ref, o_ref):
    xb = x_ref[...].astype(BF16)
    for a in range(0, IN_PAD, 512):
        b = min(a + 512, IN_PAD)
        o_ref[:, a:b] = _dot(xb, w_ref[:, a:b])


def _proj_in(x, w, tm):
    n = x.shape[0]
    return pl.pallas_call(
        _proj_in_body,
        grid=(n // tm,),
        in_specs=[pl.BlockSpec((tm, D_MODEL), lambda i: (i, 0)), _const_spec((D_MODEL, IN_PAD))],
        out_specs=pl.BlockSpec((tm, IN_PAD), lambda i: (i, 0)),
        out_shape=jax.ShapeDtypeStruct((n, IN_PAD), F32),
        compiler_params=_cparams("arbitrary"),
        name="proj_in",
    )(x, w)


FF_CHUNK = 1024


def _out_ffn_body(x_ref, m0_ref, m1_ref, m2_ref, m3_ref, wo_ref, g1_ref, b1_ref, wu_ref, wd_ref, g2_ref, b2_ref, o_ref):
    acc = _dot(m0_ref[...], wo_ref[0:256, :])
    acc += _dot(m1_ref[...], wo_ref[256:512, :])
    acc += _dot(m2_ref[...], wo_ref[512:768, :])
    acc += _dot(m3_ref[...], wo_ref[768:1024, :])
    x1 = _layer_norm(DN_ALPHA * x_ref[...] + acc, g1_ref[...], b1_ref[...])
    xb = x1.astype(BF16)
    acc = jnp.zeros(x1.shape, F32)
    for c in range(0, D_FF, FF_CHUNK):
        h = _dot(xb, wu_ref[:, c:c + FF_CHUNK])
        a = jnp.square(jnp.maximum(h, 0.0)).astype(BF16)
        acc += _dot(a, wd_ref[c:c + FF_CHUNK, :])
    o_ref[...] = _layer_norm(DN_ALPHA * x1 + acc, g2_ref[...], b2_ref[...])


def _out_ffn(x, mixes, lw, tm):
    n = x.shape[0]
    row = lambda i: (i, 0)
    vec = _const_spec((1, D_MODEL))
    return pl.pallas_call(
        _out_ffn_body,
        grid=(n // tm,),
        in_specs=[pl.BlockSpec((tm, D_MODEL), row)] + [pl.BlockSpec((tm, GROUP_WIDTH), row)] * 4
        + [_const_spec((D_MODEL, D_MODEL)), vec, vec, _const_spec((D_MODEL, D_FF)), _const_spec((D_FF, D_MODEL)), vec, vec],
        out_specs=pl.BlockSpec((tm, D_MODEL), row),
        out_shape=jax.ShapeDtypeStruct((n, D_MODEL), F32),
        compiler_params=_cparams("arbitrary"),
        name="proj_out_ffn",
    )(x, *mixes, lw["w_out"], lw["ln1_g"], lw["ln1_b"], lw["w_up"], lw["w_down"], lw["ln2_g"], lw["ln2_b"])


CONV_PAD = 32


def _conv_body(h_ref, buf_ref, dw_ref, dwb_ref, g_ref, b_ref, pw_ref, y_ref, new_ref, full_ref, *, T, t_valid):
    hh = h_ref[0]
    full_ref[0:8, :] = jnp.zeros((8, CONV_CH), F32)
    full_ref[2:CONV_PAD, :] = buf_ref[0]
    full_ref[CONV_PAD:CONV_PAD + T, :] = hh[:, :CONV_CH] * _sigmoid(hh[:, CONV_CH:])
    new_ref[0] = full_ref[t_valid + 2:t_valid + CONV_PAD, :]
    rc = min(T, 128)

    def chunk(c, carry):
        base = pl.multiple_of(c * rc, rc)
        win = full_ref[pl.ds(base, rc + CONV_PAD), :]
        acc = jnp.zeros((rc, CONV_CH), F32) + dwb_ref[...]
        for r in range(8):
            shifted = win[r:r + (rc + CONV_PAD - r) // 8 * 8, :]
            for k in range(CONV_WIDTH):
                if (2 + k) % 8 == r:
                    a = (2 + k) // 8 * 8
                    acc = acc + dw_ref[k:k + 1, :] * shifted[a:a + rc, :]
        y = _silu(_layer_norm(acc, g_ref[...], b_ref[...]))
        y_ref[0, pl.ds(base, rc), :] = _dot(y.astype(BF16), pw_ref[...]).astype(BF16)
        return carry

    lax.fori_loop(0, T // rc, chunk, 0)


def _conv_mixer(h3, buf, dw, dwb, g, b, pw, t_valid):
    B, T, _ = h3.shape
    return pl.pallas_call(
        functools.partial(_conv_body, T=T, t_valid=t_valid),
        grid=(B,),
        in_specs=[pl.BlockSpec((1, T, 2 * CONV_CH), lambda i: (i, 0, C_GLU // (2 * CONV_CH))),
                  pl.BlockSpec((1, CONV_WIDTH - 1, CONV_CH), lambda i: (i, 0, 0)),
                  _const_spec((CONV_WIDTH, CONV_CH)), _const_spec((1, CONV_CH)), _const_spec((1, CONV_CH)),
                  _const_spec((1, CONV_CH)), _const_spec((CONV_CH, CONV_CH))],
        out_specs=[pl.BlockSpec((1, T, CONV_CH), lambda i: (i, 0, 0)),
                   pl.BlockSpec((1, CONV_WIDTH - 1, CONV_CH), lambda i: (i, 0, 0))],
        out_shape=[jax.ShapeDtypeStruct((B, T, CONV_CH), BF16),
                   jax.ShapeDtypeStruct((B, CONV_WIDTH - 1, CONV_CH), F32)],
        scratch_shapes=[pltpu.VMEM((T + CONV_PAD, CONV_CH), F32)],
        compiler_params=_cparams("arbitrary"),
        name="conv_mixer",
    )(h3, buf, dw, dwb, g, b, pw)


POOL_PAD = 16


def _pool_body(h_ref, buf_ref, w_ref, sc_ref, y_ref, new_ref, full_ref, *, T, t_valid, offset):
    full_ref[0:8, :] = jnp.zeros((8, POOL_CH), F32)
    full_ref[1:POOL_PAD, :] = buf_ref[0]
    full_ref[POOL_PAD:POOL_PAD + T, :] = h_ref[0]
    new_ref[0] = full_ref[t_valid + 1:t_valid + POOL_PAD, :]
    rc = min(T, 128)
    lane = lax.broadcasted_iota(I32, (1, POOL_CH), 1)
    group = lane // (POOL_CH // len(POOL_WINDOWS))
    wl = jnp.where(group == 0, 2, jnp.where(group == 1, 4, jnp.where(group == 2, 8, 16)))

    def chunk(c, carry):
        base = pl.multiple_of(c * rc, rc)
        win = full_ref[pl.ds(base, rc + POOL_PAD), :]
        x0 = win[POOL_PAD:POOL_PAD + rc, :]
        sums = {}
        acc = x0
        for i in range(1, 16):
            acc = acc + win[POOL_PAD - i:POOL_PAD - i + rc, :]
            if i + 1 in POOL_WINDOWS:
                sums[i + 1] = acc
        sel = jnp.where(group == 0, sums[2], jnp.where(group == 1, sums[4], jnp.where(group == 2, sums[8], sums[16])))
        pos = offset + base + lax.broadcasted_iota(I32, (rc, 1), 0)
        cnt = jnp.minimum(pos + 1, wl).astype(F32)
        d = sel / cnt - x0
        y_ref[0, pl.ds(base, rc), :] = (_dot(d.astype(BF16), w_ref[...]) * sc_ref[...]).astype(BF16)
        return carry

    lax.fori_loop(0, T // rc, chunk, 0)


def _pool_mixer(h3, buf, wblk, scale, t_valid, offset):
    B, T, _ = h3.shape
    return pl.pallas_call(
        functools.partial(_pool_body, T=T, t_valid=t_valid, offset=offset),
        grid=(B,),
        in_specs=[pl.BlockSpec((1, T, POOL_CH), lambda i: (i, 0, C_POOL // POOL_CH)),
                  pl.BlockSpec((1, POOL_BUF, POOL_CH), lambda i: (i, 0, 0)),
                  _const_spec((POOL_CH, POOL_CH)), _const_spec((1, POOL_CH))],
        out_specs=[pl.BlockSpec((1, T, POOL_CH), lambda i: (i, 0, 0)),
                   pl.BlockSpec((1, POOL_BUF, POOL_CH), lambda i: (i, 0, 0))],
        out_shape=[jax.ShapeDtypeStruct((B, T, POOL_CH), BF16),
                   jax.ShapeDtypeStruct((B, POOL_BUF, POOL_CH), F32)],
        scratch_shapes=[pltpu.VMEM((T + POOL_PAD, POOL_CH), F32)],
        compiler_params=_cparams("arbitrary"),
        name="pool_mixer",
    )(h3, buf, wblk, scale)


GDN_PAD = 8
CK = GDN_CHUNK


def _gdn_body(qkv_ref, z_ref, sm_ref, buf_ref, s0_ref, cw_ref, alog_ref, dtb_ref, ng_ref,
              y_ref, newbuf_ref, sout_ref, full_ref, c_ref, g_ref, bt_ref, gi_ref, bi_ref, u_ref, w_ref, a_ref,
              qg_ref, kdt_ref, s_ref, *, T, Tp, t_valid):
    full_ref[0:8, :] = jnp.zeros((8, GDN_QKV), F32)
    full_ref[5:GDN_PAD, :] = buf_ref[0]
    full_ref[GDN_PAD:GDN_PAD + T, :] = qkv_ref[0]
    newbuf_ref[0] = full_ref[t_valid + 5:t_valid + GDN_PAD, :]
    if Tp > t_valid:
        c_ref[...] = jnp.zeros((Tp, GDN_QKV), F32)
        g_ref[...] = jnp.zeros((Tp, LANES), F32)
        bt_ref[...] = jnp.zeros((Tp, LANES), F32)

    rc = min(t_valid, 128)

    def conv_chunk(c, carry):
        base = pl.multiple_of(c * rc, rc)
        win = full_ref[pl.ds(base, rc + GDN_PAD), :] if rc % 8 == 0 else full_ref[0:rc + GDN_PAD, :]
        acc = jnp.zeros((rc, GDN_QKV), F32)
        for k in range(GDN_CONV):
            acc = acc + cw_ref[k:k + 1, :] * win[5 + k:5 + k + rc, :]
        sm = sm_ref[0, pl.ds(base, rc), :] if rc % 8 == 0 else sm_ref[0, 0:rc, :]
        x = sm + dtb_ref[...]
        softplus = jnp.maximum(x, 0.0) + jnp.log1p(jnp.exp(-jnp.abs(x)))
        gv = -jnp.exp(alog_ref[...]) * softplus
        bv = _sigmoid(sm)
        if rc % 8 == 0:
            c_ref[pl.ds(base, rc), :] = _silu(acc)
            g_ref[pl.ds(base, rc), :] = gv
            bt_ref[pl.ds(base, rc), :] = bv
        else:
            c_ref[0:rc, :] = _silu(acc)
            g_ref[0:rc, :] = gv
            bt_ref[0:rc, :] = bv
        return carry

    lax.fori_loop(0, t_valid // rc, conv_chunk, 0)

    HW = GDN_HEADS * GDN_DK
    lane = lax.broadcasted_iota(I32, (1, HW), 1)
    hmask = [jnp.where(lane // GDN_DK == h, 1.0, 0.0).astype(BF16) for h in range(GDN_HEADS)]
    row = lax.broadcasted_iota(I32, (CK, 1), 0)
    jl = lane % CK
    incl = row >= jl
    strict = row > jl
    eye_all = jnp.where(row == jl, 1.0, 0.0)
    er = lax.broadcasted_iota(I32, (LANES, HW), 0)
    ec = lax.broadcasted_iota(I32, (LANES, HW), 1) // GDN_DK
    exp_a = jnp.where(er == ec + SM_A, 1.0, 0.0).astype(BF16)
    exp_b = jnp.where(er == ec + SM_B, 1.0, 0.0).astype(BF16)
    br = lax.broadcasted_iota(I32, (HW, HW), 0) // GDN_DK
    bc = lax.broadcasted_iota(I32, (HW, HW), 1) // GDN_DK
    same_head = br == bc
    bones = jnp.where(same_head, 1.0, 0.0).astype(BF16)

    def blockdiag(x):
        return jnp.concatenate([x * m for m in hmask], axis=0)

    def expand3(x, e):
        x1, x2, x3 = _split3(x)
        return _dot(jnp.concatenate([x1, x2, x3], axis=1), jnp.concatenate([e, e, e], axis=0))

    def bd_dot_hl(a, b):
        ah = a.astype(BF16)
        al = (a - ah.astype(F32)).astype(BF16)
        bh = b.astype(BF16)
        bl = (b - bh.astype(F32)).astype(BF16)
        bdh = blockdiag(bh)
        return _dot(jnp.concatenate([ah, al, ah], axis=1), jnp.concatenate([bdh, bdh, blockdiag(bl)], axis=0))

    rb = min(Tp, 4 * CK)
    row_in_chunk = lax.broadcasted_iota(I32, (rb, 1), 0) % CK

    def prep(c, carry):
        r0 = pl.multiple_of(c * rb, rb)
        g = g_ref[pl.ds(r0, rb), :]
        for s in (1, 2, 4, 8, 16, 32):
            g = g + jnp.where(row_in_chunk >= s, jnp.roll(g, s, axis=0), 0.0)
        gi_ref[pl.ds(r0, rb), :] = expand3(g, exp_a)
        bi_ref[pl.ds(r0, rb), :] = expand3(bt_ref[pl.ds(r0, rb), :], exp_b)
        for part, scale in ((0, GDN_DK ** -0.5), (1, 1.0)):
            x = c_ref[pl.ds(r0, rb), HW * part:HW * (part + 1)]
            ssq = expand3(x * x, bones)
            c_ref[pl.ds(r0, rb), HW * part:HW * (part + 1)] = x * lax.rsqrt(ssq + 1e-6) * scale
        return carry

    lax.fori_loop(0, Tp // rb, prep, 0)

    n_chunks = Tp // CK
    group = 8 if n_chunks % 8 == 0 else 1

    def solve(it, carry):
        r0s = [pl.multiple_of((it * group + k) * CK, CK) for k in range(group)]
        gi = [gi_ref[pl.ds(r0, CK), :] for r0 in r0s]
        kn = [c_ref[pl.ds(r0, CK), HW:2 * HW] for r0 in r0s]
        decay = [jnp.exp(jnp.where(incl, g - jnp.sum(eye_all * g, axis=0, keepdims=True), NEG)) for g in gi]
        kb = [k * bi_ref[pl.ds(r0, CK), :] for k, r0 in zip(kn, r0s)]
        kst = [blockdiag(k.astype(BF16)) for k in kn]
        pw = [-jnp.where(strict, _dot_nt(b.astype(BF16), s) * d, 0.0) for b, s, d in zip(kb, kst, decay)]
        tinv = [eye_all + p for p in pw]
        pw = [bd_dot_hl(p, p) for p in pw]
        for _ in range(4):
            both = [bd_dot_hl(jnp.concatenate([t, p], axis=0), p) for t, p in zip(tinv, pw)]
            tinv = [t + b[:CK] for t, b in zip(tinv, both)]
            pw = [b[CK:] for b in both]
        tinv = [t + bd_dot_hl(t, p) for t, p in zip(tinv, pw)]
        for k, r0 in enumerate(r0s):
            vb = c_ref[pl.ds(r0, CK), 2 * HW:3 * HW] * bi_ref[pl.ds(r0, CK), :]
            u_ref[pl.ds(r0, CK), :] = bd_dot_hl(tinv[k], vb)
        for k, r0 in enumerate(r0s):
            w_ref[pl.ds(r0, CK), :] = bd_dot_hl(tinv[k], kb[k] * jnp.exp(gi[k]))
        for k, r0 in enumerate(r0s):
            qn = c_ref[pl.ds(r0, CK), 0:HW]
            a_ref[pl.ds(r0, CK), :] = (_dot_nt(qn.astype(BF16), kst[k]) * decay[k]).astype(BF16)
            qg_ref[pl.ds(r0, CK), :] = (qn * jnp.exp(gi[k])).astype(BF16)
            kdt_ref[it * group + k] = (kn[k] * jnp.exp(gi[k][CK - 1:CK, :] - gi[k])).T.astype(BF16)
        return carry

    lax.fori_loop(0, n_chunks // group, solve, 0)

    s_ref[...] = jnp.zeros((HW, HW), F32)
    for h in range(GDN_HEADS):
        s_ref[GDN_DK * h:GDN_DK * (h + 1), GDN_DV * h:GDN_DV * (h + 1)] = s0_ref[0, h]
    rows_out = min(CK, T)
    ng_all = jnp.concatenate([ng_ref[...]] * GDN_HEADS, axis=1)

    def recur(c, carry):
        r0 = pl.multiple_of(c * CK, CK)
        s_all = s_ref[...]
        s_b = s_all.astype(BF16)
        vnew = u_ref[pl.ds(r0, CK), :] - _dot(w_ref[pl.ds(r0, CK), :].astype(BF16), s_b)
        vnb = vnew.astype(BF16)
        glast = gi_ref[pl.ds(r0 + CK - 1, 1), :]
        s_ref[...] = s_all * jnp.exp(glast) + jnp.where(same_head, _dot(kdt_ref[c], vnb), 0.0)
        u_ref[pl.ds(r0, CK), :] = _dot(qg_ref[pl.ds(r0, CK), :], s_b) + _dot(a_ref[pl.ds(r0, CK), :], blockdiag(vnb))
        return carry

    lax.fori_loop(0, Tp // CK, recur, 0)
    for h in range(GDN_HEADS):
        sout_ref[0, h] = s_ref[GDN_DK * h:GDN_DK * (h + 1), GDN_DV * h:GDN_DV * (h + 1)]

    def finish(c, carry):
        r0 = pl.multiple_of(c * rb, rb)
        o = u_ref[pl.ds(r0, rb), :]
        on = o * lax.rsqrt(expand3(o * o, bones) * (1.0 / GDN_DV) + LN_EPS) * ng_all
        if T >= CK:
            y_ref[0, pl.ds(r0, rb), :] = (on * _silu(z_ref[0, pl.ds(r0, rb), :])).astype(BF16)
        else:
            y_ref[0] = (on[0:rows_out] * _silu(z_ref[0])).astype(BF16)
        return carry

    lax.fori_loop(0, Tp // rb, finish, 0)


def _gdn_mixer(h3, buf, s0, cw, alog_l, dtb_l, ng, t_valid):
    B, T, _ = h3.shape
    Tp = -(-T // CK) * CK
    return pl.pallas_call(
        functools.partial(_gdn_body, T=T, Tp=Tp, t_valid=t_valid),
        grid=(B,),
        in_specs=[pl.BlockSpec((1, T, GDN_QKV), lambda i: (i, 0, C_GQKV // GDN_QKV)),
                  pl.BlockSpec((1, T, 256), lambda i: (i, 0, C_Z // 256)),
                  pl.BlockSpec((1, T, LANES), lambda i: (i, 0, C_SM // LANES)),
                  pl.BlockSpec((1, GDN_CONV - 1, GDN_QKV), lambda i: (i, 0, 0)),
                  pl.BlockSpec((1, GDN_HEADS, GDN_DK, GDN_DV), lambda i: (i, 0, 0, 0)),
                  _const_spec((GDN_CONV, GDN_QKV)), _const_spec((1, LANES)), _const_spec((1, LANES)),
                  _const_spec((1, GDN_DV))],
        out_specs=[pl.BlockSpec((1, T, 256), lambda i: (i, 0, 0)),
                   pl.BlockSpec((1, GDN_CONV - 1, GDN_QKV), lambda i: (i, 0, 0)),
                   pl.BlockSpec((1, GDN_HEADS, GDN_DK, GDN_DV), lambda i: (i, 0, 0, 0))],
        out_shape=[jax.ShapeDtypeStruct((B, T, 256), BF16),
                   jax.ShapeDtypeStruct((B, GDN_CONV - 1, GDN_QKV), F32),
                   jax.ShapeDtypeStruct((B, GDN_HEADS, GDN_DK, GDN_DV), F32)],
        scratch_shapes=[pltpu.VMEM((T + GDN_PAD, GDN_QKV), F32), pltpu.VMEM((Tp, GDN_QKV), F32),
                        pltpu.VMEM((Tp, LANES), F32), pltpu.VMEM((Tp, LANES), F32),
                        pltpu.VMEM((Tp, 256), F32), pltpu.VMEM((Tp, 256), F32),
                        pltpu.VMEM((Tp, 256), F32), pltpu.VMEM((Tp, 256), F32), pltpu.VMEM((Tp, 256), BF16),
                        pltpu.VMEM((Tp, 256), BF16), pltpu.VMEM((Tp // CK, GDN_HEADS * GDN_DK, CK), BF16),
                        pltpu.VMEM((GDN_HEADS * GDN_DK, GDN_HEADS * GDN_DV), F32)],
        compiler_params=_cparams("arbitrary"),
        name="gdn_mixer",
    )(h3, h3, h3, buf, s0, cw, alog_l, dtb_l, ng)


def _peterm_body(pe_ref, w1_ref, o_ref):
    pe = jnp.broadcast_to(pe_ref[0], (8, CMP_BLOCK * HEAD_DIM)).astype(BF16)
    o_ref[0] = _dot(pe, w1_ref[0].astype(BF16))


def _peterm(pe_flat, w1):
    n = pe_flat.shape[0]
    return pl.pallas_call(
        _peterm_body,
        grid=(n,),
        in_specs=[pl.BlockSpec((1, 1, CMP_BLOCK * HEAD_DIM), lambda i: (i, 0, 0)),
                  pl.BlockSpec((1, CMP_BLOCK * HEAD_DIM, HEAD_DIM), lambda i: (i, 0, 0))],
        out_specs=pl.BlockSpec((1, 8, HEAD_DIM), lambda i: (i, 0, 0)),
        out_shape=jax.ShapeDtypeStruct((n, 8, HEAD_DIM), F32),
        compiler_params=_cparams("arbitrary"),
        name="cmp_pe_term",
    )(pe_flat, w1)


QT = 128
WBAND = WINDOW + QT


def _tabw_body(rb_ref, o_ref):
    i = lax.broadcasted_iota(I32, (QT, WBAND), 0)
    j = lax.broadcasted_iota(I32, (QT, WBAND), 1)
    bk = _t5_bucket(WINDOW + i - j)
    for h in range(ATT_HEADS):
        o_ref[h] = _bias_lookup(bk, lambda k: rb_ref[k, h])


def _tabc_body(rb_ref, o_ref):
    p0 = pl.program_id(0) * QT
    ns = o_ref.shape[-1]
    t = p0 + lax.broadcasted_iota(I32, (QT, ns), 0)
    n = lax.broadcasted_iota(I32, (QT, ns), 1)
    bk = _t5_bucket(t - (n * CMP_STRIDE + CMP_BLOCK - 1))
    for h in range(ATT_HEADS):
        o_ref[h] = _bias_lookup(bk, lambda k: rb_ref[k, h])


def _bias_tables(rel_bias, T):
    ns = T // CMP_STRIDE
    tabw = pl.pallas_call(
        _tabw_body, in_specs=[_smem_spec()],
        out_shape=jax.ShapeDtypeStruct((ATT_HEADS, QT, WBAND), F32), name="bias_window_table")(rel_bias)
    tabc = pl.pallas_call(
        _tabc_body, grid=(T // QT,), in_specs=[_smem_spec()],
        out_specs=pl.BlockSpec((ATT_HEADS, QT, ns), lambda i: (0, i, 0)),
        out_shape=jax.ShapeDtypeStruct((ATT_HEADS, T, ns), F32),
        compiler_params=_cparams("arbitrary"), name="bias_cmp_table")(rel_bias)
    return tabw, tabc


def _compress_pre(load_rows, wcat_ref, c):
    acc = None
    for r in range(0, CMP_STRIDE, 2):
        lhs = jnp.concatenate([load_rows(r), load_rows(r + 1)], axis=1).astype(BF16)
        part = _dot(lhs, wcat_ref[c, r // 2])
        acc = part if acc is None else acc + part
    return acc


def _compress_finish(hcat, pt, w2):
    pre = hcat[:, :LANES] + jnp.roll(hcat[:, LANES:], -1, axis=0) + pt
    return _dot(_silu(pre).astype(BF16), w2)


def _topn_select(score, n_cols):
    j = lax.broadcasted_iota(I32, score.shape, 1)
    rank = jnp.zeros(score.shape, F32)
    for jp in range(n_cols):
        col = score[:, jp:jp + 1]
        ahead = (col > score) | ((col == score) & (jp < j))
        rank = rank + jnp.where(ahead, 1.0, 0.0)
    return jnp.where((rank < SEL_TOPN) & (score > 0.5 * NEG), 1.0, 0.0)


def _topn_select_rows(score_t, n_rows):
    j = lax.broadcasted_iota(I32, score_t.shape, 0)
    rank = jnp.zeros(score_t.shape, F32)
    for jp in range(n_rows):
        r = score_t[jp:jp + 1, :]
        ahead = (r > score_t) | ((r == score_t) & (jp < j))
        rank = rank + jnp.where(ahead, 1.0, 0.0)
    return jnp.where((rank < SEL_TOPN) & (score_t > 0.5 * NEG), 1.0, 0.0)


def _nsa_prompt_body(rb_ref, q_ref, sm_ref, kv_ref, wcat_ref, w2_ref, pt_ref, tabw_ref, tabc_ref, covert_ref,
                     y_ref, kvp_ref, kc_ref, vc_ref, cmp_ref, nsel_ref, *, T, FT):
    ns = T // CMP_STRIDE
    n_sel = T // SEL_BLOCK
    qt = pl.program_id(1)
    p0 = pl.multiple_of(qt * QT, QT)

    @pl.when(qt == 0)
    def _():
        kvp_ref[0:WINDOW, :] = jnp.zeros((WINDOW, 512), BF16)
        kvp_ref[WINDOW:WINDOW + T, :] = kv_ref[0, :, 256:768].astype(BF16)
        for c, dst in ((0, kc_ref), (1, vc_ref)):
            cmp_ref[c] = kv_ref[0, :, LANES * c:LANES * (c + 1)]
            hcat = _compress_pre(lambda r: cmp_ref[c, pl.ds(r, ns, stride=CMP_STRIDE), :], wcat_ref, c)
            dst[...] = _compress_finish(hcat, pt_ref[c], w2_ref[c]).astype(BF16)

    gates = _sigmoid(sm_ref[0])
    t = p0 + lax.broadcasted_iota(I32, (QT, 1), 0)
    n_i = lax.broadcasted_iota(I32, (1, ns), 1)
    ok_cmp = (t - (n_i * CMP_STRIDE + CMP_BLOCK - 1) >= 0) & (n_i < ns - 1)
    ok_cmp2 = jnp.concatenate([ok_cmp, ok_cmp], axis=0)
    t_l = p0 + lax.broadcasted_iota(I32, (1, QT), 1)
    j_s = lax.broadcasted_iota(I32, (n_sel, 1), 0)
    cur = t_l // SEL_BLOCK
    forced = (j_s == 0) | (j_s == cur) | (j_s == cur - 1)
    avail = j_s * SEL_BLOCK <= t_l
    m_near = p0 - QT + lax.broadcasted_iota(I32, (1, 2 * QT), 1)
    near_blk = (p0 - QT + lax.broadcasted_iota(I32, (n_sel, 2 * QT), 1)) // SEL_BLOCK
    e_near_neg = jnp.where(near_blk == j_s, NEG, 0.0).astype(BF16)
    causal_near = jnp.where((m_near >= 0) & (m_near <= t), 0.0, NEG)
    m_win = p0 - WINDOW + lax.broadcasted_iota(I32, (1, WBAND), 1)
    d_win = t - m_win
    add_win = jnp.where((m_win >= 0) & (d_win >= 0) & (d_win < WINDOW), 0.0, NEG)
    two = lambda x: jnp.concatenate([x, x], axis=0)

    KH = range(ATT_KV_HEADS)
    kcol = lambda kh, base: slice(base + 64 * kh, base + 64 * kh + 64)
    q2 = [(jnp.concatenate([q_ref[0, :, kcol(2 * kh, 0)], q_ref[0, :, kcol(2 * kh + 1, 0)]], axis=0)
           * ATT_SCALE).astype(BF16) for kh in KH]
    s_c = [_dot_nt(q2[kh], kc_ref[:, kcol(kh, 0)]) + jnp.concatenate([tabc_ref[2 * kh], tabc_ref[2 * kh + 1]], axis=0)
           for kh in KH]
    p_c = [_masked_softmax_parts([(s, ok_cmp2)])[0] for s in s_c]
    o_cmp = [_dot(p_c[kh].astype(BF16), vc_ref[:, kcol(kh, 0)]) for kh in KH]
    nsel_t = []
    for kh in KH:
        p3 = _split3(p_c[kh][0:QT] + p_c[kh][QT:2 * QT])
        imp_t = (_dot_nt(covert_ref[...], p3[0]) + _dot_nt(covert_ref[...], p3[1])
                 + _dot_nt(covert_ref[...], p3[2]))
        score_t = jnp.where(avail, imp_t + jnp.where(forced, FORCE, 0.0), NEG)
        nsel = 1.0 - _topn_select_rows(score_t, n_sel)
        nsel_ref[kh] = nsel
        nsel_t.append(nsel.astype(BF16))
    add_near = [_dot_tn(nsel_t[kh], e_near_neg) + causal_near for kh in KH]
    nb = FT // SEL_BLOCK
    e_tile = jnp.where(lax.broadcasted_iota(I32, (nb, FT), 1) // SEL_BLOCK == lax.broadcasted_iota(I32, (nb, FT), 0),
                       NEG, 0.0)
    near_rows = pl.ds(WINDOW + p0 - QT, 2 * QT)
    bias_near = [jnp.concatenate(
        [tabw_ref[2 * kh + g, :, WINDOW - QT:WINDOW + QT] - rb_ref[N_BUCKETS - 1, 2 * kh + g] + add_near[kh]
         for g in range(2)], axis=0) for kh in KH]
    s_n = [_dot_nt(q2[kh], kvp_ref[near_rows, kcol(kh, 0)]) + bias_near[kh] for kh in KH]
    m0 = [s.max(-1, keepdims=True) for s in s_n]
    e_n = [jnp.exp(s - m) for s, m in zip(s_n, m0)]
    init = tuple((m0[kh], e_n[kh].sum(-1, keepdims=True), _dot(e_n[kh].astype(BF16), kvp_ref[near_rows, kcol(kh, 128)]))
                 for kh in KH)

    def far(i, carry):
        k0 = pl.multiple_of(i * FT, FT)
        rows = pl.ds(WINDOW + k0, FT)
        lim = jnp.where(k0 + lax.broadcasted_iota(I32, (1, FT), 1) < p0 - QT, 0.0, NEG)
        blocks = pl.ds(pl.multiple_of(i * nb, nb), nb)
        s = [_dot_nt(q2[kh], kvp_ref[rows, kcol(kh, 0)]) + two(_dot_tn(nsel_ref[kh, blocks, :], e_tile) + lim)
             for kh in KH]
        m_new = [jnp.maximum(carry[kh][0], s[kh].max(-1, keepdims=True)) for kh in KH]
        alpha = [jnp.exp(carry[kh][0] - m_new[kh]) for kh in KH]
        e = [jnp.exp(s[kh] - m_new[kh]) for kh in KH]
        return tuple((m_new[kh], alpha[kh] * carry[kh][1] + e[kh].sum(-1, keepdims=True),
                      alpha[kh] * carry[kh][2] + _dot(e[kh].astype(BF16), kvp_ref[rows, kcol(kh, 128)])) for kh in KH)

    n_far = (jnp.maximum(p0 - QT, 0) + FT - 1) // FT
    fin = lax.fori_loop(0, n_far, far, init)
    o_sel = [fin[kh][2] / jnp.maximum(fin[kh][1], 1e-30) for kh in KH]
    win_rows = pl.ds(p0, WBAND)
    s_w = [_dot_nt(q2[kh], kvp_ref[win_rows, kcol(kh, 256)])
           + jnp.concatenate([tabw_ref[2 * kh] + add_win, tabw_ref[2 * kh + 1] + add_win], axis=0) for kh in KH]
    e_w = [jnp.exp(s - s.max(-1, keepdims=True)) for s in s_w]
    o_win = [_dot(e_w[kh].astype(BF16), kvp_ref[win_rows, kcol(kh, 384)])
             / jnp.maximum(e_w[kh].sum(-1, keepdims=True), 1e-30) for kh in KH]
    for h in range(ATT_HEADS):
        kh, g = divmod(h, 2)
        rows = slice(QT * g, QT * (g + 1))
        out = (gates[:, h:h + 1] * o_cmp[kh][rows] + gates[:, 4 + h:5 + h] * o_sel[kh][rows]
               + gates[:, 8 + h:9 + h] * o_win[kh][rows])
        y_ref[0, :, 64 * h:64 * h + 64] = out.astype(BF16)


def _nsa_prompt(h3, rel_bias, wcat, w2bd, pt, tabw, tabc, covert):
    B, T, _ = h3.shape
    ns = T // CMP_STRIDE
    n_sel = T // SEL_BLOCK
    ft = min(512, T)
    return pl.pallas_call(
        functools.partial(_nsa_prompt_body, T=T, FT=ft),
        grid=(B, T // QT),
        in_specs=[_smem_spec(),
                  pl.BlockSpec((1, QT, 256), lambda b, i: (b, i, C_Q // 256)),
                  pl.BlockSpec((1, QT, LANES), lambda b, i: (b, i, C_SM // LANES)),
                  pl.BlockSpec((1, T, 768), lambda b, i: (b, 0, C_KV // 768)),
                  _const_spec((2, CMP_STRIDE // 2, 2 * LANES, 2 * LANES)), _const_spec((2, LANES, LANES)),
                  _const_spec((2, 1, LANES)), _const_spec((ATT_HEADS, QT, WBAND)),
                  pl.BlockSpec((ATT_HEADS, QT, ns), lambda b, i: (0, i, 0)),
                  _const_spec((n_sel, ns))],
        out_specs=pl.BlockSpec((1, QT, 256), lambda b, i: (b, i, 0)),
        out_shape=jax.ShapeDtypeStruct((B, T, 256), BF16),
        scratch_shapes=[pltpu.VMEM((WINDOW + T, 512), BF16), pltpu.VMEM((ns, LANES), BF16),
                        pltpu.VMEM((ns, LANES), BF16), pltpu.VMEM((2, T, LANES), F32),
                        pltpu.VMEM((ATT_KV_HEADS, n_sel, QT), F32)],
        compiler_params=_cparams("arbitrary", "arbitrary"),
        name="nsa_prompt",
    )(rel_bias, h3, h3, h3, wcat, w2bd, pt, tabw, tabc, covert)


def _nsa_s1_body(pt_ref, cache_ref, wcat_ref, o_ref, buf_ref, row_ref, sem, *, layer, CH):
    s = pl.program_id(0)
    nsteps = pl.num_programs(0)
    slot = s % 2

    def page_copy(step, p, sl):
        phys = pt_ref[step * CH + p]
        return pltpu.make_async_copy(cache_ref.at[layer, phys, pl.ds(0, 2)], buf_ref.at[sl, p], sem.at[sl])

    def issue(step, sl):
        def one(p, carry):
            page_copy(step, p, sl).start()
            return carry
        lax.fori_loop(0, CH, one, 0)

    @pl.when(s == 0)
    def _():
        issue(0, 0)

    @pl.when(s + 1 < nsteps)
    def _():
        issue(s + 1, 1 - slot)

    def wait_one(p, carry):
        page_copy(s, p, slot).wait()
        return carry
    lax.fori_loop(0, CH, wait_one, 0)

    gp = 16 if CH % 16 == 0 else CH
    gsub = gp * LANES // CMP_STRIDE
    for g in range(CH // gp):
        for p in range(g * gp, (g + 1) * gp):
            for c in range(2):
                row_ref[c, LANES * p:LANES * (p + 1), :] = buf_ref[slot, p, c].reshape(2 * HEAD_DIM, LANES).T
        for c in range(2):
            hcat = _compress_pre(lambda r: row_ref[c, pl.ds(g * gp * LANES + r, gsub, stride=CMP_STRIDE), :], wcat_ref, c)
            o_ref[0, g * gsub:(g + 1) * gsub, 2 * LANES * c:2 * LANES * (c + 1)] = hcat


def _nsa_s1(page_flat, cache_t, wcat, layer, B, NP):
    CH = min(64, NP)
    nsub = CH * LANES // CMP_STRIDE
    per_b = NP // CH
    grid_spec = pltpu.PrefetchScalarGridSpec(
        num_scalar_prefetch=1,
        grid=(B * per_b,),
        in_specs=[pl.BlockSpec(memory_space=pl.ANY),
                  pl.BlockSpec((2, CMP_STRIDE // 2, 2 * LANES, 2 * LANES), lambda s, pt: (0, 0, 0, 0),
                               pipeline_mode=pl.Buffered(1))],
        out_specs=pl.BlockSpec((1, nsub, 4 * LANES), lambda s, pt: (s // per_b, s % per_b, 0)),
        scratch_shapes=[pltpu.VMEM((2, CH, 2, 2, HEAD_DIM, LANES), F32), pltpu.VMEM((2, CH * LANES, LANES), F32),
                        pltpu.SemaphoreType.DMA((2,))],
    )
    return pl.pallas_call(
        functools.partial(_nsa_s1_body, layer=layer, CH=CH),
        grid_spec=grid_spec,
        out_shape=jax.ShapeDtypeStruct((B, NP * LANES // CMP_STRIDE, 4 * LANES), F32),
        compiler_params=_cparams("arbitrary"),
        name="nsa_sample_compress",
    )(page_flat, cache_t, wcat)


def _row_bias(bk, rb_ref, kh, g_of_row):
    return _bias_lookup(bk, lambda k: jnp.where(g_of_row == 1, rb_ref[k, 2 * kh + 1], rb_ref[k, 2 * kh]))


MAX_SEL_PAGES = 64
SLOT_INFO = 5


def _nsa_s2_body(rb_ref, hc_ref, q_ref, w2_ref, pt_ref, cover_ref, pair_ref, ut_ref, ocmp_ref, slots_ref,
                 *, P, t_valid, NSELP):
    nsub = hc_ref.shape[1]
    kc = _compress_finish(hc_ref[0, :, 0:2 * LANES], pt_ref[0], w2_ref[0]).astype(BF16)
    vc = _compress_finish(hc_ref[0, :, 2 * LANES:4 * LANES], pt_ref[1], w2_ref[1]).astype(BF16)
    n_sel = P // SEL_BLOCK + 1
    row = lax.broadcasted_iota(I32, (16, 1), 0)
    t16 = row % 8
    g16 = row // 8
    n_i = lax.broadcasted_iota(I32, (1, nsub), 1)
    d_cmp = (P + t16) - (n_i * CMP_STRIDE + CMP_BLOCK - 1)
    ok_cmp = d_cmp >= 0
    bk_cmp = _t5_bucket(d_cmp)
    t8 = lax.broadcasted_iota(I32, (8, 1), 0)
    j_i = lax.broadcasted_iota(I32, (1, NSELP), 1)
    qpos = P + t8
    cur = qpos // SEL_BLOCK
    forced = (j_i == 0) | (j_i == cur) | (j_i == cur - 1)
    avail = (j_i * SEL_BLOCK <= qpos) & (j_i < n_sel)
    wts = jnp.where(t8 < t_valid, jnp.left_shift(1, t8 + 4 * (j_i % 2)), 0).astype(F32)
    NP = P // LANES
    page_l = lax.broadcasted_iota(I32, (1, 2 * LANES), 1)
    slot_s = lax.broadcasted_iota(I32, (MAX_SEL_PAGES, 1), 0).astype(F32)
    col = lax.broadcasted_iota(I32, (1, LANES), 1)
    pick = lambda x, lane_no: jnp.sum(jnp.where(page_l == lane_no, x, 0.0), axis=1, keepdims=True)
    scores = []
    for kh in range(ATT_KV_HEADS):
        q16 = jnp.concatenate([q_ref[0, :, 128 * kh:128 * kh + 64], q_ref[0, :, 128 * kh + 64:128 * kh + 128]],
                              axis=0).astype(BF16)
        s = _dot_nt(q16, kc[:, 64 * kh:64 * kh + 64]) * ATT_SCALE + _row_bias(bk_cmp, rb_ref, kh, g16)
        (p,) = _masked_softmax_parts([(s, ok_cmp)])
        o = _dot(p.astype(BF16), vc[:, 64 * kh:64 * kh + 64])
        ocmp_ref[0, :, 128 * kh:128 * kh + 64] = o[0:8]
        ocmp_ref[0, :, 128 * kh + 64:128 * kh + 128] = o[8:16]
        p3 = _split3(p[0:8] + p[8:16])
        imp = _dot(p3[0], cover_ref[...]) + _dot(p3[1], cover_ref[...]) + _dot(p3[2], cover_ref[...])
        scores.append(jnp.where(avail, imp + jnp.where(forced, FORCE, 0.0), NEG))
    sel_all = _topn_select(jnp.concatenate(scores, axis=0), n_sel)
    for kh in range(ATT_KV_HEADS):
        sel = sel_all[8 * kh:8 * kh + 8]
        colsum = jnp.sum(sel * wts, axis=0, keepdims=True)
        urow = _dot(jnp.broadcast_to(colsum, (8, NSELP)).astype(BF16), pair_ref[...])[0:1]
        nz = jnp.where((urow > 0.5) & (page_l < NP - 1), 1.0, 0.0)
        before = _dot(jnp.broadcast_to(nz, (8, 2 * LANES)).astype(BF16), ut_ref[...])[0:1]
        in_slot = (nz > 0.5) & (before == slot_s)
        page_of = jnp.sum(jnp.where(in_slot, page_l.astype(F32), 0.0), axis=1, keepdims=True)
        bits_of = jnp.sum(jnp.where(in_slot, urow, 0.0), axis=1, keepdims=True)
        count = jnp.sum(nz, axis=1, keepdims=True)
        info = jnp.where(col == 0, page_of, jnp.where(col == 1, bits_of, jnp.where(
            col == 2, count, jnp.where(col == 3, pick(urow, NP - 1), jnp.where(col == 4, pick(urow, NP), 0.0)))))
        slots_ref[0, MAX_SEL_PAGES * kh:MAX_SEL_PAGES * (kh + 1), :] = info.astype(I32)


def _nsa_s2(hs3, hc, rel_bias, w2bd, pt, cover, pair, P, t_valid):
    B, T, _ = hs3.shape
    nsub = hc.shape[1]
    nselp = cover.shape[1]
    ut = (jnp.arange(2 * LANES)[:, None] < jnp.arange(2 * LANES)[None, :]).astype(BF16)
    return pl.pallas_call(
        functools.partial(_nsa_s2_body, P=P, t_valid=t_valid, NSELP=nselp),
        grid=(B,),
        in_specs=[_smem_spec(),
                  pl.BlockSpec((1, nsub, 4 * LANES), lambda b: (b, 0, 0)),
                  pl.BlockSpec((1, T, 256), lambda b: (b, 0, C_Q // 256)),
                  _const_spec((2, LANES, LANES)), _const_spec((2, 1, LANES)),
                  _const_spec((nsub, nselp)), _const_spec((nselp, 2 * LANES)), _const_spec((2 * LANES, 2 * LANES))],
        out_specs=[pl.BlockSpec((1, 8, 256), lambda b: (b, 0, 0)),
                   pl.BlockSpec((1, ATT_KV_HEADS * MAX_SEL_PAGES, LANES), lambda b: (b, 0, 0))],
        out_shape=[jax.ShapeDtypeStruct((B, 8, 256), F32),
                   jax.ShapeDtypeStruct((B, ATT_KV_HEADS * MAX_SEL_PAGES, LANES), I32)],
        compiler_params=_cparams("arbitrary"),
        name="nsa_sample_cmp_select",
    )(rel_bias, hc, hs3, w2bd, pt, cover, pair, ut)


def _nsa_s3_body(pt_ref, sl_ref, rb_ref, q_ref, sm_ref, kvn_ref, ocmp_ref, cache_ref, win_ref, y_ref,
                 kcat_ref, vcat_ref, rec_ref, sem, *, layer, P, NP, t_valid):
    b = pl.program_id(0)

    def slot_info(kh, i, field):
        return sl_ref[((b * ATT_KV_HEADS + kh) * MAX_SEL_PAGES + i) * SLOT_INFO + field]

    wb = win_ref.shape[-1]
    ncat = MAX_SEL_PAGES * LANES
    row = lax.broadcasted_iota(I32, (16, 1), 0)
    t16 = row % 8
    g16 = row // 8
    valid_row = t16 < t_valid
    gates = _sigmoid(sm_ref[0])

    @pl.when(b == 0)
    def _():
        kcat_ref[...] = jnp.zeros(kcat_ref.shape, F32)
        vcat_ref[...] = jnp.zeros(vcat_ref.shape, F32)

    rec_copy = pltpu.make_async_copy(cache_ref.at[layer, pt_ref[b * NP + NP - 1], pl.ds(2, 2)], rec_ref, sem.at[1])
    rec_copy.start()

    def kv_copies(kh, phys, i):
        dst = pl.ds(pl.multiple_of(i * LANES, LANES), LANES)
        return (pltpu.make_async_copy(cache_ref.at[layer, phys, 2, kh], kcat_ref.at[kh, :, dst], sem.at[0]),
                pltpu.make_async_copy(cache_ref.at[layer, phys, 3, kh], vcat_ref.at[kh, :, dst], sem.at[0]))

    counts = [slot_info(kh, 0, 2) for kh in range(ATT_KV_HEADS)]
    for kh in range(ATT_KV_HEADS):
        def issue(i, carry, kh=kh):
            ck, cv = kv_copies(kh, pt_ref[b * NP + slot_info(kh, i, 0)], i)
            ck.start()
            cv.start()
            return carry
        lax.fori_loop(0, counts[kh], issue, 0)

    q16fs = [jnp.concatenate([q_ref[0, :, 128 * kh:128 * kh + 64], q_ref[0, :, 128 * kh + 64:128 * kh + 128]], axis=0)
             for kh in range(ATT_KV_HEADS)]

    o_wins = []
    for kh in range(ATT_KV_HEADS):
        q16f = q16fs[kh]
        q16 = q16f.astype(BF16)
        col_w = lax.broadcasted_iota(I32, (1, wb), 1)
        d_w = wb + t16 - col_w
        ok_w = (d_w < WINDOW) & valid_row
        s_w = _dot(q16, win_ref[0, 0, kh].astype(BF16)) * ATT_SCALE + _row_bias(_t5_bucket(d_w), rb_ref, kh, g16)
        s_w = jnp.where(ok_w, s_w, NEG)
        mw = s_w.max(-1, keepdims=True)
        cols = []
        for c in range(t_valid):
            kn = kvn_ref[0, c:c + 1, 512 + 64 * kh:512 + 64 * kh + 64]
            sc = jnp.sum(q16f * kn, axis=-1, keepdims=True) * ATT_SCALE + _row_bias(_t5_bucket(t16 - c), rb_ref, kh, g16)
            ok = (t16 >= c) & valid_row
            sc = jnp.where(ok, sc, NEG)
            cols.append((sc, ok))
            mw = jnp.maximum(mw, sc)
        e_w = jnp.where(ok_w, jnp.exp(s_w - mw), 0.0)
        l_w = e_w.sum(-1, keepdims=True)
        acc_w = _dot_nt(e_w.astype(BF16), win_ref[0, 1, kh].astype(BF16))
        for c, (sc, ok) in enumerate(cols):
            e = jnp.where(ok, jnp.exp(sc - mw), 0.0)
            l_w = l_w + e
            acc_w = acc_w + e * kvn_ref[0, c:c + 1, 640 + 64 * kh:640 + 64 * kh + 64]
        o_wins.append(acc_w / jnp.maximum(l_w, 1e-30))

    for kh in range(ATT_KV_HEADS):
        def wait_pair(i, carry, kh=kh):
            ck, cv = kv_copies(kh, 0, i)
            ck.wait()
            cv.wait()
            return carry
        lax.fori_loop(0, counts[kh], wait_pair, 0)
    rec_copy.wait()

    lane_c = lax.broadcasted_iota(I32, (1, ncat), 1)
    shift_c = t16 + 4 * ((lane_c % LANES) // SEL_BLOCK)
    lane = lax.broadcasted_iota(I32, (1, LANES), 1)
    shift = t16 + 4 * (lane // SEL_BLOCK)
    for kh in range(ATT_KV_HEADS):
        q16f = q16fs[kh]
        q16 = q16f.astype(BF16)
        c_far = jnp.where(g16 == 1, rb_ref[N_BUCKETS - 1, 2 * kh + 1], rb_ref[N_BUCKETS - 1, 2 * kh])

        u_vec = jnp.concatenate([jnp.full((1, LANES), slot_info(kh, i, 1), I32) for i in range(MAX_SEL_PAGES)], axis=1)
        ok_c = (jnp.bitwise_and(jnp.right_shift(u_vec, shift_c), 1) == 1) & valid_row
        s_c = jnp.where(ok_c, _dot(q16, kcat_ref[kh].astype(BF16)) * ATT_SCALE + c_far, NEG)
        u_rec = slot_info(kh, 0, 3)
        d_r = (P + t16) - ((NP - 1) * LANES + lane)
        ok_r = (jnp.bitwise_and(jnp.right_shift(u_rec, shift), 1) == 1) & valid_row
        s_r = _dot(q16, rec_ref[0, kh].astype(BF16)) * ATT_SCALE + _row_bias(_t5_bucket(d_r), rb_ref, kh, g16)
        s_r = jnp.where(ok_r, s_r, NEG)
        u_new = slot_info(kh, 0, 4)
        sel_new = jnp.bitwise_and(jnp.right_shift(u_new, t16), 1) == 1
        m_fin = jnp.maximum(s_c.max(-1, keepdims=True), s_r.max(-1, keepdims=True))
        cols = []
        for c in range(t_valid):
            kn = kvn_ref[0, c:c + 1, 256 + 64 * kh:256 + 64 * kh + 64]
            sc = jnp.sum(q16f * kn, axis=-1, keepdims=True) * ATT_SCALE + _row_bias(_t5_bucket(t16 - c), rb_ref, kh, g16)
            ok = sel_new & (t16 >= c) & valid_row
            sc = jnp.where(ok, sc, NEG)
            cols.append((sc, ok))
            m_fin = jnp.maximum(m_fin, sc)
        e_c = jnp.where(ok_c, jnp.exp(s_c - m_fin), 0.0)
        e_r = jnp.where(ok_r, jnp.exp(s_r - m_fin), 0.0)
        l_fin = e_c.sum(-1, keepdims=True) + e_r.sum(-1, keepdims=True)
        acc = (_dot_nt(e_c.astype(BF16), vcat_ref[kh].astype(BF16))
               + _dot_nt(e_r.astype(BF16), rec_ref[1, kh].astype(BF16)))
        for c, (sc, ok) in enumerate(cols):
            e = jnp.where(ok, jnp.exp(sc - m_fin), 0.0)
            l_fin = l_fin + e
            acc = acc + e * kvn_ref[0, c:c + 1, 384 + 64 * kh:384 + 64 * kh + 64]
        o_sel = acc / jnp.maximum(l_fin, 1e-30)
        o_win = o_wins[kh]

        for g in range(2):
            h = 2 * kh + g
            out = (gates[:, h:h + 1] * ocmp_ref[0, :, 64 * h:64 * h + 64]
                   + gates[:, 4 + h:5 + h] * o_sel[8 * g:8 * g + 8]
                   + gates[:, 8 + h:9 + h] * o_win[8 * g:8 * g + 8])
            y_ref[0, :, 64 * h:64 * h + 64] = out.astype(BF16)


def _nsa_s3(page_flat, slots_flat, rel_bias, hs3, ocmp, cache_t, win_t, layer, P, NP, t_valid):
    B, T, _ = hs3.shape
    wb = win_t.shape[-1]
    grid_spec = pltpu.PrefetchScalarGridSpec(
        num_scalar_prefetch=2,
        grid=(B,),
        in_specs=[_smem_spec(),
                  pl.BlockSpec((1, T, 256), lambda b, *_: (b, 0, C_Q // 256)),
                  pl.BlockSpec((1, T, LANES), lambda b, *_: (b, 0, C_SM // LANES)),
                  pl.BlockSpec((1, T, 768), lambda b, *_: (b, 0, C_KV // 768)),
                  pl.BlockSpec((1, 8, 256), lambda b, *_: (b, 0, 0)),
                  pl.BlockSpec(memory_space=pl.ANY),
                  pl.BlockSpec((1, 2, 2, HEAD_DIM, wb), lambda b, *_: (b, 0, 0, 0, 0))],
        out_specs=pl.BlockSpec((1, T, 256), lambda b, *_: (b, 0, 0)),
        scratch_shapes=[pltpu.VMEM((2, HEAD_DIM, MAX_SEL_PAGES * LANES), F32),
                        pltpu.VMEM((2, HEAD_DIM, MAX_SEL_PAGES * LANES), F32),
                        pltpu.VMEM((2, 2, HEAD_DIM, LANES), F32),
                        pltpu.SemaphoreType.DMA((2,))],
    )
    return pl.pallas_call(
        functools.partial(_nsa_s3_body, layer=layer, P=P, NP=NP, t_valid=t_valid),
        grid_spec=grid_spec,
        out_shape=jax.ShapeDtypeStruct((B, T, 256), BF16),
        compiler_params=_cparams("arbitrary"),
        name="nsa_sample_select_window",
    )(page_flat, slots_flat, rel_bias, hs3, hs3, hs3, ocmp, cache_t, win_t)


def _per_layer_specs(depth, block, index_of_step):
    def spec(k):
        def index_map(l, i):
            idx = index_of_step(i)
            return tuple(jnp.where(l == k, v, 0) if n < 2 else v for n, v in enumerate(idx))
        return pl.BlockSpec(block, index_map)
    return [spec(k) for k in range(depth)]


def _kv_pages_body(*refs):
    o_ref = refs[-1]
    for k, h_ref in enumerate(refs[:-1]):
        @pl.when(pl.program_id(0) == k)
        def _(h_ref=h_ref):
            for j in range(o_ref.shape[1]):
                for c in range(4):
                    x = h_ref[0, LANES * j:LANES * (j + 1), LANES * c:LANES * (c + 1)]
                    o_ref[0, j, c] = x.T.reshape(ATT_KV_HEADS, HEAD_DIM, LANES)


def _kv_pages_t(hs):
    depth = len(hs)
    B, T, _ = hs[0].shape
    rows = min(8 * LANES, T)
    per_b = T // rows
    return pl.pallas_call(
        _kv_pages_body,
        grid=(depth, B * per_b),
        in_specs=_per_layer_specs(depth, (1, rows, 512), lambda i: (i // per_b, i % per_b, 0)),
        out_specs=pl.BlockSpec((1, rows // LANES, 4, ATT_KV_HEADS, HEAD_DIM, LANES), lambda l, i: (l, i, 0, 0, 0, 0)),
        out_shape=jax.ShapeDtypeStruct((depth, B * T // LANES, 4, ATT_KV_HEADS, HEAD_DIM, LANES), F32),
        compiler_params=_cparams("arbitrary", "arbitrary"),
        name="kv_pages_token_minor",
    )(*hs)


def _win_rows_body(*refs):
    o_ref = refs[-1]
    for k, h_ref in enumerate(refs[:-1]):
        @pl.when(pl.program_id(0) == k)
        def _(h_ref=h_ref):
            for c in range(2):
                x = h_ref[0, :, LANES * c:LANES * (c + 1)]
                o_ref[0, 0, c] = x.T.reshape(ATT_KV_HEADS, HEAD_DIM, x.shape[0])


def _win_rows_t(hs, wrows):
    depth = len(hs)
    B, T, _ = hs[0].shape
    last = T // wrows - 1
    return pl.pallas_call(
        _win_rows_body,
        grid=(depth, B),
        in_specs=_per_layer_specs(depth, (1, wrows, 256), lambda b: (b, last, (C_KV + 512) // 256)),
        out_specs=pl.BlockSpec((1, 1, 2, ATT_KV_HEADS, HEAD_DIM, wrows), lambda l, b: (l, b, 0, 0, 0, 0)),
        out_shape=jax.ShapeDtypeStruct((depth, B, 2, ATT_KV_HEADS, HEAD_DIM, wrows), F32),
        compiler_params=_cparams("arbitrary", "arbitrary"),
        name="win_rows_token_minor",
    )(*hs)


def _blockdiag2(w):
    z = jnp.zeros_like(w)
    return jnp.concatenate([jnp.concatenate([w, z], axis=-1), jnp.concatenate([z, w], axis=-1)], axis=-2)


def _prep_layer(w_in, pool_w, cmp_w1, cmp_w2, alog, dtb):
    glu, pool, q, kv, gate, gqkv, z, a, b = jnp.split(w_in, [512, 768, 1024, 1792, 1804, 2572, 2828, 2832], axis=1)
    pad = jnp.zeros((D_MODEL, IN_PAD - C_SM - 20), F32)
    w_in_p = jnp.concatenate([kv, gqkv, glu, pool, q, z, gate, a, b, pad], axis=1).astype(BF16)
    wblk = jnp.zeros((POOL_CH, POOL_CH), F32)
    for gi in range(4):
        wblk = wblk.at[64 * gi:64 * gi + 64, 64 * gi:64 * gi + 64].set(pool_w[gi])
    w1 = cmp_w1.reshape(2, CMP_BLOCK, HEAD_DIM, HEAD_DIM)
    wcat = jnp.concatenate([_blockdiag2(w1[:, :CMP_STRIDE]), _blockdiag2(w1[:, CMP_STRIDE:])], axis=-1).astype(BF16)
    wcat = wcat.reshape(2, CMP_STRIDE // 2, 2 * LANES, 2 * LANES)
    w2bd = _blockdiag2(cmp_w2).astype(BF16)
    lane_pad = lambda v: jnp.zeros((1, LANES), F32).at[0, SM_A:SM_A + GDN_HEADS].set(v)
    return w_in_p, wblk.astype(BF16), wcat, w2bd, lane_pad(alog), lane_pad(dtb)


def _cover_matrix(nsub, n_sel, cols):
    n = jnp.arange(nsub)[:, None] * CMP_STRIDE
    j = jnp.arange(cols)[None, :] * SEL_BLOCK
    cov = (n < j + SEL_BLOCK) & (n + CMP_BLOCK > j) & (jnp.arange(cols)[None, :] < n_sel) & (jnp.arange(nsub)[:, None] < nsub - 1)
    return cov.astype(BF16)


def kernel(x_prompt, x_sample, cache_nsa_kv, cache_win_kv, state_conv, state_pool, state_gdn_conv, state_gdn,
           page_table, w_in, conv_dw, conv_dw_b, conv_ln_g, conv_ln_b, conv_pw, pool_w, pool_scale,
           cmp_pe, cmp_w1, cmp_w2, gdn_conv_w, gdn_a_log, gdn_dt_bias, gdn_norm_g,
           w_out, ln1_g, ln1_b, w_up, w_down, ln2_g, ln2_b, rel_bias):
    depth = w_in.shape[0]
    BP, T, _ = x_prompt.shape
    BS, TS, _ = x_sample.shape
    NP = page_table.shape[1]
    page = cache_nsa_kv.shape[2]
    P = NP * page
    TSP = 8
    assert page == LANES and TS <= TSP and TS < CMP_STRIDE and T % QT == 0 and P % SEL_BLOCK == 0

    cache_t = jnp.transpose(cache_nsa_kv, (0, 1, 3, 4, 5, 2))
    win_t = jnp.transpose(cache_win_kv, (0, 1, 3, 4, 5, 2))
    page_flat = page_table.reshape(-1)

    tabw, tabc = _bias_tables(rel_bias, T)
    pterm = _peterm(cmp_pe.reshape(depth * 2, 1, CMP_BLOCK * HEAD_DIM), cmp_w1.reshape(depth * 2, CMP_BLOCK * HEAD_DIM, HEAD_DIM))
    pterm = pterm[:, 0:1, :].reshape(depth, 2, 1, HEAD_DIM)
    pterm = jnp.concatenate([pterm, pterm], axis=-1)

    ns_p, nsel_p = T // CMP_STRIDE, T // SEL_BLOCK
    covert_p = _cover_matrix(ns_p, nsel_p, nsel_p).T
    ns_s = P // CMP_STRIDE
    nsel_s = P // SEL_BLOCK + 1
    nselp_s = -(-nsel_s // LANES) * LANES
    cover_s = _cover_matrix(ns_s + 1, nsel_s, nselp_s)[:ns_s]
    jj = jnp.arange(nselp_s)[:, None]
    pp = jnp.arange(2 * LANES)[None, :]
    pair = (((jj // 2 == pp) & (jj < 2 * NP)) | ((jj == 2 * NP) & (pp == NP))).astype(BF16)

    yp = x_prompt.reshape(BP * T, D_MODEL)
    ys = jnp.pad(x_sample, ((0, 0), (0, TSP - TS), (0, 0))).reshape(BS * TSP, D_MODEL)
    zeros = lambda *s: jnp.zeros(s, F32)
    outs_p = [[] for _ in range(6)]
    outs_s = [[] for _ in range(6)]
    hps = []
    preps = [_prep_layer(w_in[l], pool_w[l], cmp_w1[l], cmp_w2[l], gdn_a_log[l], gdn_dt_bias[l]) for l in range(depth)]
    hcs = [_nsa_s1(page_flat, cache_t, preps[l][2], l, BS, NP) for l in range(depth)]
    for l in range(depth):
        w_in_p, wblk, wcat, w2bd, alog_l, dtb_l = preps[l]
        lw = dict(w_out=w_out[l].astype(BF16), ln1_g=ln1_g[l][None], ln1_b=ln1_b[l][None],
                  w_up=w_up[l].astype(BF16), w_down=w_down[l].astype(BF16), ln2_g=ln2_g[l][None], ln2_b=ln2_b[l][None])
        conv_w = (conv_dw[l], conv_dw_b[l][None], conv_ln_g[l][None], conv_ln_b[l][None], conv_pw[l].astype(BF16))
        gdn_w = (gdn_conv_w[l], alog_l, dtb_l, gdn_norm_g[l][None])

        hp = _proj_in(yp, w_in_p, min(1024, BP * T)).reshape(BP, T, IN_PAD)
        m_conv, conv_new = _conv_mixer(hp, zeros(BP, CONV_WIDTH - 1, CONV_CH), *conv_w, t_valid=T)
        m_pool, pool_new = _pool_mixer(hp, zeros(BP, POOL_BUF, POOL_CH), wblk, pool_scale[l][None], t_valid=T, offset=0)
        m_att = _nsa_prompt(hp, rel_bias, wcat, w2bd, pterm[l], tabw, tabc, covert_p)
        m_gdn, gbuf_new, s_new = _gdn_mixer(hp, zeros(BP, GDN_CONV - 1, GDN_QKV), zeros(BP, GDN_HEADS, GDN_DK, GDN_DV),
                                            *gdn_w, t_valid=T)
        mixes = [m.reshape(BP * T, GROUP_WIDTH) for m in (m_conv, m_pool, m_att, m_gdn)]
        yp = _out_ffn(yp, mixes, lw, 512)
        hps.append(hp)
        for lst, arr in zip(outs_p[2:], (conv_new, pool_new, gbuf_new, s_new)):
            lst.append(arr)

        hs = _proj_in(ys, w_in_p, BS * TSP).reshape(BS, TSP, IN_PAD)
        m_conv, conv_new = _conv_mixer(hs, state_conv[l], *conv_w, t_valid=TS)
        m_pool, pool_new = _pool_mixer(hs, state_pool[l], wblk, pool_scale[l][None], t_valid=TS, offset=P)
        ocmp, slots = _nsa_s2(hs, hcs[l], rel_bias, w2bd, pterm[l], cover_s, pair, P, TS)
        m_att = _nsa_s3(page_flat, slots[:, :, :SLOT_INFO].reshape(-1), rel_bias, hs, ocmp, cache_t, win_t[l], l, P, NP, TS)
        m_gdn, gbuf_new, s_new = _gdn_mixer(hs, state_gdn_conv[l], state_gdn[l], *gdn_w, t_valid=TS)
        mixes = [m.reshape(BS * TSP, GROUP_WIDTH) for m in (m_conv, m_pool, m_att, m_gdn)]
        ys = _out_ffn(ys, mixes, lw, BS * TSP)
        kv_new = hs[:, :TS, C_KV:C_KV + 768].reshape(BS, TS, 6, ATT_KV_HEADS, HEAD_DIM)
        outs_s[0].append(kv_new[:, :, :4])
        kw_all = jnp.concatenate([cache_win_kv[l], kv_new[:, :, 4:]], axis=1)
        outs_s[1].append(kw_all[:, -min(WINDOW, kw_all.shape[1]):])
        for lst, arr in zip(outs_s[2:], (conv_new, pool_new, gbuf_new, s_new)):
            lst.append(arr)

    p_nsa = jnp.transpose(_kv_pages_t(hps), (0, 1, 5, 2, 3, 4))
    p_win = jnp.transpose(_win_rows_t(hps, min(WINDOW, T)), (0, 1, 5, 2, 3, 4))
    p_rest = [p_win] + [jnp.stack(a) for a in outs_p[2:]]
    s_all = [jnp.stack(a) for a in outs_s]
    y_s = ys.reshape(BS, TSP, D_MODEL)[:, :TS]
    return (yp.reshape(BP, T, D_MODEL), y_s, p_nsa, *p_rest, *s_all)
```

```python
import functools
import math

import jax
import jax.numpy as jnp
from jax import lax
from jax.experimental import pallas as pl
from jax.experimental.pallas import tpu as pltpu

F32 = jnp.float32
BF16 = jnp.bfloat16
I32 = jnp.int32

D_MODEL = 1024
GROUP_WIDTH = 256
CONV_CH = 256
CONV_WIDTH = 31
POOL_CH = 256
POOL_WINDOWS = (2, 4, 8, 16)
POOL_BUF = 15
ATT_HEADS = 4
ATT_KV_HEADS = 2
HEAD_DIM = 64
ATT_SCALE = HEAD_DIM ** -0.5
CMP_STRIDE = 16
CMP_BLOCK = 32
SEL_BLOCK = 64
SEL_TOPN = 16
WINDOW = 512
N_BUCKETS = 32
GDN_HEADS = 4
GDN_DK = 64
GDN_DV = 64
GDN_QKV = 768
GDN_CONV = 4
GDN_CHUNK = 64
D_FF = 4096
DEPTH = 2
DN_ALPHA = (2 * DEPTH) ** 0.25
LN_EPS = 1e-5
NEG = -1e30
FORCE = 1e4

LANES = 128
VMEM_LIMIT_BYTES = 56 * 1024 * 1024

C_KV, C_GQKV, C_GLU, C_POOL, C_Q, C_Z, C_SM = 0, 768, 1536, 2048, 2304, 2560, 2816
IN_PAD = 2944
SM_GATE, SM_A, SM_B = 0, 12, 16


def _cparams(*sem):
    return pltpu.CompilerParams(dimension_semantics=sem, vmem_limit_bytes=VMEM_LIMIT_BYTES)


def _const_spec(shape):
    nd = len(shape)
    return pl.BlockSpec(shape, lambda *_: (0,) * nd, pipeline_mode=pl.Buffered(1))


def _smem_spec():
    return pl.BlockSpec(memory_space=pltpu.SMEM)


def _sigmoid(x):
    return 0.5 * jnp.tanh(0.5 * x) + 0.5


def _silu(x):
    h = 0.5 * x
    return h * jnp.tanh(h) + h


def _layer_norm(y, g, b):
    mu = jnp.mean(y, axis=-1, keepdims=True)
    yc = y - mu
    var = jnp.mean(yc * yc, axis=-1, keepdims=True)
    return yc * lax.rsqrt(var + LN_EPS) * g + b


def _dot(a, b):
    return jnp.dot(a, b, preferred_element_type=F32)


def _dot_nt(a, b):
    return lax.dot_general(a, b, (((1,), (1,)), ((), ())), preferred_element_type=F32)


def _dot_tn(a, b):
    return lax.dot_general(a, b, (((0,), (0,)), ((), ())), preferred_element_type=F32)


def _split3(x):
    x1 = x.astype(BF16)
    r = x - x1.astype(F32)
    x2 = r.astype(BF16)
    x3 = (r - x2.astype(F32)).astype(BF16)
    return x1, x2, x3


def _t5_bucket(d):
    d = jnp.maximum(d, 0)
    logd = jnp.log(jnp.maximum(d, 1).astype(F32) / 16.0) / math.log(8.0)
    large = jnp.minimum(16 + (logd * 16.0).astype(I32), N_BUCKETS - 1)
    return jnp.where(d < 16, d, large)


def _bias_lookup(bk, value_of_bucket):
    out = jnp.zeros(bk.shape, F32)
    for k in range(N_BUCKETS):
        out = jnp.where(bk == k, value_of_bucket(k), out)
    return out


def _masked_softmax_parts(parts):
    ss = [jnp.where(ok, s, NEG) for s, ok in parts]
    mx = ss[0].max(-1, keepdims=True)
    for s in ss[1:]:
        mx = jnp.maximum(mx, s.max(-1, keepdims=True))
    es = [jnp.where(ok, jnp.exp(s - mx), 0.0) for s, (_, ok) in zip(ss, parts)]
    tot = es[0].sum(-1, keepdims=True)
    for e in es[1:]:
        tot = tot + e.sum(-1, keepdims=True)
    inv = 1.0 / jnp.maximum(tot, 1e-30)
    return [e * inv for e in es]


def _proj_in_body(x_ref, w_ref, o_ref):
    xb = x_ref[...].astype(BF16)
    for a in range(0, IN_PAD, 512):
        b = min(a + 512, IN_PAD)
        o_ref[:, a:b] = _dot(xb, w_ref[:, a:b])


def _proj_in(x, w, tm):
    n = x.shape[0]
    return pl.pallas_call(
        _proj_in_body,
        grid=(n // tm,),
        in_specs=[pl.BlockSpec((tm, D_MODEL), lambda i: (i, 0)), _const_spec((D_MODEL, IN_PAD))],
        out_specs=pl.BlockSpec((tm, IN_PAD), lambda i: (i, 0)),
        out_shape=jax.ShapeDtypeStruct((n, IN_PAD), F32),
        compiler_params=_cparams("arbitrary"),
        name="proj_in",
    )(x, w)


FF_CHUNK = 1024


def _out_ffn_body(x_ref, m0_ref, m1_ref, m2_ref, m3_ref, wo_ref, g1_ref, b1_ref, wu_ref, wd_ref, g2_ref, b2_ref, o_ref):
    acc = _dot(m0_ref[...], wo_ref[0:256, :])
    acc += _dot(m1_ref[...], wo_ref[256:512, :])
    acc += _dot(m2_ref[...], wo_ref[512:768, :])
    acc += _dot(m3_ref[...], wo_ref[768:1024, :])
    x1 = _layer_norm(DN_ALPHA * x_ref[...] + acc, g1_ref[...], b1_ref[...])
    xb = x1.astype(BF16)
    acc = jnp.zeros(x1.shape, F32)
    for c in range(0, D_FF, FF_CHUNK):
        h = _dot(xb, wu_ref[:, c:c + FF_CHUNK])
        a = jnp.square(jnp.maximum(h, 0.0)).astype(BF16)
        acc += _dot(a, wd_ref[c:c + FF_CHUNK, :])
    o_ref[...] = _layer_norm(DN_ALPHA * x1 + acc, g2_ref[...], b2_ref[...])


def _out_ffn(x, mixes, lw, tm):
    n = x.shape[0]
    row = lambda i: (i, 0)
    vec = _const_spec((1, D_MODEL))
    return pl.pallas_call(
        _out_ffn_body,
        grid=(n // tm,),
        in_specs=[pl.BlockSpec((tm, D_MODEL), row)] + [pl.BlockSpec((tm, GROUP_WIDTH), row)] * 4
        + [_const_spec((D_MODEL, D_MODEL)), vec, vec, _const_spec((D_MODEL, D_FF)), _const_spec((D_FF, D_MODEL)), vec, vec],
        out_specs=pl.BlockSpec((tm, D_MODEL), row),
        out_shape=jax.ShapeDtypeStruct((n, D_MODEL), F32),
        compiler_params=_cparams("arbitrary"),
        name="proj_out_ffn",
    )(x, *mixes, lw["w_out"], lw["ln1_g"], lw["ln1_b"], lw["w_up"], lw["w_down"], lw["ln2_g"], lw["ln2_b"])


CONV_PAD = 32


def _conv_body(h_ref, buf_ref, dw_ref, dwb_ref, g_ref, b_ref, pw_ref, y_ref, new_ref, full_ref, *, T, t_valid):
    hh = h_ref[0]
    full_ref[0:8, :] = jnp.zeros((8, CONV_CH), F32)
    full_ref[2:CONV_PAD, :] = buf_ref[0]
    full_ref[CONV_PAD:CONV_PAD + T, :] = hh[:, :CONV_CH] * _sigmoid(hh[:, CONV_CH:])
    new_ref[0] = full_ref[t_valid + 2:t_valid + CONV_PAD, :]
    rc = min(T, 128)

    def chunk(c, carry):
        base = pl.multiple_of(c * rc, rc)
        win = full_ref[pl.ds(base, rc + CONV_PAD), :]
        acc = jnp.zeros((rc, CONV_CH), F32) + dwb_ref[...]
        for r in range(8):
            shifted = win[r:r + (rc + CONV_PAD - r) // 8 * 8, :]
            for k in range(CONV_WIDTH):
                if (2 + k) % 8 == r:
                    a = (2 + k) // 8 * 8
                    acc = acc + dw_ref[k:k + 1, :] * shifted[a:a + rc, :]
        y = _silu(_layer_norm(acc, g_ref[...], b_ref[...]))
        y_ref[0, pl.ds(base, rc), :] = _dot(y.astype(BF16), pw_ref[...]).astype(BF16)
        return carry

    lax.fori_loop(0, T // rc, chunk, 0)


def _conv_mixer(h3, buf, dw, dwb, g, b, pw, t_valid):
    B, T, _ = h3.shape
    return pl.pallas_call(
        functools.partial(_conv_body, T=T, t_valid=t_valid),
        grid=(B,),
        in_specs=[pl.BlockSpec((1, T, 2 * CONV_CH), lambda i: (i, 0, C_GLU // (2 * CONV_CH))),
                  pl.BlockSpec((1, CONV_WIDTH - 1, CONV_CH), lambda i: (i, 0, 0)),
                  _const_spec((CONV_WIDTH, CONV_CH)), _const_spec((1, CONV_CH)), _const_spec((1, CONV_CH)),
                  _const_spec((1, CONV_CH)), _const_spec((CONV_CH, CONV_CH))],
        out_specs=[pl.BlockSpec((1, T, CONV_CH), lambda i: (i, 0, 0)),
                   pl.BlockSpec((1, CONV_WIDTH - 1, CONV_CH), lambda i: (i, 0, 0))],
        out_shape=[jax.ShapeDtypeStruct((B, T, CONV_CH), BF16),
                   jax.ShapeDtypeStruct((B, CONV_WIDTH - 1, CONV_CH), F32)],
        scratch_shapes=[pltpu.VMEM((T + CONV_PAD, CONV_CH), F32)],
        compiler_params=_cparams("arbitrary"),
        name="conv_mixer",
    )(h3, buf, dw, dwb, g, b, pw)


POOL_PAD = 16


def _pool_body(h_ref, buf_ref, w_ref, sc_ref, y_ref, new_ref, full_ref, *, T, t_valid, offset):
    full_ref[0:8, :] = jnp.zeros((8, POOL_CH), F32)
    full_ref[1:POOL_PAD, :] = buf_ref[0]
    full_ref[POOL_PAD:POOL_PAD + T, :] = h_ref[0]
    new_ref[0] = full_ref[t_valid + 1:t_valid + POOL_PAD, :]
    rc = min(T, 128)
    lane = lax.broadcasted_iota(I32, (1, POOL_CH), 1)
    group = lane // (POOL_CH // len(POOL_WINDOWS))
    wl = jnp.where(group == 0, 2, jnp.where(group == 1, 4, jnp.where(group == 2, 8, 16)))

    def chunk(c, carry):
        base = pl.multiple_of(c * rc, rc)
        win = full_ref[pl.ds(base, rc + POOL_PAD), :]
        x0 = win[POOL_PAD:POOL_PAD + rc, :]
        sums = {}
        acc = x0
        for i in range(1, 16):
            acc = acc + win[POOL_PAD - i:POOL_PAD - i + rc, :]
            if i + 1 in POOL_WINDOWS:
                sums[i + 1] = acc
        sel = jnp.where(group == 0, sums[2], jnp.where(group == 1, sums[4], jnp.where(group == 2, sums[8], sums[16])))
        pos = offset + base + lax.broadcasted_iota(I32, (rc, 1), 0)
        cnt = jnp.minimum(pos + 1, wl).astype(F32)
        d = sel / cnt - x0
        y_ref[0, pl.ds(base, rc), :] = (_dot(d.astype(BF16), w_ref[...]) * sc_ref[...]).astype(BF16)
        return carry

    lax.fori_loop(0, T // rc, chunk, 0)


def _pool_mixer(h3, buf, wblk, scale, t_valid, offset):
    B, T, _ = h3.shape
    return pl.pallas_call(
        functools.partial(_pool_body, T=T, t_valid=t_valid, offset=offset),
        grid=(B,),
        in_specs=[pl.BlockSpec((1, T, POOL_CH), lambda i: (i, 0, C_POOL // POOL_CH)),
                  pl.BlockSpec((1, POOL_BUF, POOL_CH), lambda i: (i, 0, 0)),
                  _const_spec((POOL_CH, POOL_CH)), _const_spec((1, POOL_CH))],
        out_specs=[pl.BlockSpec((1, T, POOL_CH), lambda i: (i, 0, 0)),
                   pl.BlockSpec((1, POOL_BUF, POOL_CH), lambda i: (i, 0, 0))],
        out_shape=[jax.ShapeDtypeStruct((B, T, POOL_CH), BF16),
                   jax.ShapeDtypeStruct((B, POOL_BUF, POOL_CH), F32)],
        scratch_shapes=[pltpu.VMEM((T + POOL_PAD, POOL_CH), F32)],
        compiler_params=_cparams("arbitrary"),
        name="pool_mixer",
    )(h3, buf, wblk, scale)


GDN_PAD = 8
CK = GDN_CHUNK


def _gdn_body(qkv_ref, z_ref, sm_ref, buf_ref, s0_ref, cw_ref, alog_ref, dtb_ref, ng_ref,
              y_ref, newbuf_ref, sout_ref, full_ref, c_ref, g_ref, bt_ref, gi_ref, bi_ref, u_ref, w_ref, a_ref,
              qg_ref, kdt_ref, s_ref, *, T, Tp, t_valid):
    full_ref[0:8, :] = jnp.zeros((8, GDN_QKV), F32)
    full_ref[5:GDN_PAD, :] = buf_ref[0]
    full_ref[GDN_PAD:GDN_PAD + T, :] = qkv_ref[0]
    newbuf_ref[0] = full_ref[t_valid + 5:t_valid + GDN_PAD, :]
    if Tp > t_valid:
        c_ref[...] = jnp.zeros((Tp, GDN_QKV), F32)
        g_ref[...] = jnp.zeros((Tp, LANES), F32)
        bt_ref[...] = jnp.zeros((Tp, LANES), F32)

    rc = min(t_valid, 128)

    def conv_chunk(c, carry):
        base = pl.multiple_of(c * rc, rc)
        win = full_ref[pl.ds(base, rc + GDN_PAD), :] if rc % 8 == 0 else full_ref[0:rc + GDN_PAD, :]
        acc = jnp.zeros((rc, GDN_QKV), F32)
        for k in range(GDN_CONV):
            acc = acc + cw_ref[k:k + 1, :] * win[5 + k:5 + k + rc, :]
        sm = sm_ref[0, pl.ds(base, rc), :] if rc % 8 == 0 else sm_ref[0, 0:rc, :]
        x = sm + dtb_ref[...]
        softplus = jnp.maximum(x, 0.0) + jnp.log1p(jnp.exp(-jnp.abs(x)))
        gv = -jnp.exp(alog_ref[...]) * softplus
        bv = _sigmoid(sm)
        if rc % 8 == 0:
            c_ref[pl.ds(base, rc), :] = _silu(acc)
            g_ref[pl.ds(base, rc), :] = gv
            bt_ref[pl.ds(base, rc), :] = bv
        else:
            c_ref[0:rc, :] = _silu(acc)
            g_ref[0:rc, :] = gv
            bt_ref[0:rc, :] = bv
        return carry

    lax.fori_loop(0, t_valid // rc, conv_chunk, 0)

    HW = GDN_HEADS * GDN_DK
    lane = lax.broadcasted_iota(I32, (1, HW), 1)
    hmask = [jnp.where(lane // GDN_DK == h, 1.0, 0.0).astype(BF16) for h in range(GDN_HEADS)]
    row = lax.broadcasted_iota(I32, (CK, 1), 0)
    jl = lane % CK
    incl = row >= jl
    strict = row > jl
    eye_all = jnp.where(row == jl, 1.0, 0.0)
    er = lax.broadcasted_iota(I32, (LANES, HW), 0)
    ec = lax.broadcasted_iota(I32, (LANES, HW), 1) // GDN_DK
    exp_a = jnp.where(er == ec + SM_A, 1.0, 0.0).astype(BF16)
    exp_b = jnp.where(er == ec + SM_B, 1.0, 0.0).astype(BF16)
    br = lax.broadcasted_iota(I32, (HW, HW), 0) // GDN_DK
    bc = lax.broadcasted_iota(I32, (HW, HW), 1) // GDN_DK
    same_head = br == bc
    bones = jnp.where(same_head, 1.0, 0.0).astype(BF16)

    def blockdiag(x):
        return jnp.concatenate([x * m for m in hmask], axis=0)

    def expand3(x, e):
        x1, x2, x3 = _split3(x)
        return _dot(jnp.concatenate([x1, x2, x3], axis=1), jnp.concatenate([e, e, e], axis=0))

    def bd_dot_hl(a, b):
        ah = a.astype(BF16)
        al = (a - ah.astype(F32)).astype(BF16)
        bh = b.astype(BF16)
        bl = (b - bh.astype(F32)).astype(BF16)
        bdh = blockdiag(bh)
        return _dot(jnp.concatenate([ah, al, ah], axis=1), jnp.concatenate([bdh, bdh, blockdiag(bl)], axis=0))

    rb = min(Tp, 4 * CK)
    row_in_chunk = lax.broadcasted_iota(I32, (rb, 1), 0) % CK

    def prep(c, carry):
        r0 = pl.multiple_of(c * rb, rb)
        g = g_ref[pl.ds(r0, rb), :]
        for s in (1, 2, 4, 8, 16, 32):
            g = g + jnp.where(row_in_chunk >= s, jnp.roll(g, s, axis=0), 0.0)
        gi_ref[pl.ds(r0, rb), :] = expand3(g, exp_a)
        bi_ref[pl.ds(r0, rb), :] = expand3(bt_ref[pl.ds(r0, rb), :], exp_b)
        for part, scale in ((0, GDN_DK ** -0.5), (1, 1.0)):
            x = c_ref[pl.ds(r0, rb), HW * part:HW * (part + 1)]
            ssq = expand3(x * x, bones)
            c_ref[pl.ds(r0, rb), HW * part:HW * (part + 1)] = x * lax.rsqrt(ssq + 1e-6) * scale
        return carry

    lax.fori_loop(0, Tp // rb, prep, 0)

    n_chunks = Tp // CK
    group = 8 if n_chunks % 8 == 0 else 1

    def solve(it, carry):
        r0s = [pl.multiple_of((it * group + k) * CK, CK) for k in range(group)]
        gi = [gi_ref[pl.ds(r0, CK), :] for r0 in r0s]
        kn = [c_ref[pl.ds(r0, CK), HW:2 * HW] for r0 in r0s]
        decay = [jnp.exp(jnp.where(incl, g - jnp.sum(eye_all * g, axis=0, keepdims=True), NEG)) for g in gi]
        kb = [k * bi_ref[pl.ds(r0, CK), :] for k, r0 in zip(kn, r0s)]
        kst = [blockdiag(k.astype(BF16)) for k in kn]
        pw = [-jnp.where(strict, _dot_nt(b.astype(BF16), s) * d, 0.0) for b, s, d in zip(kb, kst, decay)]
        tinv = [eye_all + p for p in pw]
        pw = [bd_dot_hl(p, p) for p in pw]
        for _ in range(4):
            both = [bd_dot_hl(jnp.concatenate([t, p], axis=0), p) for t, p in zip(tinv, pw)]
            tinv = [t + b[:CK] for t, b in zip(tinv, both)]
            pw = [b[CK:] for b in both]
        tinv = [t + bd_dot_hl(t, p) for t, p in zip(tinv, pw)]
        for k, r0 in enumerate(r0s):
            vb = c_ref[pl.ds(r0, CK), 2 * HW:3 * HW] * bi_ref[pl.ds(r0, CK), :]
            u_ref[pl.ds(r0, CK), :] = bd_dot_hl(tinv[k], vb)
        for k, r0 in enumerate(r0s):
            w_ref[pl.ds(r0, CK), :] = bd_dot_hl(tinv[k], kb[k] * jnp.exp(gi[k]))
        for k, r0 in enumerate(r0s):
            qn = c_ref[pl.ds(r0, CK), 0:HW]
            a_ref[pl.ds(r0, CK), :] = (_dot_nt(qn.astype(BF16), kst[k]) * decay[k]).astype(BF16)
            qg_ref[pl.ds(r0, CK), :] = (qn * jnp.exp(gi[k])).astype(BF16)
            kdt_ref[it * group + k] = (kn[k] * jnp.exp(gi[k][CK - 1:CK, :] - gi[k])).T.astype(BF16)
        return carry

    lax.fori_loop(0, n_chunks // group, solve, 0)

    s_ref[...] = jnp.zeros((HW, HW), F32)
    for h in range(GDN_HEADS):
        s_ref[GDN_DK * h:GDN_DK * (h + 1), GDN_DV * h:GDN_DV * (h + 1)] = s0_ref[0, h]
    rows_out = min(CK, T)
    ng_all = jnp.concatenate([ng_ref[...]] * GDN_HEADS, axis=1)

    def recur(c, carry):
        r0 = pl.multiple_of(c * CK, CK)
        s_all = s_ref[...]
        s_b = s_all.astype(BF16)
        vnew = u_ref[pl.ds(r0, CK), :] - _dot(w_ref[pl.ds(r0, CK), :].astype(BF16), s_b)
        vnb = vnew.astype(BF16)
        glast = gi_ref[pl.ds(r0 + CK - 1, 1), :]
        s_ref[...] = s_all * jnp.exp(glast) + jnp.where(same_head, _dot(kdt_ref[c], vnb), 0.0)
        u_ref[pl.ds(r0, CK), :] = _dot(qg_ref[pl.ds(r0, CK), :], s_b) + _dot(a_ref[pl.ds(r0, CK), :], blockdiag(vnb))
        return carry

    lax.fori_loop(0, Tp // CK, recur, 0)
    for h in range(GDN_HEADS):
        sout_ref[0, h] = s_ref[GDN_DK * h:GDN_DK * (h + 1), GDN_DV * h:GDN_DV * (h + 1)]

    def finish(c, carry):
        r0 = pl.multiple_of(c * rb, rb)
        o = u_ref[pl.ds(r0, rb), :]
        on = o * lax.rsqrt(expand3(o * o, bones) * (1.0 / GDN_DV) + LN_EPS) * ng_all
        if T >= CK:
            y_ref[0, pl.ds(r0, rb), :] = (on * _silu(z_ref[0, pl.ds(r0, rb), :])).astype(BF16)
        else:
            y_ref[0] = (on[0:rows_out] * _silu(z_ref[0])).astype(BF16)
        return carry

    lax.fori_loop(0, Tp // rb, finish, 0)


def _gdn_mixer(h3, buf, s0, cw, alog_l, dtb_l, ng, t_valid):
    B, T, _ = h3.shape
    Tp = -(-T // CK) * CK
    return pl.pallas_call(
        functools.partial(_gdn_body, T=T, Tp=Tp, t_valid=t_valid),
        grid=(B,),
        in_specs=[pl.BlockSpec((1, T, GDN_QKV), lambda i: (i, 0, C_GQKV // GDN_QKV)),
                  pl.BlockSpec((1, T, 256), lambda i: (i, 0, C_Z // 256)),
                  pl.BlockSpec((1, T, LANES), lambda i: (i, 0, C_SM // LANES)),
                  pl.BlockSpec((1, GDN_CONV - 1, GDN_QKV), lambda i: (i, 0, 0)),
                  pl.BlockSpec((1, GDN_HEADS, GDN_DK, GDN_DV), lambda i: (i, 0, 0, 0)),
                  _const_spec((GDN_CONV, GDN_QKV)), _const_spec((1, LANES)), _const_spec((1, LANES)),
                  _const_spec((1, GDN_DV))],
        out_specs=[pl.BlockSpec((1, T, 256), lambda i: (i, 0, 0)),
                   pl.BlockSpec((1, GDN_CONV - 1, GDN_QKV), lambda i: (i, 0, 0)),
                   pl.BlockSpec((1, GDN_HEADS, GDN_DK, GDN_DV), lambda i: (i, 0, 0, 0))],
        out_shape=[jax.ShapeDtypeStruct((B, T, 256), BF16),
                   jax.ShapeDtypeStruct((B, GDN_CONV - 1, GDN_QKV), F32),
                   jax.ShapeDtypeStruct((B, GDN_HEADS, GDN_DK, GDN_DV), F32)],
        scratch_shapes=[pltpu.VMEM((T + GDN_PAD, GDN_QKV), F32), pltpu.VMEM((Tp, GDN_QKV), F32),
                        pltpu.VMEM((Tp, LANES), F32), pltpu.VMEM((Tp, LANES), F32),
                        pltpu.VMEM((Tp, 256), F32), pltpu.VMEM((Tp, 256), F32),
                        pltpu.VMEM((Tp, 256), F32), pltpu.VMEM((Tp, 256), F32), pltpu.VMEM((Tp, 256), BF16),
                        pltpu.VMEM((Tp, 256), BF16), pltpu.VMEM((Tp // CK, GDN_HEADS * GDN_DK, CK), BF16),
                        pltpu.VMEM((GDN_HEADS * GDN_DK, GDN_HEADS * GDN_DV), F32)],
        compiler_params=_cparams("arbitrary"),
        name="gdn_mixer",
    )(h3, h3, h3, buf, s0, cw, alog_l, dtb_l, ng)


def _peterm_body(pe_ref, w1_ref, o_ref):
    pe = jnp.broadcast_to(pe_ref[0], (8, CMP_BLOCK * HEAD_DIM)).astype(BF16)
    o_ref[0] = _dot(pe, w1_ref[0].astype(BF16))


def _peterm(pe_flat, w1):
    n = pe_flat.shape[0]
    return pl.pallas_call(
        _peterm_body,
        grid=(n,),
        in_specs=[pl.BlockSpec((1, 1, CMP_BLOCK * HEAD_DIM), lambda i: (i, 0, 0)),
                  pl.BlockSpec((1, CMP_BLOCK * HEAD_DIM, HEAD_DIM), lambda i: (i, 0, 0))],
        out_specs=pl.BlockSpec((1, 8, HEAD_DIM), lambda i: (i, 0, 0)),
        out_shape=jax.ShapeDtypeStruct((n, 8, HEAD_DIM), F32),
        compiler_params=_cparams("arbitrary"),
        name="cmp_pe_term",
    )(pe_flat, w1)


QT = 128
WBAND = WINDOW + QT


def _tabw_body(rb_ref, o_ref):
    i = lax.broadcasted_iota(I32, (QT, WBAND), 0)
    j = lax.broadcasted_iota(I32, (QT, WBAND), 1)
    bk = _t5_bucket(WINDOW + i - j)
    for h in range(ATT_HEADS):
        o_ref[h] = _bias_lookup(bk, lambda k: rb_ref[k, h])


def _tabc_body(rb_ref, o_ref):
    p0 = pl.program_id(0) * QT
    ns = o_ref.shape[-1]
    t = p0 + lax.broadcasted_iota(I32, (QT, ns), 0)
    n = lax.broadcasted_iota(I32, (QT, ns), 1)
    bk = _t5_bucket(t - (n * CMP_STRIDE + CMP_BLOCK - 1))
    for h in range(ATT_HEADS):
        o_ref[h] = _bias_lookup(bk, lambda k: rb_ref[k, h])


def _bias_tables(rel_bias, T):
    ns = T // CMP_STRIDE
    tabw = pl.pallas_call(
        _tabw_body, in_specs=[_smem_spec()],
        out_shape=jax.ShapeDtypeStruct((ATT_HEADS, QT, WBAND), F32), name="bias_window_table")(rel_bias)
    tabc = pl.pallas_call(
        _tabc_body, grid=(T // QT,), in_specs=[_smem_spec()],
        out_specs=pl.BlockSpec((ATT_HEADS, QT, ns), lambda i: (0, i, 0)),
        out_shape=jax.ShapeDtypeStruct((ATT_HEADS, T, ns), F32),
        compiler_params=_cparams("arbitrary"), name="bias_cmp_table")(rel_bias)
    return tabw, tabc


def _compress_pre(load_rows, wcat_ref, c):
    acc = None
    for r in range(0, CMP_STRIDE, 2):
        lhs = jnp.concatenate([load_rows(r), load_rows(r + 1)], axis=1).astype(BF16)
        part = _dot(lhs, wcat_ref[c, r // 2])
        acc = part if acc is None else acc + part
    return acc


def _compress_finish(hcat, pt, w2):
    pre = hcat[:, :LANES] + jnp.roll(hcat[:, LANES:], -1, axis=0) + pt
    return _dot(_silu(pre).astype(BF16), w2)


def _topn_select(score, n_cols):
    j = lax.broadcasted_iota(I32, score.shape, 1)
    rank = jnp.zeros(score.shape, F32)
    for jp in range(n_cols):
        col = score[:, jp:jp + 1]
        ahead = (col > score) | ((col == score) & (jp < j))
        rank = rank + jnp.where(ahead, 1.0, 0.0)
    return jnp.where((rank < SEL_TOPN) & (score > 0.5 * NEG), 1.0, 0.0)


def _topn_select_rows(score_t, n_rows):
    j = lax.broadcasted_iota(I32, score_t.shape, 0)
    rank = jnp.zeros(score_t.shape, F32)
    for jp in range(n_rows):
        r = score_t[jp:jp + 1, :]
        ahead = (r > score_t) | ((r == score_t) & (jp < j))
        rank = rank + jnp.where(ahead, 1.0, 0.0)
    return jnp.where((rank < SEL_TOPN) & (score_t > 0.5 * NEG), 1.0, 0.0)


def _nsa_prompt_body(rb_ref, q_ref, sm_ref, kv_ref, wcat_ref, w2_ref, pt_ref, tabw_ref, tabc_ref, covert_ref,
                     y_ref, kvp_ref, kc_ref, vc_ref, cmp_ref, nsel_ref, *, T, FT):
    ns = T // CMP_STRIDE
    n_sel = T // SEL_BLOCK
    qt = pl.program_id(1)
    p0 = pl.multiple_of(qt * QT, QT)

    @pl.when(qt == 0)
    def _():
        kvp_ref[0:WINDOW, :] = jnp.zeros((WINDOW, 512), BF16)
        kvp_ref[WINDOW:WINDOW + T, :] = kv_ref[0, :, 256:768].astype(BF16)
        for c, dst in ((0, kc_ref), (1, vc_ref)):
            cmp_ref[c] = kv_ref[0, :, LANES * c:LANES * (c + 1)]
            hcat = _compress_pre(lambda r: cmp_ref[c, pl.ds(r, ns, stride=CMP_STRIDE), :], wcat_ref, c)
            dst[...] = _compress_finish(hcat, pt_ref[c], w2_ref[c]).astype(BF16)

    gates = _sigmoid(sm_ref[0])
    t = p0 + lax.broadcasted_iota(I32, (QT, 1), 0)
    n_i = lax.broadcasted_iota(I32, (1, ns), 1)
    ok_cmp = (t - (n_i * CMP_STRIDE + CMP_BLOCK - 1) >= 0) & (n_i < ns - 1)
    ok_cmp2 = jnp.concatenate([ok_cmp, ok_cmp], axis=0)
    t_l = p0 + lax.broadcasted_iota(I32, (1, QT), 1)
    j_s = lax.broadcasted_iota(I32, (n_sel, 1), 0)
    cur = t_l // SEL_BLOCK
    forced = (j_s == 0) | (j_s == cur) | (j_s == cur - 1)
    avail = j_s * SEL_BLOCK <= t_l
    m_near = p0 - QT + lax.broadcasted_iota(I32, (1, 2 * QT), 1)
    near_blk = (p0 - QT + lax.broadcasted_iota(I32, (n_sel, 2 * QT), 1)) // SEL_BLOCK
    e_near_neg = jnp.where(near_blk == j_s, NEG, 0.0).astype(BF16)
    causal_near = jnp.where((m_near >= 0) & (m_near <= t), 0.0, NEG)
    m_win = p0 - WINDOW + lax.broadcasted_iota(I32, (1, WBAND), 1)
    d_win = t - m_win
    add_win = jnp.where((m_win >= 0) & (d_win >= 0) & (d_win < WINDOW), 0.0, NEG)
    two = lambda x: jnp.concatenate([x, x], axis=0)

    KH = range(ATT_KV_HEADS)
    kcol = lambda kh, base: slice(base + 64 * kh, base + 64 * kh + 64)
    q2 = [(jnp.concatenate([q_ref[0, :, kcol(2 * kh, 0)], q_ref[0, :, kcol(2 * kh + 1, 0)]], axis=0)
           * ATT_SCALE).astype(BF16) for kh in KH]
    s_c = [_dot_nt(q2[kh], kc_ref[:, kcol(kh, 0)]) + jnp.concatenate([tabc_ref[2 * kh], tabc_ref[2 * kh + 1]], axis=0)
           for kh in KH]
    p_c = [_masked_softmax_parts([(s, ok_cmp2)])[0] for s in s_c]
    o_cmp = [_dot(p_c[kh].astype(BF16), vc_ref[:, kcol(kh, 0)]) for kh in KH]
    nsel_t = []
    for kh in KH:
        p3 = _split3(p_c[kh][0:QT] + p_c[kh][QT:2 * QT])
        imp_t = (_dot_nt(covert_ref[...], p3[0]) + _dot_nt(covert_ref[...], p3[1])
                 + _dot_nt(covert_ref[...], p3[2]))
        score_t = jnp.where(avail, imp_t + jnp.where(forced, FORCE, 0.0), NEG)
        nsel = 1.0 - _topn_select_rows(score_t, n_sel)
        nsel_ref[kh] = nsel
        nsel_t.append(nsel.astype(BF16))
    add_near = [_dot_tn(nsel_t[kh], e_near_neg) + causal_near for kh in KH]
    nb = FT // SEL_BLOCK
    e_tile = jnp.where(lax.broadcasted_iota(I32, (nb, FT), 1) // SEL_BLOCK == lax.broadcasted_iota(I32, (nb, FT), 0),
                       NEG, 0.0)
    near_rows = pl.ds(WINDOW + p0 - QT, 2 * QT)
    bias_near = [jnp.concatenate(
        [tabw_ref[2 * kh + g, :, WINDOW - QT:WINDOW + QT] - rb_ref[N_BUCKETS - 1, 2 * kh + g] + add_near[kh]
         for g in range(2)], axis=0) for kh in KH]
    s_n = [_dot_nt(q2[kh], kvp_ref[near_rows, kcol(kh, 0)]) + bias_near[kh] for kh in KH]
    m0 = [s.max(-1, keepdims=True) for s in s_n]
    e_n = [jnp.exp(s - m) for s, m in zip(s_n, m0)]
    init = tuple((m0[kh], e_n[kh].sum(-1, keepdims=True), _dot(e_n[kh].astype(BF16), kvp_ref[near_rows, kcol(kh, 128)]))
                 for kh in KH)

    def far(i, carry):
        k0 = pl.multiple_of(i * FT, FT)
        rows = pl.ds(WINDOW + k0, FT)
        lim = jnp.where(k0 + lax.broadcasted_iota(I32, (1, FT), 1) < p0 - QT, 0.0, NEG)
        blocks = pl.ds(pl.multiple_of(i * nb, nb), nb)
        s = [_dot_nt(q2[kh], kvp_ref[rows, kcol(kh, 0)]) + two(_dot_tn(nsel_ref[kh, blocks, :], e_tile) + lim)
             for kh in KH]
        m_new = [jnp.maximum(carry[kh][0], s[kh].max(-1, keepdims=True)) for kh in KH]
        alpha = [jnp.exp(carry[kh][0] - m_new[kh]) for kh in KH]
        e = [jnp.exp(s[kh] - m_new[kh]) for kh in KH]
        return tuple((m_new[kh], alpha[kh] * carry[kh][1] + e[kh].sum(-1, keepdims=True),
                      alpha[kh] * carry[kh][2] + _dot(e[kh].astype(BF16), kvp_ref[rows, kcol(kh, 128)])) for kh in KH)

    n_far = (jnp.maximum(p0 - QT, 0) + FT - 1) // FT
    fin = lax.fori_loop(0, n_far, far, init)
    o_sel = [fin[kh][2] / jnp.maximum(fin[kh][1], 1e-30) for kh in KH]
    win_rows = pl.ds(p0, WBAND)
    s_w = [_dot_nt(q2[kh], kvp_ref[win_rows, kcol(kh, 256)])
           + jnp.concatenate([tabw_ref[2 * kh] + add_win, tabw_ref[2 * kh + 1] + add_win], axis=0) for kh in KH]
    e_w = [jnp.exp(s - s.max(-1, keepdims=True)) for s in s_w]
    o_win = [_dot(e_w[kh].astype(BF16), kvp_ref[win_rows, kcol(kh, 384)])
             / jnp.maximum(e_w[kh].sum(-1, keepdims=True), 1e-30) for kh in KH]
    for h in range(ATT_HEADS):
        kh, g = divmod(h, 2)
        rows = slice(QT * g, QT * (g + 1))
        out = (gates[:, h:h + 1] * o_cmp[kh][rows] + gates[:, 4 + h:5 + h] * o_sel[kh][rows]
               + gates[:, 8 + h:9 + h] * o_win[kh][rows])
        y_ref[0, :, 64 * h:64 * h + 64] = out.astype(BF16)


def _nsa_prompt(h3, rel_bias, wcat, w2bd, pt, tabw, tabc, covert):
    B, T, _ = h3.shape
    ns = T // CMP_STRIDE
    n_sel = T // SEL_BLOCK
    ft = min(512, T)
    return pl.pallas_call(
        functools.partial(_nsa_prompt_body, T=T, FT=ft),
        grid=(B, T // QT),
        in_specs=[_smem_spec(),
                  pl.BlockSpec((1, QT, 256), lambda b, i: (b, i, C_Q // 256)),
                  pl.BlockSpec((1, QT, LANES), lambda b, i: (b, i, C_SM // LANES)),
                  pl.BlockSpec((1, T, 768), lambda b, i: (b, 0, C_KV // 768)),
                  _const_spec((2, CMP_STRIDE // 2, 2 * LANES, 2 * LANES)), _const_spec((2, LANES, LANES)),
                  _const_spec((2, 1, LANES)), _const_spec((ATT_HEADS, QT, WBAND)),
                  pl.BlockSpec((ATT_HEADS, QT, ns), lambda b, i: (0, i, 0)),
                  _const_spec((n_sel, ns))],
        out_specs=pl.BlockSpec((1, QT, 256), lambda b, i: (b, i, 0)),
        out_shape=jax.ShapeDtypeStruct((B, T, 256), BF16),
        scratch_shapes=[pltpu.VMEM((WINDOW + T, 512), BF16), pltpu.VMEM((ns, LANES), BF16),
                        pltpu.VMEM((ns, LANES), BF16), pltpu.VMEM((2, T, LANES), F32),
                        pltpu.VMEM((ATT_KV_HEADS, n_sel, QT), F32)],
        compiler_params=_cparams("arbitrary", "arbitrary"),
        name="nsa_prompt",
    )(rel_bias, h3, h3, h3, wcat, w2bd, pt, tabw, tabc, covert)


S1_SLOTS = 3


def _nsa_s1_body(pt_ref, cache_ref, wcat_ref, o_ref, buf_ref, row_ref, sem, *, layer, CH):
    s = pl.program_id(0)
    nsteps = pl.num_programs(0)
    slot = s % S1_SLOTS

    def page_copy(step, p, sl):
        phys = pt_ref[step * CH + p]
        return pltpu.make_async_copy(cache_ref.at[layer, phys, pl.ds(0, 2)], buf_ref.at[sl, p], sem.at[sl])

    def issue(step, sl):
        def one(p, carry):
            page_copy(step, p, sl).start()
            return carry
        lax.fori_loop(0, CH, one, 0)

    @pl.when(s == 0)
    def _():
        for k in range(S1_SLOTS - 1):
            @pl.when(k < nsteps)
            def _(k=k):
                issue(k, k)

    ahead = s + S1_SLOTS - 1

    @pl.when(ahead < nsteps)
    def _():
        issue(ahead, ahead % S1_SLOTS)

    def wait_one(p, carry):
        page_copy(s, p, slot).wait()
        return carry
    lax.fori_loop(0, CH, wait_one, 0)

    gp = 16 if CH % 16 == 0 else CH
    gsub = gp * LANES // CMP_STRIDE
    for g in range(CH // gp):
        for p in range(g * gp, (g + 1) * gp):
            for c in range(2):
                row_ref[c, LANES * p:LANES * (p + 1), :] = buf_ref[slot, p, c].reshape(2 * HEAD_DIM, LANES).T
        for c in range(2):
            hcat = _compress_pre(lambda r: row_ref[c, pl.ds(g * gp * LANES + r, gsub, stride=CMP_STRIDE), :], wcat_ref, c)
            o_ref[0, g * gsub:(g + 1) * gsub, 2 * LANES * c:2 * LANES * (c + 1)] = hcat


def _nsa_s1(page_flat, cache_t, wcat, layer, B, NP):
    CH = min(64, NP)
    nsub = CH * LANES // CMP_STRIDE
    per_b = NP // CH
    grid_spec = pltpu.PrefetchScalarGridSpec(
        num_scalar_prefetch=1,
        grid=(B * per_b,),
        in_specs=[pl.BlockSpec(memory_space=pl.ANY),
                  pl.BlockSpec((2, CMP_STRIDE // 2, 2 * LANES, 2 * LANES), lambda s, pt: (0, 0, 0, 0),
                               pipeline_mode=pl.Buffered(1))],
        out_specs=pl.BlockSpec((1, nsub, 4 * LANES), lambda s, pt: (s // per_b, s % per_b, 0)),
        scratch_shapes=[pltpu.VMEM((S1_SLOTS, CH, 2, 2, HEAD_DIM, LANES), F32), pltpu.VMEM((2, CH * LANES, LANES), F32),
                        pltpu.SemaphoreType.DMA((S1_SLOTS,))],
    )
    return pl.pallas_call(
        functools.partial(_nsa_s1_body, layer=layer, CH=CH),
        grid_spec=grid_spec,
        out_shape=jax.ShapeDtypeStruct((B, NP * LANES // CMP_STRIDE, 4 * LANES), F32),
        compiler_params=_cparams("arbitrary"),
        name="nsa_sample_compress",
    )(page_flat, cache_t, wcat)


def _row_bias(bk, rb_ref, kh, g_of_row):
    return _bias_lookup(bk, lambda k: jnp.where(g_of_row == 1, rb_ref[k, 2 * kh + 1], rb_ref[k, 2 * kh]))


MAX_SEL_PAGES = 64
SLOT_INFO = 5


def _nsa_s2_body(rb_ref, hc_ref, q_ref, w2_ref, pt_ref, cover_ref, pair_ref, ut_ref, ocmp_ref, slots_ref,
                 *, P, t_valid, NSELP):
    nsub = hc_ref.shape[1]
    kc = _compress_finish(hc_ref[0, :, 0:2 * LANES], pt_ref[0], w2_ref[0]).astype(BF16)
    vc = _compress_finish(hc_ref[0, :, 2 * LANES:4 * LANES], pt_ref[1], w2_ref[1]).astype(BF16)
    n_sel = P // SEL_BLOCK + 1
    row = lax.broadcasted_iota(I32, (16, 1), 0)
    t16 = row % 8
    g16 = row // 8
    n_i = lax.broadcasted_iota(I32, (1, nsub), 1)
    d_cmp = (P + t16) - (n_i * CMP_STRIDE + CMP_BLOCK - 1)
    ok_cmp = d_cmp >= 0
    bk_cmp = _t5_bucket(d_cmp)
    t8 = lax.broadcasted_iota(I32, (8, 1), 0)
    j_i = lax.broadcasted_iota(I32, (1, NSELP), 1)
    qpos = P + t8
    cur = qpos // SEL_BLOCK
    forced = (j_i == 0) | (j_i == cur) | (j_i == cur - 1)
    avail = (j_i * SEL_BLOCK <= qpos) & (j_i < n_sel)
    wts = jnp.where(t8 < t_valid, jnp.left_shift(1, t8 + 4 * (j_i % 2)), 0).astype(F32)
    NP = P // LANES
    page_l = lax.broadcasted_iota(I32, (1, 2 * LANES), 1)
    slot_s = lax.broadcasted_iota(I32, (MAX_SEL_PAGES, 1), 0).astype(F32)
    col = lax.broadcasted_iota(I32, (1, LANES), 1)
    pick = lambda x, lane_no: jnp.sum(jnp.where(page_l == lane_no, x, 0.0), axis=1, keepdims=True)
    scores = []
    for kh in range(ATT_KV_HEADS):
        q16 = jnp.concatenate([q_ref[0, :, 128 * kh:128 * kh + 64], q_ref[0, :, 128 * kh + 64:128 * kh + 128]],
                              axis=0).astype(BF16)
        s = _dot_nt(q16, kc[:, 64 * kh:64 * kh + 64]) * ATT_SCALE + _row_bias(bk_cmp, rb_ref, kh, g16)
        (p,) = _masked_softmax_parts([(s, ok_cmp)])
        o = _dot(p.astype(BF16), vc[:, 64 * kh:64 * kh + 64])
        ocmp_ref[0, :, 128 * kh:128 * kh + 64] = o[0:8]
        ocmp_ref[0, :, 128 * kh + 64:128 * kh + 128] = o[8:16]
        p3 = _split3(p[0:8] + p[8:16])
        imp = _dot(p3[0], cover_ref[...]) + _dot(p3[1], cover_ref[...]) + _dot(p3[2], cover_ref[...])
        scores.append(jnp.where(avail, imp + jnp.where(forced, FORCE, 0.0), NEG))
    sel_all = _topn_select(jnp.concatenate(scores, axis=0), n_sel)
    for kh in range(ATT_KV_HEADS):
        sel = sel_all[8 * kh:8 * kh + 8]
        colsum = jnp.sum(sel * wts, axis=0, keepdims=True)
        urow = _dot(jnp.broadcast_to(colsum, (8, NSELP)).astype(BF16), pair_ref[...])[0:1]
        nz = jnp.where((urow > 0.5) & (page_l < NP - 1), 1.0, 0.0)
        before = _dot(jnp.broadcast_to(nz, (8, 2 * LANES)).astype(BF16), ut_ref[...])[0:1]
        in_slot = (nz > 0.5) & (before == slot_s)
        page_of = jnp.sum(jnp.where(in_slot, page_l.astype(F32), 0.0), axis=1, keepdims=True)
        bits_of = jnp.sum(jnp.where(in_slot, urow, 0.0), axis=1, keepdims=True)
        count = jnp.sum(nz, axis=1, keepdims=True)
        info = jnp.where(col == 0, page_of, jnp.where(col == 1, bits_of, jnp.where(
            col == 2, count, jnp.where(col == 3, pick(urow, NP - 1), jnp.where(col == 4, pick(urow, NP), 0.0)))))
        slots_ref[0, MAX_SEL_PAGES * kh:MAX_SEL_PAGES * (kh + 1), :] = info.astype(I32)


def _nsa_s2(hs3, hc, rel_bias, w2bd, pt, cover, pair, P, t_valid):
    B, T, _ = hs3.shape
    nsub = hc.shape[1]
    nselp = cover.shape[1]
    ut = (jnp.arange(2 * LANES)[:, None] < jnp.arange(2 * LANES)[None, :]).astype(BF16)
    return pl.pallas_call(
        functools.partial(_nsa_s2_body, P=P, t_valid=t_valid, NSELP=nselp),
        grid=(B,),
        in_specs=[_smem_spec(),
                  pl.BlockSpec((1, nsub, 4 * LANES), lambda b: (b, 0, 0)),
                  pl.BlockSpec((1, T, 256), lambda b: (b, 0, C_Q // 256)),
                  _const_spec((2, LANES, LANES)), _const_spec((2, 1, LANES)),
                  _const_spec((nsub, nselp)), _const_spec((nselp, 2 * LANES)), _const_spec((2 * LANES, 2 * LANES))],
        out_specs=[pl.BlockSpec((1, 8, 256), lambda b: (b, 0, 0)),
                   pl.BlockSpec((1, ATT_KV_HEADS * MAX_SEL_PAGES, LANES), lambda b: (b, 0, 0))],
        out_shape=[jax.ShapeDtypeStruct((B, 8, 256), F32),
                   jax.ShapeDtypeStruct((B, ATT_KV_HEADS * MAX_SEL_PAGES, LANES), I32)],
        compiler_params=_cparams("arbitrary"),
        name="nsa_sample_cmp_select",
    )(rel_bias, hc, hs3, w2bd, pt, cover, pair, ut)


def _nsa_s3_body(pt_ref, sl_ref, rb_ref, q_ref, sm_ref, kvn_ref, ocmp_ref, cache_ref, win_ref, y_ref,
                 kcat_ref, vcat_ref, rec_ref, sem, *, layer, P, NP, t_valid):
    b = pl.program_id(0)

    def slot_info(kh, i, field):
        return sl_ref[((b * ATT_KV_HEADS + kh) * MAX_SEL_PAGES + i) * SLOT_INFO + field]

    wb = win_ref.shape[-1]
    ncat = MAX_SEL_PAGES * LANES
    row = lax.broadcasted_iota(I32, (16, 1), 0)
    t16 = row % 8
    g16 = row // 8
    valid_row = t16 < t_valid
    gates = _sigmoid(sm_ref[0])

    @pl.when(b == 0)
    def _():
        kcat_ref[...] = jnp.zeros(kcat_ref.shape, F32)
        vcat_ref[...] = jnp.zeros(vcat_ref.shape, F32)

    rec_copy = pltpu.make_async_copy(cache_ref.at[layer, pt_ref[b * NP + NP - 1], pl.ds(2, 2)], rec_ref, sem.at[1])
    rec_copy.start()

    def kv_copies(kh, phys, i):
        dst = pl.ds(pl.multiple_of(i * LANES, LANES), LANES)
        return (pltpu.make_async_copy(cache_ref.at[layer, phys, 2, kh], kcat_ref.at[kh, :, dst], sem.at[0]),
                pltpu.make_async_copy(cache_ref.at[layer, phys, 3, kh], vcat_ref.at[kh, :, dst], sem.at[0]))

    counts = [slot_info(kh, 0, 2) for kh in range(ATT_KV_HEADS)]
    for kh in range(ATT_KV_HEADS):
        def issue(i, carry, kh=kh):
            ck, cv = kv_copies(kh, pt_ref[b * NP + slot_info(kh, i, 0)], i)
            ck.start()
            cv.start()
            return carry
        lax.fori_loop(0, counts[kh], issue, 0)

    q16fs = [jnp.concatenate([q_ref[0, :, 128 * kh:128 * kh + 64], q_ref[0, :, 128 * kh + 64:128 * kh + 128]], axis=0)
             for kh in range(ATT_KV_HEADS)]

    o_wins = []
    for kh in range(ATT_KV_HEADS):
        q16f = q16fs[kh]
        q16 = q16f.astype(BF16)
        col_w = lax.broadcasted_iota(I32, (1, wb), 1)
        d_w = wb + t16 - col_w
        ok_w = (d_w < WINDOW) & valid_row
        s_w = _dot(q16, win_ref[0, 0, kh].astype(BF16)) * ATT_SCALE + _row_bias(_t5_bucket(d_w), rb_ref, kh, g16)
        s_w = jnp.where(ok_w, s_w, NEG)
        mw = s_w.max(-1, keepdims=True)
        cols = []
        for c in range(t_valid):
            kn = kvn_ref[0, c:c + 1, 512 + 64 * kh:512 + 64 * kh + 64]
            sc = jnp.sum(q16f * kn, axis=-1, keepdims=True) * ATT_SCALE + _row_bias(_t5_bucket(t16 - c), rb_ref, kh, g16)
            ok = (t16 >= c) & valid_row
            sc = jnp.where(ok, sc, NEG)
            cols.append((sc, ok))
            mw = jnp.maximum(mw, sc)
        e_w = jnp.where(ok_w, jnp.exp(s_w - mw), 0.0)
        l_w = e_w.sum(-1, keepdims=True)
        acc_w = _dot_nt(e_w.astype(BF16), win_ref[0, 1, kh].astype(BF16))
        for c, (sc, ok) in enumerate(cols):
            e = jnp.where(ok, jnp.exp(sc - mw), 0.0)
            l_w = l_w + e
            acc_w = acc_w + e * kvn_ref[0, c:c + 1, 640 + 64 * kh:640 + 64 * kh + 64]
        o_wins.append(acc_w / jnp.maximum(l_w, 1e-30))

    for kh in range(ATT_KV_HEADS):
        def wait_pair(i, carry, kh=kh):
            ck, cv = kv_copies(kh, 0, i)
            ck.wait()
            cv.wait()
            return carry
        lax.fori_loop(0, counts[kh], wait_pair, 0)
    rec_copy.wait()

    lane_c = lax.broadcasted_iota(I32, (1, ncat), 1)
    shift_c = t16 + 4 * ((lane_c % LANES) // SEL_BLOCK)
    lane = lax.broadcasted_iota(I32, (1, LANES), 1)
    shift = t16 + 4 * (lane // SEL_BLOCK)
    for kh in range(ATT_KV_HEADS):
        q16f = q16fs[kh]
        q16 = q16f.astype(BF16)
        c_far = jnp.where(g16 == 1, rb_ref[N_BUCKETS - 1, 2 * kh + 1], rb_ref[N_BUCKETS - 1, 2 * kh])

        u_vec = jnp.concatenate([jnp.full((1, LANES), slot_info(kh, i, 1), I32) for i in range(MAX_SEL_PAGES)], axis=1)
        ok_c = (jnp.bitwise_and(jnp.right_shift(u_vec, shift_c), 1) == 1) & valid_row
        s_c = jnp.where(ok_c, _dot(q16, kcat_ref[kh].astype(BF16)) * ATT_SCALE + c_far, NEG)
        u_rec = slot_info(kh, 0, 3)
        d_r = (P + t16) - ((NP - 1) * LANES + lane)
        ok_r = (jnp.bitwise_and(jnp.right_shift(u_rec, shift), 1) == 1) & valid_row
        s_r = _dot(q16, rec_ref[0, kh].astype(BF16)) * ATT_SCALE + _row_bias(_t5_bucket(d_r), rb_ref, kh, g16)
        s_r = jnp.where(ok_r, s_r, NEG)
        u_new = slot_info(kh, 0, 4)
        sel_new = jnp.bitwise_and(jnp.right_shift(u_new, t16), 1) == 1
        m_fin = jnp.maximum(s_c.max(-1, keepdims=True), s_r.max(-1, keepdims=True))
        cols = []
        for c in range(t_valid):
            kn = kvn_ref[0, c:c + 1, 256 + 64 * kh:256 + 64 * kh + 64]
            sc = jnp.sum(q16f * kn, axis=-1, keepdims=True) * ATT_SCALE + _row_bias(_t5_bucket(t16 - c), rb_ref, kh, g16)
            ok = sel_new & (t16 >= c) & valid_row
            sc = jnp.where(ok, sc, NEG)
            cols.append((sc, ok))
            m_fin = jnp.maximum(m_fin, sc)
        e_c = jnp.where(ok_c, jnp.exp(s_c - m_fin), 0.0)
        e_r = jnp.where(ok_r, jnp.exp(s_r - m_fin), 0.0)
        l_fin = e_c.sum(-1, keepdims=True) + e_r.sum(-1, keepdims=True)
        acc = (_dot_nt(e_c.astype(BF16), vcat_ref[kh].astype(BF16))
               + _dot_nt(e_r.astype(BF16), rec_ref[1, kh].astype(BF16)))
        for c, (sc, ok) in enumerate(cols):
            e = jnp.where(ok, jnp.exp(sc - m_fin), 0.0)
            l_fin = l_fin + e
            acc = acc + e * kvn_ref[0, c:c + 1, 384 + 64 * kh:384 + 64 * kh + 64]
        o_sel = acc / jnp.maximum(l_fin, 1e-30)
        o_win = o_wins[kh]

        for g in range(2):
            h = 2 * kh + g
            out = (gates[:, h:h + 1] * ocmp_ref[0, :, 64 * h:64 * h + 64]
                   + gates[:, 4 + h:5 + h] * o_sel[8 * g:8 * g + 8]
                   + gates[:, 8 + h:9 + h] * o_win[8 * g:8 * g + 8])
            y_ref[0, :, 64 * h:64 * h + 64] = out.astype(BF16)


def _nsa_s3(page_flat, slots_flat, rel_bias, hs3, ocmp, cache_t, win_t, layer, P, NP, t_valid):
    B, T, _ = hs3.shape
    wb = win_t.shape[-1]
    grid_spec = pltpu.PrefetchScalarGridSpec(
        num_scalar_prefetch=2,
        grid=(B,),
        in_specs=[_smem_spec(),
                  pl.BlockSpec((1, T, 256), lambda b, *_: (b, 0, C_Q // 256)),
                  pl.BlockSpec((1, T, LANES), lambda b, *_: (b, 0, C_SM // LANES)),
                  pl.BlockSpec((1, T, 768), lambda b, *_: (b, 0, C_KV // 768)),
                  pl.BlockSpec((1, 8, 256), lambda b, *_: (b, 0, 0)),
                  pl.BlockSpec(memory_space=pl.ANY),
                  pl.BlockSpec((1, 2, 2, HEAD_DIM, wb), lambda b, *_: (b, 0, 0, 0, 0))],
        out_specs=pl.BlockSpec((1, T, 256), lambda b, *_: (b, 0, 0)),
        scratch_shapes=[pltpu.VMEM((2, HEAD_DIM, MAX_SEL_PAGES * LANES), F32),
                        pltpu.VMEM((2, HEAD_DIM, MAX_SEL_PAGES * LANES), F32),
                        pltpu.VMEM((2, 2, HEAD_DIM, LANES), F32),
                        pltpu.SemaphoreType.DMA((2,))],
    )
    return pl.pallas_call(
        functools.partial(_nsa_s3_body, layer=layer, P=P, NP=NP, t_valid=t_valid),
        grid_spec=grid_spec,
        out_shape=jax.ShapeDtypeStruct((B, T, 256), BF16),
        compiler_params=_cparams("arbitrary"),
        name="nsa_sample_select_window",
    )(page_flat, slots_flat, rel_bias, hs3, hs3, hs3, ocmp, cache_t, win_t)


def _per_layer_specs(depth, block, index_of_step):
    def spec(k):
        def index_map(l, i):
            idx = index_of_step(i)
            return tuple(jnp.where(l == k, v, 0) if n < 2 else v for n, v in enumerate(idx))
        return pl.BlockSpec(block, index_map)
    return [spec(k) for k in range(depth)]


def _kv_pages_body(*refs):
    o_ref = refs[-1]
    for k, h_ref in enumerate(refs[:-1]):
        @pl.when(pl.program_id(0) == k)
        def _(h_ref=h_ref):
            for j in range(o_ref.shape[1]):
                for c in range(4):
                    x = h_ref[0, LANES * j:LANES * (j + 1), LANES * c:LANES * (c + 1)]
                    o_ref[0, j, c] = x.T.reshape(ATT_KV_HEADS, HEAD_DIM, LANES)


def _kv_pages_t(hs):
    depth = len(hs)
    B, T, _ = hs[0].shape
    rows = min(8 * LANES, T)
    per_b = T // rows
    return pl.pallas_call(
        _kv_pages_body,
        grid=(depth, B * per_b),
        in_specs=_per_layer_specs(depth, (1, rows, 512), lambda i: (i // per_b, i % per_b, 0)),
        out_specs=pl.BlockSpec((1, rows // LANES, 4, ATT_KV_HEADS, HEAD_DIM, LANES), lambda l, i: (l, i, 0, 0, 0, 0)),
        out_shape=jax.ShapeDtypeStruct((depth, B * T // LANES, 4, ATT_KV_HEADS, HEAD_DIM, LANES), F32),
        compiler_params=_cparams("arbitrary", "arbitrary"),
        name="kv_pages_token_minor",
    )(*hs)


def _win_rows_body(*refs):
    o_ref = refs[-1]
    for k, h_ref in enumerate(refs[:-1]):
        @pl.when(pl.program_id(0) == k)
        def _(h_ref=h_ref):
            for c in range(2):
                x = h_ref[0, :, LANES * c:LANES * (c + 1)]
                o_ref[0, 0, c] = x.T.reshape(ATT_KV_HEADS, HEAD_DIM, x.shape[0])


def _win_rows_t(hs, wrows):
    depth = len(hs)
    B, T, _ = hs[0].shape
    last = T // wrows - 1
    return pl.pallas_call(
        _win_rows_body,
        grid=(depth, B),
        in_specs=_per_layer_specs(depth, (1, wrows, 256), lambda b: (b, last, (C_KV + 512) // 256)),
        out_specs=pl.BlockSpec((1, 1, 2, ATT_KV_HEADS, HEAD_DIM, wrows), lambda l, b: (l, b, 0, 0, 0, 0)),
        out_shape=jax.ShapeDtypeStruct((depth, B, 2, ATT_KV_HEADS, HEAD_DIM, wrows), F32),
        compiler_params=_cparams("arbitrary", "arbitrary"),
        name="win_rows_token_minor",
    )(*hs)


def _blockdiag2(w):
    z = jnp.zeros_like(w)
    return jnp.concatenate([jnp.concatenate([w, z], axis=-1), jnp.concatenate([z, w], axis=-1)], axis=-2)


def _prep_layer(w_in, pool_w, cmp_w1, cmp_w2, alog, dtb):
    glu, pool, q, kv, gate, gqkv, z, a, b = jnp.split(w_in, [512, 768, 1024, 1792, 1804, 2572, 2828, 2832], axis=1)
    pad = jnp.zeros((D_MODEL, IN_PAD - C_SM - 20), F32)
    w_in_p = jnp.concatenate([kv, gqkv, glu, pool, q, z, gate, a, b, pad], axis=1).astype(BF16)
    wblk = jnp.zeros((POOL_CH, POOL_CH), F32)
    for gi in range(4):
        wblk = wblk.at[64 * gi:64 * gi + 64, 64 * gi:64 * gi + 64].set(pool_w[gi])
    w1 = cmp_w1.reshape(2, CMP_BLOCK, HEAD_DIM, HEAD_DIM)
    wcat = jnp.concatenate([_blockdiag2(w1[:, :CMP_STRIDE]), _blockdiag2(w1[:, CMP_STRIDE:])], axis=-1).astype(BF16)
    wcat = wcat.reshape(2, CMP_STRIDE // 2, 2 * LANES, 2 * LANES)
    w2bd = _blockdiag2(cmp_w2).astype(BF16)
    lane_pad = lambda v: jnp.zeros((1, LANES), F32).at[0, SM_A:SM_A + GDN_HEADS].set(v)
    return w_in_p, wblk.astype(BF16), wcat, w2bd, lane_pad(alog), lane_pad(dtb)


def _cover_matrix(nsub, n_sel, cols):
    n = jnp.arange(nsub)[:, None] * CMP_STRIDE
    j = jnp.arange(cols)[None, :] * SEL_BLOCK
    cov = (n < j + SEL_BLOCK) & (n + CMP_BLOCK > j) & (jnp.arange(cols)[None, :] < n_sel) & (jnp.arange(nsub)[:, None] < nsub - 1)
    return cov.astype(BF16)


def kernel(x_prompt, x_sample, cache_nsa_kv, cache_win_kv, state_conv, state_pool, state_gdn_conv, state_gdn,
           page_table, w_in, conv_dw, conv_dw_b, conv_ln_g, conv_ln_b, conv_pw, pool_w, pool_scale,
           cmp_pe, cmp_w1, cmp_w2, gdn_conv_w, gdn_a_log, gdn_dt_bias, gdn_norm_g,
           w_out, ln1_g, ln1_b, w_up, w_down, ln2_g, ln2_b, rel_bias):
    depth = w_in.shape[0]
    BP, T, _ = x_prompt.shape
    BS, TS, _ = x_sample.shape
    NP = page_table.shape[1]
    page = cache_nsa_kv.shape[2]
    P = NP * page
    TSP = 8
    assert page == LANES and TS <= TSP and TS < CMP_STRIDE and T % QT == 0 and P % SEL_BLOCK == 0

    cache_t = jnp.transpose(cache_nsa_kv, (0, 1, 3, 4, 5, 2))
    win_t = jnp.transpose(cache_win_kv, (0, 1, 3, 4, 5, 2))
    page_flat = page_table.reshape(-1)

    tabw, tabc = _bias_tables(rel_bias, T)
    pterm = _peterm(cmp_pe.reshape(depth * 2, 1, CMP_BLOCK * HEAD_DIM), cmp_w1.reshape(depth * 2, CMP_BLOCK * HEAD_DIM, HEAD_DIM))
    pterm = pterm[:, 0:1, :].reshape(depth, 2, 1, HEAD_DIM)
    pterm = jnp.concatenate([pterm, pterm], axis=-1)

    ns_p, nsel_p = T // CMP_STRIDE, T // SEL_BLOCK
    covert_p = _cover_matrix(ns_p, nsel_p, nsel_p).T
    ns_s = P // CMP_STRIDE
    nsel_s = P // SEL_BLOCK + 1
    nselp_s = -(-nsel_s // LANES) * LANES
    cover_s = _cover_matrix(ns_s + 1, nsel_s, nselp_s)[:ns_s]
    jj = jnp.arange(nselp_s)[:, None]
    pp = jnp.arange(2 * LANES)[None, :]
    pair = (((jj // 2 == pp) & (jj < 2 * NP)) | ((jj == 2 * NP) & (pp == NP))).astype(BF16)

    yp = x_prompt.reshape(BP * T, D_MODEL)
    ys = jnp.pad(x_sample, ((0, 0), (0, TSP - TS), (0, 0))).reshape(BS * TSP, D_MODEL)
    zeros = lambda *s: jnp.zeros(s, F32)
    outs_p = [[] for _ in range(6)]
    outs_s = [[] for _ in range(6)]
    hps = []
    preps = [_prep_layer(w_in[l], pool_w[l], cmp_w1[l], cmp_w2[l], gdn_a_log[l], gdn_dt_bias[l]) for l in range(depth)]
    hcs = [_nsa_s1(page_flat, cache_t, preps[l][2], l, BS, NP) for l in range(depth)]
    for l in range(depth):
        w_in_p, wblk, wcat, w2bd, alog_l, dtb_l = preps[l]
        lw = dict(w_out=w_out[l].astype(BF16), ln1_g=ln1_g[l][None], ln1_b=ln1_b[l][None],
                  w_up=w_up[l].astype(BF16), w_down=w_down[l].astype(BF16), ln2_g=ln2_g[l][None], ln2_b=ln2_b[l][None])
        conv_w = (conv_dw[l], conv_dw_b[l][None], conv_ln_g[l][None], conv_ln_b[l][None], conv_pw[l].astype(BF16))
        gdn_w = (gdn_conv_w[l], alog_l, dtb_l, gdn_norm_g[l][None])

        hp = _proj_in(yp, w_in_p, min(1024, BP * T)).reshape(BP, T, IN_PAD)
        m_conv, conv_new = _conv_mixer(hp, zeros(BP, CONV_WIDTH - 1, CONV_CH), *conv_w, t_valid=T)
        m_pool, pool_new = _pool_mixer(hp, zeros(BP, POOL_BUF, POOL_CH), wblk, pool_scale[l][None], t_valid=T, offset=0)
        m_att = _nsa_prompt(hp, rel_bias, wcat, w2bd, pterm[l], tabw, tabc, covert_p)
        m_gdn, gbuf_new, s_new = _gdn_mixer(hp, zeros(BP, GDN_CONV - 1, GDN_QKV), zeros(BP, GDN_HEADS, GDN_DK, GDN_DV),
                                            *gdn_w, t_valid=T)
        mixes = [m.reshape(BP * T, GROUP_WIDTH) for m in (m_conv, m_pool, m_att, m_gdn)]
        yp = _out_ffn(yp, mixes, lw, 512)
        hps.append(hp)
        for lst, arr in zip(outs_p[2:], (conv_new, pool_new, gbuf_new, s_new)):
            lst.append(arr)

        hs = _proj_in(ys, w_in_p, BS * TSP).reshape(BS, TSP, IN_PAD)
        m_conv, conv_new = _conv_mixer(hs, state_conv[l], *conv_w, t_valid=TS)
        m_pool, pool_new = _pool_mixer(hs, state_pool[l], wblk, pool_scale[l][None], t_valid=TS, offset=P)
        ocmp, slots = _nsa_s2(hs, hcs[l], rel_bias, w2bd, pterm[l], cover_s, pair, P, TS)
        m_att = _nsa_s3(page_flat, slots[:, :, :SLOT_INFO].reshape(-1), rel_bias, hs, ocmp, cache_t, win_t[l], l, P, NP, TS)
        m_gdn, gbuf_new, s_new = _gdn_mixer(hs, state_gdn_conv[l], state_gdn[l], *gdn_w, t_valid=TS)
        mixes = [m.reshape(BS * TSP, GROUP_WIDTH) for m in (m_conv, m_pool, m_att, m_gdn)]
        ys = _out_ffn(ys, mixes, lw, BS * TSP)
        kv_new = hs[:, :TS, C_KV:C_KV + 768].reshape(BS, TS, 6, ATT_KV_HEADS, HEAD_DIM)
        outs_s[0].append(kv_new[:, :, :4])
        kw_all = jnp.concatenate([cache_win_kv[l], kv_new[:, :, 4:]], axis=1)
        outs_s[1].append(kw_all[:, -min(WINDOW, kw_all.shape[1]):])
        for lst, arr in zip(outs_s[2:], (conv_new, pool_new, gbuf_new, s_new)):
            lst.append(arr)

    p_nsa = jnp.transpose(_kv_pages_t(hps), (0, 1, 5, 2, 3, 4))
    p_win = jnp.transpose(_win_rows_t(hps, min(WINDOW, T)), (0, 1, 5, 2, 3, 4))
    p_rest = [p_win] + [jnp.stack(a) for a in outs_p[2:]]
    s_all = [jnp.stack(a) for a in outs_s]
    y_s = ys.reshape(BS, TSP, D_MODEL)[:, :TS]
    return (yp.reshape(BP, T, D_MODEL), y_s, p_nsa, *p_rest, *s_all)
```
